```python
import math
import jax
import jax.numpy as jnp
from jax import lax
import numpy as np

D_MODEL = 1024
BATCH = 32
SEQ = 256
DEPTH = 1
DEC_BATCH = 8
DEC_SEQ = 1024
PAST_LEN = 512

GRID_W = 64
N_HEADS = 8
HEAD_DIM = D_MODEL // (2 * N_HEADS)
ATTN_W = N_HEADS * 2 * HEAD_DIM
POOL_GROUPS = 4
POOL_WINDOWS = (2, 4, 8, 16)
POOL_W = D_MODEL // 2
POOL_GROUP_W = POOL_W // POOL_GROUPS
N_BRANCHES = 2
IN_W = 3 * ATTN_W + POOL_W + N_BRANCHES * D_MODEL
ROPE_THETA = 10000.0
ROPE_AXIS_DIM = HEAD_DIM // 2
ROPE_PAIRS = ROPE_AXIS_DIM // 2
Q_BLOCK = 128
N_GROUPS = 4
EXPERTS_PER_GROUP = 4
N_EXPERTS = N_GROUPS * EXPERTS_PER_GROUP
TOP_K_IN_GROUP = 2
D_EXPERT = D_MODEL // 2
ADA_CHUNKS = 6
EPS = 1e-6

kernel_name = 'hybrid_diffattn_pool_hmoe_prefix_step'


def _rmsnorm(x, g):
    xf = x.astype(jnp.float32)
    y = xf * lax.rsqrt(jnp.mean(xf * xf, axis=-1, keepdims=True) + EPS)
    return (y * g.astype(jnp.float32)).astype(x.dtype)


def _adaln(cond, w, b):
    m = jax.nn.silu(cond) @ w + b
    return m.reshape(cond.shape[0], ADA_CHUNKS, D_MODEL)


def _modulate(h, shift, scale):
    return h * (1 + scale[:, None, :]) + shift[:, None, :]


def _axial_rope_tables(n_tokens):
    rows = n_tokens // GRID_W
    row_ids = jnp.repeat(jnp.arange(rows, dtype=jnp.float32), GRID_W)
    col_ids = jnp.tile(jnp.arange(GRID_W, dtype=jnp.float32), rows)
    inv_freq = jnp.power(ROPE_THETA, -jnp.arange(ROPE_PAIRS, dtype=jnp.float32) / ROPE_PAIRS)
    ang_r = row_ids[:, None] * inv_freq[None, :]
    ang_c = col_ids[:, None] * inv_freq[None, :]
    ang = jnp.concatenate([ang_r, ang_r, ang_c, ang_c], axis=-1)
    return jnp.cos(ang), jnp.sin(ang)


def _rotate_half_axial(x):
    xs = x.reshape(x.shape[:-1] + (2, 2, ROPE_PAIRS))
    return jnp.concatenate([-xs[..., 1:, :], xs[..., :1, :]], axis=-2).reshape(x.shape)


def _apply_rope(x, cos, sin):
    xf = x.astype(jnp.float32)
    c = cos[None, :, None, None, :]
    s = sin[None, :, None, None, :]
    return (xf * c + _rotate_half_axial(xf) * s).astype(x.dtype)


def _diff_attention(q, k, v, lam):
    B, N = q.shape[0], q.shape[1]
    nb = N // Q_BLOCK
    qb = jnp.moveaxis(q.reshape(B, nb, Q_BLOCK, N_HEADS, 2, HEAD_DIM), 1, 0)
    kf = k.astype(jnp.float32)
    vf = v.astype(jnp.float32)
    scale = HEAD_DIM ** -0.5

    def one_block(qblk):
        s = jnp.einsum('bqhid,bkhid->bihqk', qblk.astype(jnp.float32), kf) * scale
        a = jax.nn.softmax(s, axis=-1)
        w = a[:, 0] - lam * a[:, 1]
        return jnp.einsum('bhqk,bkhe->bqhe', w, vf)

    o = lax.map(one_block, qb)
    return jnp.moveaxis(o, 0, 1).reshape(B, N, N_HEADS, 2 * HEAD_DIM).astype(v.dtype)


def _pool_mixer(p, pool_w, pool_scale):
    B, N, _ = p.shape
    pf = p.astype(jnp.float32).reshape(B, N, POOL_GROUPS, POOL_GROUP_W)
    csum = jnp.concatenate([jnp.zeros((B, 1, POOL_GROUPS, POOL_GROUP_W), jnp.float32),
                            jnp.cumsum(pf, axis=1)], axis=1)
    t = jnp.arange(N)
    outs = []
    for gi, w in enumerate(POOL_WINDOWS):
        half = w // 2
        lo = jnp.clip(t - half, 0, N)
        hi = jnp.clip(t + half, 0, N)
        cs = csum[:, :, gi]
        sums = cs[:, hi] - cs[:, lo]
        cnt = (hi - lo).astype(jnp.float32)[None, :, None]
        outs.append(sums / cnt - pf[:, :, gi])
    pooled = jnp.stack(outs, axis=2)
    mixed = jnp.einsum('bngc,gce->bnge', pooled, pool_w.astype(jnp.float32))
    return (mixed.reshape(B, N, POOL_W) * pool_scale.astype(jnp.float32)).astype(p.dtype)


def _hier_moe(h, lp):
    B, N, D = h.shape
    t = h.reshape(B * N, D)
    g_logits = (t @ lp['router_group_w'] + lp['router_group_b']).astype(jnp.float32)
    g_prob = jax.nn.softmax(g_logits, axis=-1)
    g_idx = jnp.argmax(g_logits, axis=-1)
    g_w = jnp.max(g_prob, axis=-1, keepdims=True)
    e_logits = (t @ lp['router_expert_w'] + lp['router_expert_b']).astype(jnp.float32)
    e_logits = e_logits.reshape(B * N, N_GROUPS, EXPERTS_PER_GROUP)
    e_sel = e_logits[jnp.arange(B * N), g_idx]
    top_v, top_i = lax.top_k(e_sel, TOP_K_IN_GROUP)
    wts = jax.nn.softmax(top_v, axis=-1) * g_w
    ids = g_idx[:, None] * EXPERTS_PER_GROUP + top_i
    gates = jnp.sum(jax.nn.one_hot(ids, N_EXPERTS, dtype=jnp.float32) * wts[..., None], axis=1)
    a = jnp.einsum('td,edf->tef', t, lp['expert_w_gate'])
    u = jnp.einsum('td,edf->tef', t, lp['expert_w_up'])
    hid = jax.nn.silu(a) * u * gates[..., None].astype(h.dtype)
    out = jnp.einsum('tef,efd->td', hid, lp['expert_w_down'])
    return out.reshape(B, N, D)


def _layer(x, mod, lp, lambda_init, rope, ctx_k, ctx_v):
    B, N, _ = x.shape
    shift1, scale1, gate1, shift2, scale2, gate2 = [mod[:, i] for i in range(ADA_CHUNKS)]
    h = _modulate(_rmsnorm(x, lp['norm1_g']), shift1, scale1)
    z = h @ lp['w_in']
    q, k, v, p, gl = jnp.split(z, [ATTN_W, 2 * ATTN_W, 3 * ATTN_W, 3 * ATTN_W + POOL_W], axis=-1)
    q = _rmsnorm(q.reshape(B, N, N_HEADS, 2, HEAD_DIM), lp['q_norm_g'])
    k = _rmsnorm(k.reshape(B, N, N_HEADS, 2, HEAD_DIM), lp['k_norm_g'])
    v = v.reshape(B, N, N_HEADS, 2 * HEAD_DIM)
    if rope is not None:
        q = _apply_rope(q, rope[0], rope[1])
        k = _apply_rope(k, rope[0], rope[1])
    if ctx_k is not None:
        k_all = jnp.concatenate([k, ctx_k.astype(k.dtype)], axis=1)
        v_all = jnp.concatenate([v, ctx_v.astype(v.dtype)], axis=1)
    else:
        k_all, v_all = k, v
    lam = (jnp.exp(jnp.sum(lp['lambda_q1'].astype(jnp.float32) * lp['lambda_k1'].astype(jnp.float32)))
           - jnp.exp(jnp.sum(lp['lambda_q2'].astype(jnp.float32) * lp['lambda_k2'].astype(jnp.float32)))
           + lambda_init)
    o = _diff_attention(q, k_all, v_all, lam)
    o = _rmsnorm(o, lp['subln_g']) * (1.0 - lambda_init)
    attn_out = o.reshape(B, N, ATTN_W) @ lp['w_br_attn']
    pool_out = _pool_mixer(p, lp['pool_w'], lp['pool_scale']) @ lp['w_br_pool']
    g = jax.nn.sigmoid((gl + lp['b_gate']).astype(jnp.float32)).astype(x.dtype)
    g = g.reshape(B, N, N_BRANCHES, D_MODEL)
    merged = g[..., 0, :] * attn_out + g[..., 1, :] * pool_out
    x = x + gate1[:, None, :] * (merged @ lp['w_out'])
    h2 = _modulate(_rmsnorm(x, lp['norm2_g']), shift2, scale2)
    x = x + gate2[:, None, :] * _hier_moe(h2, lp)
    return x, k, v


def setup_inputs(seed: int = 0) -> dict:
    key = jax.random.key(seed)
    ks = jax.random.split(key, 32)

    def nrm(k, shape, s):
        return jax.random.normal(k, shape, jnp.float32) * s

    def gain(k, shape):
        return 1.0 + nrm(k, shape, 0.02)

    D = D_MODEL
    return {
        'x_prompt': nrm(ks[0], (BATCH, SEQ, D), 1.0),
        'x_sample': nrm(ks[1], (DEC_BATCH, DEC_SEQ, D), 1.0),
        'c': nrm(ks[2], (DEC_BATCH, D), 1.0),
        'cache_k': nrm(ks[3], (DEC_BATCH, DEPTH, PAST_LEN, N_HEADS, 2 * HEAD_DIM), 1.0),
        'cache_v': nrm(ks[4], (DEC_BATCH, DEPTH, PAST_LEN, N_HEADS, 2 * HEAD_DIM), 1.0),
        'c_ctx': nrm(ks[5], (D,), 1.0),
        'w_ada': nrm(ks[6], (DEPTH, D, ADA_CHUNKS * D), 0.5 * D ** -0.5),
        'b_ada': nrm(ks[7], (DEPTH, ADA_CHUNKS * D), 0.02),
        'norm1_g': gain(ks[8], (DEPTH, D)),
        'w_in': nrm(ks[9], (DEPTH, D, IN_W), D ** -0.5),
        'b_gate': nrm(ks[10], (DEPTH, N_BRANCHES * D), 0.02),
        'q_norm_g': gain(ks[11], (DEPTH, HEAD_DIM)),
        'k_norm_g': gain(ks[12], (DEPTH, HEAD_DIM)),
        'lambda_q1': nrm(ks[13], (DEPTH, HEAD_DIM), 0.1),
        'lambda_k1': nrm(ks[14], (DEPTH, HEAD_DIM), 0.1),
        'lambda_q2': nrm(ks[15], (DEPTH, HEAD_DIM), 0.1),
        'lambda_k2': nrm(ks[16], (DEPTH, HEAD_DIM), 0.1),
        'subln_g': gain(ks[17], (DEPTH, 2 * HEAD_DIM)),
        'pool_w': nrm(ks[18], (DEPTH, POOL_GROUPS, POOL_GROUP_W, POOL_GROUP_W), POOL_GROUP_W ** -0.5),
        'pool_scale': gain(ks[19], (DEPTH, POOL_W)),
        'w_br_attn': nrm(ks[20], (DEPTH, ATTN_W, D), ATTN_W ** -0.5),
        'w_br_pool': nrm(ks[21], (DEPTH, POOL_W, D), POOL_W ** -0.5),
        'w_out': nrm(ks[22], (DEPTH, D, D), D ** -0.5),
        'norm2_g': gain(ks[23], (DEPTH, D)),
        'router_group_w': nrm(ks[24], (DEPTH, D, N_GROUPS), D ** -0.5),
        'router_group_b': nrm(ks[25], (DEPTH, N_GROUPS), 0.01),
        'router_expert_w': nrm(ks[26], (DEPTH, D, N_EXPERTS), D ** -0.5),
        'router_expert_b': nrm(ks[27], (DEPTH, N_EXPERTS), 0.01),
        'expert_w_gate': nrm(ks[28], (DEPTH, N_EXPERTS, D, D_EXPERT), D ** -0.5),
        'expert_w_up': nrm(ks[29], (DEPTH, N_EXPERTS, D, D_EXPERT), D ** -0.5),
        'expert_w_down': nrm(ks[30], (DEPTH, N_EXPERTS, D_EXPERT, D), D_EXPERT ** -0.5),
    }


def reference(x_prompt, x_sample, c, cache_k, cache_v, c_ctx, w_ada, b_ada, norm1_g, w_in, b_gate,
              q_norm_g, k_norm_g, lambda_q1, lambda_k1, lambda_q2, lambda_k2, subln_g, pool_w, pool_scale,
              w_br_attn, w_br_pool, w_out, norm2_g, router_group_w, router_group_b, router_expert_w,
              router_expert_b, expert_w_gate, expert_w_up, expert_w_down):
    rope = _axial_rope_tables(x_sample.shape[1])
    y_prompt = x_prompt
    y_sample = x_sample
    past_len = cache_k.shape[2]
    new_k = []
    new_v = []
    for l in range(DEPTH):
        lp = dict(norm1_g=norm1_g[l], w_in=w_in[l], b_gate=b_gate[l], q_norm_g=q_norm_g[l],
                  k_norm_g=k_norm_g[l], lambda_q1=lambda_q1[l], lambda_k1=lambda_k1[l],
                  lambda_q2=lambda_q2[l], lambda_k2=lambda_k2[l], subln_g=subln_g[l], pool_w=pool_w[l],
                  pool_scale=pool_scale[l], w_br_attn=w_br_attn[l], w_br_pool=w_br_pool[l], w_out=w_out[l],
                  norm2_g=norm2_g[l], router_group_w=router_group_w[l], router_group_b=router_group_b[l],
                  router_expert_w=router_expert_w[l], router_expert_b=router_expert_b[l],
                  expert_w_gate=expert_w_gate[l], expert_w_up=expert_w_up[l], expert_w_down=expert_w_down[l])
        lambda_init = 0.8 - 0.6 * math.exp(-0.3 * l)
        mod_ctx = _adaln(c_ctx[None, :], w_ada[l], b_ada[l])
        mod_lat = _adaln(c, w_ada[l], b_ada[l])
        y_prompt, k_ctx, v_ctx = _layer(y_prompt, mod_ctx, lp, lambda_init, None, None, None)
        new_k.append(k_ctx.reshape(k_ctx.shape[0], k_ctx.shape[1], N_HEADS, 2 * HEAD_DIM))
        new_v.append(v_ctx)
        ck = cache_k[:, l].reshape(cache_k.shape[0], past_len, N_HEADS, 2, HEAD_DIM)
        cv = cache_v[:, l]
        y_sample, _, _ = _layer(y_sample, mod_lat, lp, lambda_init, rope, ck, cv)
    new_cache_k = jnp.stack(new_k, axis=1)
    new_cache_v = jnp.stack(new_v, axis=1)
    return (y_prompt, y_sample, new_cache_k, new_cache_v)
```

```python
import functools
import math

import jax
import jax.numpy as jnp
from jax import lax
from jax.experimental import pallas as pl
from jax.experimental.pallas import tpu as pltpu

D_MODEL = 1024
GRID_W = 64
N_HEADS = 8
HEAD_DIM = 64
HEAD_W = 2 * HEAD_DIM
ATTN_W = N_HEADS * HEAD_W
POOL_GROUPS = 4
POOL_WINDOWS = (2, 4, 8, 16)
POOL_W = 512
POOL_GROUP_W = 128
IN_W = 3 * ATTN_W + POOL_W + 2 * D_MODEL
ROPE_THETA = 10000.0
ROPE_PAIRS = 16
N_GROUPS = 4
EXPERTS_PER_GROUP = 4
N_EXPERTS = 16
D_EXPERT = 512
ADA_CHUNKS = 6
EPS = 1e-6
LAMBDA_INIT = 0.8 - 0.6 * math.exp(-0.0)

LANES = 128
ROUTER_W = 2 * LANES
NEG_BIG = -1e30

F32 = jnp.float32
BF16 = jnp.bfloat16

VMEM_LIMIT = 56 * 1024 * 1024


def _cparams(*sem):
    return pltpu.CompilerParams(dimension_semantics=sem, vmem_limit_bytes=VMEM_LIMIT)


def _split_bf16(x):
    hi = x.astype(BF16)
    lo = (x - hi.astype(F32)).astype(BF16)
    return hi, lo


def _dot(a, b):
    return jnp.dot(a, b, preferred_element_type=F32)


def _adaln_kernel(cond_ref, w_ref, b_ref, o_ref):
    c = cond_ref[...]
    s = c * jax.nn.sigmoid(c)
    s_hi, s_lo = _split_bf16(s)
    w_hi, w_lo = _split_bf16(w_ref[...])
    rows = s.shape[0]
    both = _dot(jnp.concatenate([s_hi, s_lo], axis=0), w_hi)
    o_ref[...] = both[:rows] + both[rows:] + _dot(s_hi, w_lo) + b_ref[...]


def _adaln(cond, w_ada, b_ada):
    rows = cond.shape[0]
    n = w_ada.shape[1]
    tn = 1536
    return pl.pallas_call(
        _adaln_kernel,
        grid=(n // tn,),
        in_specs=[pl.BlockSpec((rows, D_MODEL), lambda j: (0, 0)),
                  pl.BlockSpec((D_MODEL, tn), lambda j: (0, j)),
                  pl.BlockSpec((1, tn), lambda j: (0, j))],
        out_specs=pl.BlockSpec((rows, tn), lambda j: (0, j)),
        out_shape=jax.ShapeDtypeStruct((rows, n), F32),
        compiler_params=_cparams("arbitrary"),
        name="adaln",
    )(cond, w_ada, b_ada.reshape(1, n))


def _chunk_rms(z, seg):
    zz = (z * z).astype(BF16)
    parts = [_dot(zz[:, c * 256:(c + 1) * 256], seg) for c in range(z.shape[1] // 256)]
    return jnp.concatenate(parts, axis=1)


def _rope(x, cos, sin):
    lane = lax.broadcasted_iota(jnp.int32, (x.shape[0], LANES), 1)
    first = (lane % (2 * ROPE_PAIRS)) < ROPE_PAIRS
    parts = []
    for c in range(x.shape[1] // LANES):
        xc = x[:, c * LANES:(c + 1) * LANES]
        up = pltpu.roll(xc, LANES - ROPE_PAIRS, 1)
        dn = pltpu.roll(xc, ROPE_PAIRS, 1)
        parts.append(xc * cos + jnp.where(first, -up, dn) * sin)
    return jnp.concatenate(parts, axis=1)


def _pre_kernel(*refs, rope, kv_dtype):
    if rope:
        (x_ref, mod_ref, g1_ref, w_ref, bg_ref, qg_ref, kg_ref, seg_ref, cos_ref, sin_ref,
         q_out, k_out, v_out, p_out, g_out) = refs
    else:
        (x_ref, mod_ref, g1_ref, w_ref, bg_ref, qg_ref, kg_ref, seg_ref,
         q_out, k_out, v_out, p_out, g_out) = refs
    x = x_ref[...]
    shift = mod_ref[0:1, :]
    scale = mod_ref[1:2, :]
    xn = x * lax.rsqrt(jnp.mean(x * x, axis=-1, keepdims=True) + EPS) * g1_ref[...]
    h = (xn * (1.0 + scale) + shift).astype(BF16)
    seg = seg_ref[...]

    zq = _dot(h, w_ref[:, 0:ATTN_W])
    qn = zq * lax.rsqrt(_chunk_rms(zq, seg) + EPS) * qg_ref[...]
    if rope:
        qn = _rope(qn, cos_ref[...], sin_ref[...])
    q_out[...] = (qn * (HEAD_DIM ** -0.5)).astype(BF16)

    zk = _dot(h, w_ref[:, ATTN_W:2 * ATTN_W])
    kn = zk * lax.rsqrt(_chunk_rms(zk, seg) + EPS) * kg_ref[...]
    if rope:
        kn = _rope(kn, cos_ref[...], sin_ref[...])
    k_out[...] = kn.astype(kv_dtype)

    v_out[...] = _dot(h, w_ref[:, 2 * ATTN_W:3 * ATTN_W]).astype(kv_dtype)
    p_out[...] = _dot(h, w_ref[:, 3 * ATTN_W:3 * ATTN_W + POOL_W]).astype(BF16)
    gl = _dot(h, w_ref[:, 3 * ATTN_W + POOL_W:IN_W]) + bg_ref[...]
    g_out[...] = jax.nn.sigmoid(gl).astype(BF16)


def _pre_mixer(x, mod, lp, rope_tabs, *, seq, tm, kv_dtype):
    t = x.shape[0]
    tiles_per_seq = seq // tm
    single_mod = mod.shape[0] == 1
    mod_idx = (lambda i: (0, 0, 0)) if single_mod else (lambda i: (i // tiles_per_seq, 0, 0))
    const = lambda i: (0, 0)
    row = lambda i: (i, 0)
    in_specs = [pl.BlockSpec((tm, D_MODEL), row),
                pl.BlockSpec((None, ADA_CHUNKS, D_MODEL), mod_idx),
                pl.BlockSpec((1, D_MODEL), const),
                pl.BlockSpec((D_MODEL, IN_W), const),
                pl.BlockSpec((1, 2 * D_MODEL), const),
                pl.BlockSpec((1, ATTN_W), const),
                pl.BlockSpec((1, ATTN_W), const),
                pl.BlockSpec((256, 256), const)]
    args = [x, mod, lp["norm1_g"], lp["w_in"], lp["b_gate"], lp["q_gain"], lp["k_gain"], lp["seg"]]
    rope = rope_tabs is not None
    if rope:
        in_specs += [pl.BlockSpec((tm, LANES), lambda i: (i % tiles_per_seq, 0))] * 2
        args += list(rope_tabs)
    out_shape = [jax.ShapeDtypeStruct((t, ATTN_W), BF16),
                 jax.ShapeDtypeStruct((t, ATTN_W), kv_dtype),
                 jax.ShapeDtypeStruct((t, ATTN_W), kv_dtype),
                 jax.ShapeDtypeStruct((t, POOL_W), BF16),
                 jax.ShapeDtypeStruct((t, 2 * D_MODEL), BF16)]
    out_specs = [pl.BlockSpec((tm, ATTN_W), row), pl.BlockSpec((tm, ATTN_W), row),
                 pl.BlockSpec((tm, ATTN_W), row), pl.BlockSpec((tm, POOL_W), row),
                 pl.BlockSpec((tm, 2 * D_MODEL), row)]
    return pl.pallas_call(
        functools.partial(_pre_kernel, rope=rope, kv_dtype=kv_dtype),
        grid=(t // tm,),
        in_specs=in_specs, out_specs=out_specs, out_shape=out_shape,
        compiler_params=_cparams("arbitrary"),
        name="pre_mixer_rope" if rope else "pre_mixer",
    )(*args)


def _attn_kernel(*refs, n_q_blocks, tq, has_cache):
    if has_cache:
        q_ref, k_ref, v_ref, ck_ref, cv_ref, lam_ref, sg_ref, o_ref = refs
    else:
        q_ref, k_ref, v_ref, lam_ref, sg_ref, o_ref = refs
    lv = lam_ref[...]
    lam = (jnp.exp(jnp.sum(lv[0:1] * lv[1:2], axis=-1, keepdims=True))
           - jnp.exp(jnp.sum(lv[2:3] * lv[3:4], axis=-1, keepdims=True)) + LAMBDA_INIT)
    nt = (((1,), (1,)), ((), ()))

    def halves(ref):
        kf = ref[...].astype(F32)
        lane = lax.broadcasted_iota(jnp.int32, kf.shape, 1)
        return (jnp.where(lane < HEAD_DIM, kf, 0.0).astype(BF16),
                jnp.where(lane >= HEAD_DIM, kf, 0.0).astype(BF16))

    keys = [halves(k_ref)]
    vals = [v_ref[...].astype(BF16)]
    if has_cache:
        keys.append(halves(ck_ref))
        vals.append(cv_ref[...].astype(BF16))
    sg = sg_ref[...] * (1.0 - LAMBDA_INIT)

    def softmax_parts(q, which):
        s = [lax.dot_general(q, kk[which], nt, preferred_element_type=F32) for kk in keys]
        m = functools.reduce(jnp.maximum, [jnp.max(x, axis=-1, keepdims=True) for x in s])
        e = [jnp.exp(x - m) for x in s]
        den = functools.reduce(jnp.add, [jnp.sum(x, axis=-1, keepdims=True) for x in e])
        return e, 1.0 / den

    def block(i, carry):
        qs = pl.multiple_of(i * tq, tq)
        q = q_ref[pl.ds(qs, tq), :]
        e1, r1 = softmax_parts(q, 0)
        e2, r2 = softmax_parts(q, 1)
        r2 = r2 * lam
        o = None
        for x1, x2, vv in zip(e1, e2, vals):
            part = _dot((x1 * r1 - x2 * r2).astype(BF16), vv)
            o = part if o is None else o + part
        on = o * lax.rsqrt(jnp.mean(o * o, axis=-1, keepdims=True) + EPS) * sg
        o_ref[pl.ds(qs, tq), :] = on.astype(o_ref.dtype)
        return carry

    if n_q_blocks == 1:
        block(0, 0)
    else:
        lax.fori_loop(0, n_q_blocks, block, 0)


def _attention(q, k, v, cache, lam_rows, subln_g, *, tq):
    b, n, _ = q.shape
    has_cache = cache is not None
    head = lambda bi, hi: (bi, 0, hi)
    const = lambda bi, hi: (0, 0)
    in_specs = [pl.BlockSpec((None, n, HEAD_W), head)] * 3
    args = [q, k, v]
    if has_cache:
        p_len = cache[0].shape[1]
        in_specs += [pl.BlockSpec((None, p_len, HEAD_W), head)] * 2
        args += list(cache)
    in_specs += [pl.BlockSpec((4, HEAD_DIM), const), pl.BlockSpec((1, HEAD_W), const)]
    args += [lam_rows, subln_g]
    return pl.pallas_call(
        functools.partial(_attn_kernel, n_q_blocks=n // tq, tq=tq, has_cache=has_cache),
        grid=(b, N_HEADS),
        in_specs=in_specs,
        out_specs=pl.BlockSpec((None, n, HEAD_W), head),
        out_shape=jax.ShapeDtypeStruct((b, n, ATTN_W), BF16),
        compiler_params=_cparams("arbitrary", "arbitrary"),
        name="diff_attn_cache" if has_cache else "diff_attn",
    )(*args)


def _router_gates(logits):
    rows = logits.shape[0]
    lane_i = lax.broadcasted_iota(jnp.int32, (rows, LANES), 1)
    valid = lane_i < N_EXPERTS
    lane = lane_i.astype(F32)
    grp = (lane_i // EXPERTS_PER_GROUP).astype(F32)
    e_log = logits[:, :LANES]
    g_log = jnp.where(valid, logits[:, LANES:], NEG_BIG)
    g_max = jnp.max(g_log, axis=-1, keepdims=True)
    g_den = jnp.sum(jnp.exp(g_log - g_max), axis=-1, keepdims=True) * (1.0 / EXPERTS_PER_GROUP)
    g_w = 1.0 / g_den
    g_idx = jnp.min(jnp.where(g_log == g_max, grp, float(N_GROUPS)), axis=-1, keepdims=True)
    e_sel = jnp.where(grp == g_idx, jnp.where(valid, e_log, NEG_BIG), NEG_BIG)
    v1 = jnp.max(e_sel, axis=-1, keepdims=True)
    i1 = jnp.min(jnp.where(e_sel == v1, lane, float(LANES)), axis=-1, keepdims=True)
    e_rest = jnp.where(lane == i1, NEG_BIG, e_sel)
    v2 = jnp.max(e_rest, axis=-1, keepdims=True)
    i2 = jnp.min(jnp.where(e_rest == v2, lane, float(LANES)), axis=-1, keepdims=True)
    t = jnp.exp(v2 - v1)
    w1 = g_w / (1.0 + t)
    w2 = w1 * t
    return jnp.where(lane == i1, w1, 0.0) + jnp.where(lane == i2, w2, 0.0)


def _post_kernel(o_ref, p_ref, g_ref, x_ref, mod_ref, wa_ref, pw_ref, ps_ref, wp_ref, wo_ref,
                 g2_ref, wrh_ref, wrl_ref, br_ref, x1_out, h2_out, gates_out, *, seq, tm):
    r0 = pl.multiple_of(pl.program_id(1) * tm, tm)
    attn_out = _dot(o_ref[...], wa_ref[...])

    p_all = p_ref[...]
    p_self = p_ref[pl.ds(r0, tm), :].astype(F32)
    tok = r0 + lax.broadcasted_iota(jnp.int32, (tm, seq), 0)
    col = lax.broadcasted_iota(jnp.int32, (tm, seq), 1)
    dist = jnp.abs(2 * (col - tok) + 1)
    tok1 = r0 + lax.broadcasted_iota(jnp.int32, (tm, 1), 0)
    mixed = []
    for gi, w in enumerate(POOL_WINDOWS):
        half = w // 2
        band = jnp.where(dist < w, 1.0, 0.0).astype(BF16)
        cnt = (jnp.minimum(tok1 + half, seq) - jnp.maximum(tok1 - half, 0)).astype(F32)
        sl = slice(gi * POOL_GROUP_W, (gi + 1) * POOL_GROUP_W)
        pooled = _dot(band, p_all[:, sl]) / cnt - p_self[:, sl]
        mixed.append(_dot(pooled.astype(BF16), pw_ref[gi]))
    mixed = jnp.concatenate(mixed, axis=1) * ps_ref[...]
    pool_out = _dot(mixed.astype(BF16), wp_ref[...])

    g = g_ref[...]
    merged = g[:, :D_MODEL].astype(F32) * attn_out + g[:, D_MODEL:].astype(F32) * pool_out
    gate1 = mod_ref[2:3, :]
    x1 = x_ref[...] + gate1 * _dot(merged.astype(BF16), wo_ref[...])
    x1_out[...] = x1

    shift2 = mod_ref[3:4, :]
    scale2 = mod_ref[4:5, :]
    xn = x1 * lax.rsqrt(jnp.mean(x1 * x1, axis=-1, keepdims=True) + EPS) * g2_ref[...]
    h2 = xn * (1.0 + scale2) + shift2
    h2_hi, h2_lo = _split_bf16(h2)
    h2_out[...] = h2_hi

    logits = (_dot(h2_hi, wrh_ref[...]) + _dot(h2_lo, wrh_ref[...]) + _dot(h2_hi, wrl_ref[...])
              + br_ref[...])
    gates_out[...] = _router_gates(logits)[:, :N_EXPERTS]


def _post_mixer(o, p, g, x, mod, lp, *, tm):
    b, n, _ = x.shape
    single_mod = mod.shape[0] == 1
    mod_idx = (lambda bi, ri: (0, 0, 0)) if single_mod else (lambda bi, ri: (bi, 0, 0))
    tile = lambda bi, ri: (bi, ri, 0)
    const2 = lambda bi, ri: (0, 0)
    const3 = lambda bi, ri: (0, 0, 0)
    in_specs = [pl.BlockSpec((None, tm, ATTN_W), tile),
                pl.BlockSpec((None, n, POOL_W), lambda bi, ri: (bi, 0, 0)),
                pl.BlockSpec((None, tm, 2 * D_MODEL), tile),
                pl.BlockSpec((None, tm, D_MODEL), tile),
                pl.BlockSpec((None, ADA_CHUNKS, D_MODEL), mod_idx),
                pl.BlockSpec((ATTN_W, D_MODEL), const2),
                pl.BlockSpec((POOL_GROUPS, POOL_GROUP_W, POOL_GROUP_W), const3),
                pl.BlockSpec((1, POOL_W), const2),
                pl.BlockSpec((POOL_W, D_MODEL), const2),
                pl.BlockSpec((D_MODEL, D_MODEL), const2),
                pl.BlockSpec((1, D_MODEL), const2),
                pl.BlockSpec((D_MODEL, ROUTER_W), const2),
                pl.BlockSpec((D_MODEL, ROUTER_W), const2),
                pl.BlockSpec((1, ROUTER_W), const2)]
    out_shape = [jax.ShapeDtypeStruct((b, n, D_MODEL), F32),
                 jax.ShapeDtypeStruct((b, n, D_MODEL), BF16),
                 jax.ShapeDtypeStruct((b, n, N_EXPERTS), F32)]
    out_specs = [pl.BlockSpec((None, tm, D_MODEL), tile),
                 pl.BlockSpec((None, tm, D_MODEL), tile),
                 pl.BlockSpec((None, tm, N_EXPERTS), tile)]
    return pl.pallas_call(
        functools.partial(_post_kernel, seq=n, tm=tm),
        grid=(b, n // tm),
        in_specs=in_specs, out_specs=out_specs, out_shape=out_shape,
        compiler_params=_cparams("arbitrary", "arbitrary"),
        name="post_mixer",
    )(o, p, g, x, mod, lp["w_br_attn"], lp["pool_w"], lp["pool_scale"], lp["w_br_pool"], lp["w_out"],
      lp["norm2_g"], lp["w_router_hi"], lp["w_router_lo"], lp["b_router"])


def _moe_kernel(h_ref, gates_ref, x1_ref, mod_ref, wg_ref, wu_ref, wd_ref, y_ref, acc_ref):
    e = pl.program_id(1)

    @pl.when(e == 0)
    def _():
        acc_ref[...] = jnp.zeros_like(acc_ref)

    h = h_ref[...]
    gates = gates_ref[...]
    lane = lax.broadcasted_iota(jnp.int32, gates.shape, 1)
    gate = jnp.sum(jnp.where(lane == e, gates, 0.0), axis=-1, keepdims=True)
    a = _dot(h, wg_ref[...])
    u = _dot(h, wu_ref[...])
    hid = a * jax.nn.sigmoid(a) * u * gate
    acc_ref[...] += _dot(hid.astype(BF16), wd_ref[...])

    @pl.when(e == pl.num_programs(1) - 1)
    def _():
        y_ref[...] = x1_ref[...] + mod_ref[5:6, :] * acc_ref[...]


def _moe(h2, gates, x1, mod, lp, *, seq, tm):
    t = h2.shape[0]
    tiles_per_seq = seq // tm
    single_mod = mod.shape[0] == 1
    mod_idx = (lambda i, e: (0, 0, 0)) if single_mod else (lambda i, e: (i // tiles_per_seq, 0, 0))
    row = lambda i, e: (i, 0)
    expert = lambda i, e: (e, 0, 0)
    return pl.pallas_call(
        _moe_kernel,
        grid=(t // tm, N_EXPERTS),
        in_specs=[pl.BlockSpec((tm, D_MODEL), row),
                  pl.BlockSpec((tm, N_EXPERTS), row),
                  pl.BlockSpec((tm, D_MODEL), row),
                  pl.BlockSpec((None, ADA_CHUNKS, D_MODEL), mod_idx),
                  pl.BlockSpec((None, D_MODEL, D_EXPERT), expert),
                  pl.BlockSpec((None, D_MODEL, D_EXPERT), expert),
                  pl.BlockSpec((None, D_EXPERT, D_MODEL), expert)],
        out_specs=pl.BlockSpec((tm, D_MODEL), row),
        out_shape=jax.ShapeDtypeStruct((t, D_MODEL), F32),
        scratch_shapes=[pltpu.VMEM((tm, D_MODEL), F32)],
        compiler_params=_cparams("arbitrary", "arbitrary"),
        name="moe",
    )(h2, gates, x1, mod, lp["expert_w_gate"], lp["expert_w_up"], lp["expert_w_down"])


def _rope_tables(n_tokens):
    rows = n_tokens // GRID_W
    row_ids = jnp.repeat(jnp.arange(rows, dtype=F32), GRID_W)
    col_ids = jnp.tile(jnp.arange(GRID_W, dtype=F32), rows)
    inv_freq = jnp.power(ROPE_THETA, -jnp.arange(ROPE_PAIRS, dtype=F32) / ROPE_PAIRS)
    ang_r = row_ids[:, None] * inv_freq[None, :]
    ang_c = col_ids[:, None] * inv_freq[None, :]
    ang = jnp.concatenate([ang_r, ang_r, ang_c, ang_c] * 2, axis=-1)
    return jnp.cos(ang), jnp.sin(ang)


def _layer(x, mod, lp, rope_tabs, cache, lam_rows, *, kv_dtype):
    b, n, _ = x.shape
    t = b * n
    q, k, v, p, g = _pre_mixer(x.reshape(t, D_MODEL), mod, lp, rope_tabs, seq=n, tm=256, kv_dtype=kv_dtype)
    o = _attention(q.reshape(b, n, ATTN_W), k.reshape(b, n, ATTN_W), v.reshape(b, n, ATTN_W),
                   cache, lam_rows, lp["subln_g"], tq=256)
    x1, h2, gates = _post_mixer(o, p.reshape(b, n, POOL_W), g.reshape(b, n, 2 * D_MODEL), x, mod, lp, tm=256)
    y = _moe(h2.reshape(t, D_MODEL), gates.reshape(t, N_EXPERTS), x1.reshape(t, D_MODEL), mod, lp,
             seq=n, tm=512)
    return y.reshape(b, n, D_MODEL), k, v


def kernel(x_prompt, x_sample, c, cache_k, cache_v, c_ctx, w_ada, b_ada, norm1_g, w_in, b_gate, q_norm_g, k_norm_g, lambda_q1, lambda_k1, lambda_q2, lambda_k2, subln_g, pool_w, pool_scale, w_br_attn, w_br_pool, w_out, norm2_g, router_group_w, router_group_b, router_expert_w, router_expert_b, expert_w_gate, expert_w_up, expert_w_down):
    b_ctx, n_ctx, _ = x_prompt.shape
    b_lat, n_lat, _ = x_sample.shape
    past = cache_k.shape[2]

    def router_layout(we, wg):
        pad = jnp.zeros(we.shape[:-1] + (LANES - N_EXPERTS,), F32)
        return jnp.concatenate([we, pad, jnp.repeat(wg, EXPERTS_PER_GROUP, axis=-1), pad], axis=-1)

    w_router = router_layout(router_expert_w[0], router_group_w[0])
    w_router_hi, w_router_lo = _split_bf16(w_router)
    seg = (jnp.arange(256)[:, None] // HEAD_DIM == jnp.arange(256)[None, :] // HEAD_DIM)
    lp = dict(
        norm1_g=norm1_g[0].reshape(1, D_MODEL),
        w_in=w_in[0].astype(BF16),
        b_gate=b_gate[0].reshape(1, 2 * D_MODEL),
        q_gain=jnp.tile(q_norm_g[0], ATTN_W // HEAD_DIM).reshape(1, ATTN_W),
        k_gain=jnp.tile(k_norm_g[0], ATTN_W // HEAD_DIM).reshape(1, ATTN_W),
        seg=(seg.astype(F32) / HEAD_DIM).astype(BF16),
        subln_g=subln_g[0].reshape(1, HEAD_W),
        w_br_attn=w_br_attn[0].astype(BF16),
        pool_w=pool_w[0].astype(BF16),
        pool_scale=pool_scale[0].reshape(1, POOL_W),
        w_br_pool=w_br_pool[0].astype(BF16),
        w_out=w_out[0].astype(BF16),
        norm2_g=norm2_g[0].reshape(1, D_MODEL),
        w_router_hi=w_router_hi, w_router_lo=w_router_lo,
        b_router=router_layout(router_expert_b[0], router_group_b[0]).reshape(1, ROUTER_W),
        expert_w_gate=expert_w_gate[0].astype(BF16),
        expert_w_up=expert_w_up[0].astype(BF16),
        expert_w_down=expert_w_down[0].astype(BF16),
    )
    lam_rows = jnp.stack([lambda_q1[0], lambda_k1[0], lambda_q2[0], lambda_k2[0]], axis=0)

    n_cond = 1 + b_lat
    cond = jnp.concatenate([c_ctx[None, :], c, jnp.zeros((16 - n_cond, D_MODEL), F32)], axis=0)
    mod = _adaln(cond, w_ada[0], b_ada[0])[:n_cond].reshape(n_cond, ADA_CHUNKS, D_MODEL)

    y_prompt, k_ctx, v_ctx = _layer(x_prompt, mod[:1], lp, None, None, lam_rows, kv_dtype=F32)
    cache = (cache_k[:, 0].reshape(b_lat, past, ATTN_W), cache_v[:, 0].reshape(b_lat, past, ATTN_W))
    y_sample, _, _ = _layer(x_sample, mod[1:], lp, _rope_tables(n_lat), cache, lam_rows, kv_dtype=BF16)

    new_cache_k = k_ctx.reshape(b_ctx, 1, n_ctx, N_HEADS, HEAD_W)
    new_cache_v = v_ctx.reshape(b_ctx, 1, n_ctx, N_HEADS, HEAD_W)
    return (y_prompt, y_sample, new_cache_k, new_cache_v)
```

```python
import functools
import math

import jax
import jax.numpy as jnp
from jax import lax
from jax.experimental import pallas as pl
from jax.experimental.pallas import tpu as pltpu

D_MODEL = 1024
GRID_W = 64
N_HEADS = 8
HEAD_DIM = 64
HEAD_W = 2 * HEAD_DIM
ATTN_W = N_HEADS * HEAD_W
POOL_GROUPS = 4
POOL_WINDOWS = (2, 4, 8, 16)
POOL_W = 512
POOL_GROUP_W = 128
IN_W = 3 * ATTN_W + POOL_W + 2 * D_MODEL
ROPE_THETA = 10000.0
ROPE_PAIRS = 16
N_GROUPS = 4
EXPERTS_PER_GROUP = 4
N_EXPERTS = 16
D_EXPERT = 512
ADA_CHUNKS = 6
EPS = 1e-6
LAMBDA_INIT = 0.8 - 0.6 * math.exp(-0.0)

LANES = 128
ROUTER_W = 2 * LANES
NEG_BIG = -1e30
Q_SCALE = math.log2(math.e) * HEAD_DIM ** -0.5

F32 = jnp.float32
BF16 = jnp.bfloat16

VMEM_LIMIT = 56 * 1024 * 1024


def _cparams(*sem):
    return pltpu.CompilerParams(dimension_semantics=sem, vmem_limit_bytes=VMEM_LIMIT)


def _split_bf16(x):
    hi = x.astype(BF16)
    lo = (x - hi.astype(F32)).astype(BF16)
    return hi, lo


def _dot(a, b):
    return jnp.dot(a, b, preferred_element_type=F32)


def _adaln_kernel(cond_ref, w_ref, b_ref, o_ref):
    c = cond_ref[...]
    s = c * jax.nn.sigmoid(c)
    s_hi, s_lo = _split_bf16(s)
    w_hi, w_lo = _split_bf16(w_ref[...])
    rows = s.shape[0]
    both = _dot(jnp.concatenate([s_hi, s_lo], axis=0), w_hi)
    o_ref[...] = both[:rows] + both[rows:] + _dot(s_hi, w_lo) + b_ref[...]


def _adaln(cond, w_ada, b_ada):
    rows = cond.shape[0]
    n = w_ada.shape[1]
    tn = 1536
    return pl.pallas_call(
        _adaln_kernel,
        grid=(n // tn,),
        in_specs=[pl.BlockSpec((rows, D_MODEL), lambda j: (0, 0)),
                  pl.BlockSpec((D_MODEL, tn), lambda j: (0, j)),
                  pl.BlockSpec((1, tn), lambda j: (0, j))],
        out_specs=pl.BlockSpec((rows, tn), lambda j: (0, j)),
        out_shape=jax.ShapeDtypeStruct((rows, n), F32),
        compiler_params=_cparams("arbitrary"),
        name="adaln",
    )(cond, w_ada, b_ada.reshape(1, n))


def _chunk_rms(z, seg):
    zz = (z * z).astype(BF16)
    parts = [_dot(zz[:, c * 256:(c + 1) * 256], seg) for c in range(z.shape[1] // 256)]
    return jnp.concatenate(parts, axis=1)


def _rope(x, cos, sin):
    lane = lax.broadcasted_iota(jnp.int32, (x.shape[0], LANES), 1)
    first = (lane % (2 * ROPE_PAIRS)) < ROPE_PAIRS
    parts = []
    for c in range(x.shape[1] // LANES):
        xc = x[:, c * LANES:(c + 1) * LANES]
        up = pltpu.roll(xc, LANES - ROPE_PAIRS, 1)
        dn = pltpu.roll(xc, ROPE_PAIRS, 1)
        parts.append(xc * cos + jnp.where(first, -up, dn) * sin)
    return jnp.concatenate(parts, axis=1)


def _pre_kernel(*refs, rope, kv_dtype):
    if rope:
        (x_ref, mod_ref, g1_ref, w_ref, bg_ref, qg_ref, kg_ref, seg_ref, cos_ref, sin_ref,
         q_out, k_out, v_out, p_out, g_out) = refs
    else:
        (x_ref, mod_ref, g1_ref, w_ref, bg_ref, qg_ref, kg_ref, seg_ref,
         q_out, k_out, v_out, p_out, g_out) = refs
    x = x_ref[...]
    shift = mod_ref[0:1, :]
    scale = mod_ref[1:2, :]
    xn = x * lax.rsqrt(jnp.mean(x * x, axis=-1, keepdims=True) + EPS) * g1_ref[...]
    h = (xn * (1.0 + scale) + shift).astype(BF16)
    seg = seg_ref[...]

    zq = _dot(h, w_ref[:, 0:ATTN_W])
    qn = zq * lax.rsqrt(_chunk_rms(zq, seg) + EPS) * qg_ref[...]
    if rope:
        qn = _rope(qn, cos_ref[...], sin_ref[...])
    q_out[...] = (qn * Q_SCALE).astype(BF16)

    zk = _dot(h, w_ref[:, ATTN_W:2 * ATTN_W])
    kn = zk * lax.rsqrt(_chunk_rms(zk, seg) + EPS) * kg_ref[...]
    if rope:
        kn = _rope(kn, cos_ref[...], sin_ref[...])
    k_out[...] = kn.astype(kv_dtype)

    v_out[...] = _dot(h, w_ref[:, 2 * ATTN_W:3 * ATTN_W]).astype(kv_dtype)
    p_out[...] = _dot(h, w_ref[:, 3 * ATTN_W:3 * ATTN_W + POOL_W]).astype(BF16)
    gl = _dot(h, w_ref[:, 3 * ATTN_W + POOL_W:IN_W]) + bg_ref[...]
    g_out[...] = jax.nn.sigmoid(gl).astype(BF16)


def _pre_mixer(x, mod, lp, rope_tabs, *, seq, tm, kv_dtype):
    t = x.shape[0]
    tiles_per_seq = seq // tm
    single_mod = mod.shape[0] == 1
    mod_idx = (lambda i: (0, 0, 0)) if single_mod else (lambda i: (i // tiles_per_seq, 0, 0))
    const = lambda i: (0, 0)
    row = lambda i: (i, 0)
    in_specs = [pl.BlockSpec((tm, D_MODEL), row),
                pl.BlockSpec((None, ADA_CHUNKS, D_MODEL), mod_idx),
                pl.BlockSpec((1, D_MODEL), const),
                pl.BlockSpec((D_MODEL, IN_W), const),
                pl.BlockSpec((1, 2 * D_MODEL), const),
                pl.BlockSpec((1, ATTN_W), const),
                pl.BlockSpec((1, ATTN_W), const),
                pl.BlockSpec((256, 256), const)]
    args = [x, mod, lp["norm1_g"], lp["w_in"], lp["b_gate"], lp["q_gain"], lp["k_gain"], lp["seg"]]
    rope = rope_tabs is not None
    if rope:
        in_specs += [pl.BlockSpec((tm, LANES), lambda i: (i % tiles_per_seq, 0))] * 2
        args += list(rope_tabs)
    out_shape = [jax.ShapeDtypeStruct((t, ATTN_W), BF16),
                 jax.ShapeDtypeStruct((t, ATTN_W), kv_dtype),
                 jax.ShapeDtypeStruct((t, ATTN_W), kv_dtype),
                 jax.ShapeDtypeStruct((t, POOL_W), BF16),
                 jax.ShapeDtypeStruct((t, 2 * D_MODEL), BF16)]
    out_specs = [pl.BlockSpec((tm, ATTN_W), row), pl.BlockSpec((tm, ATTN_W), row),
                 pl.BlockSpec((tm, ATTN_W), row), pl.BlockSpec((tm, POOL_W), row),
                 pl.BlockSpec((tm, 2 * D_MODEL), row)]
    return pl.pallas_call(
        functools.partial(_pre_kernel, rope=rope, kv_dtype=kv_dtype),
        grid=(t // tm,),
        in_specs=in_specs, out_specs=out_specs, out_shape=out_shape,
        compiler_params=_cparams("arbitrary"),
        name="pre_mixer_rope" if rope else "pre_mixer",
    )(*args)


def _attn_kernel(*refs, n_q_blocks, tq, has_cache):
    if has_cache:
        q_ref, k_ref, v_ref, ck_ref, cv_ref, lam_ref, sg_ref, o_ref, k1_s, k2_s, v_s = refs
    else:
        q_ref, k_ref, v_ref, lam_ref, sg_ref, o_ref, k1_s, k2_s, v_s = refs
    lv = lam_ref[...]
    lam = (jnp.exp(jnp.sum(lv[0:1] * lv[1:2], axis=-1, keepdims=True))
           - jnp.exp(jnp.sum(lv[2:3] * lv[3:4], axis=-1, keepdims=True)) + LAMBDA_INIT)
    nt = (((1,), (1,)), ((), ()))
    sg = sg_ref[...] * (1.0 - LAMBDA_INIT)
    n = k_ref.shape[0]

    def stage(rows, kf, vf):
        kf = kf.astype(F32)
        lane = lax.broadcasted_iota(jnp.int32, kf.shape, 1)
        k1_s[rows, :] = jnp.where(lane < HEAD_DIM, kf, 0.0).astype(BF16)
        k2_s[rows, :] = jnp.where(lane >= HEAD_DIM, kf, 0.0).astype(BF16)
        v_s[rows, :] = vf.astype(BF16)

    for h in range(N_HEADS):
        cols = slice(h * HEAD_W, (h + 1) * HEAD_W)
        stage(slice(0, n), k_ref[:, cols], v_ref[:, cols])
        if has_cache:
            stage(slice(n, k1_s.shape[0]), ck_ref[:, h, :], cv_ref[:, h, :])

        def softmax_parts(q, k_s):
            s = lax.dot_general(q, k_s[...], nt, preferred_element_type=F32)
            e = jnp.exp2(s - jnp.max(s, axis=-1, keepdims=True))
            return e.astype(BF16), jnp.sum(e, axis=-1, keepdims=True)

        def block(i, carry):
            qs = pl.multiple_of(i * tq, tq)
            q = q_ref[pl.ds(qs, tq), cols]
            e1, d1 = softmax_parts(q, k1_s)
            e2, d2 = softmax_parts(q, k2_s)
            c = (lam * d1 / d2).astype(BF16)
            o = _dot(e1 - c * e2, v_s[...]) / d1
            on = o * lax.rsqrt(jnp.mean(o * o, axis=-1, keepdims=True) + EPS) * sg
            o_ref[pl.ds(qs, tq), cols] = on.astype(o_ref.dtype)
            return carry

        if n_q_blocks == 1:
            block(0, 0)
        else:
            lax.fori_loop(0, n_q_blocks, block, 0, unroll=2)


def _attention(q, k, v, cache, lam_rows, subln_g, *, tq):
    b, n, _ = q.shape
    has_cache = cache is not None
    seq = lambda bi: (bi, 0, 0)
    const = lambda bi: (0, 0)
    in_specs = [pl.BlockSpec((None, n, ATTN_W), seq)] * 3
    args = [q, k, v]
    if has_cache:
        p_len = cache[0].shape[2]
        in_specs += [pl.BlockSpec((None, None, p_len, N_HEADS, HEAD_W), lambda bi: (bi, 0, 0, 0, 0))] * 2
        args += list(cache)
    in_specs += [pl.BlockSpec((4, HEAD_DIM), const), pl.BlockSpec((1, HEAD_W), const)]
    args += [lam_rows, subln_g]
    n_keys = n + (cache[0].shape[2] if has_cache else 0)
    return pl.pallas_call(
        functools.partial(_attn_kernel, n_q_blocks=n // tq, tq=tq, has_cache=has_cache),
        grid=(b,),
        in_specs=in_specs,
        out_specs=pl.BlockSpec((None, n, ATTN_W), seq),
        out_shape=jax.ShapeDtypeStruct((b, n, ATTN_W), BF16),
        scratch_shapes=[pltpu.VMEM((n_keys, HEAD_W), BF16)] * 3,
        compiler_params=_cparams("arbitrary"),
        name="diff_attn_cache" if has_cache else "diff_attn",
    )(*args)


def _router_gates(logits):
    rows = logits.shape[0]
    lane_i = lax.broadcasted_iota(jnp.int32, (rows, LANES), 1)
    valid = lane_i < N_EXPERTS
    lane = lane_i.astype(F32)
    grp = (lane_i // EXPERTS_PER_GROUP).astype(F32)
    e_log = logits[:, :LANES]
    g_log = jnp.where(valid, logits[:, LANES:], NEG_BIG)
    g_max = jnp.max(g_log, axis=-1, keepdims=True)
    g_den = jnp.sum(jnp.exp(g_log - g_max), axis=-1, keepdims=True) * (1.0 / EXPERTS_PER_GROUP)
    g_w = 1.0 / g_den
    g_idx = jnp.min(jnp.where(g_log == g_max, grp, float(N_GROUPS)), axis=-1, keepdims=True)
    e_sel = jnp.where(grp == g_idx, jnp.where(valid, e_log, NEG_BIG), NEG_BIG)
    v1 = jnp.max(e_sel, axis=-1, keepdims=True)
    i1 = jnp.min(jnp.where(e_sel == v1, lane, float(LANES)), axis=-1, keepdims=True)
    e_rest = jnp.where(lane == i1, NEG_BIG, e_sel)
    v2 = jnp.max(e_rest, axis=-1, keepdims=True)
    i2 = jnp.min(jnp.where(e_rest == v2, lane, float(LANES)), axis=-1, keepdims=True)
    t = jnp.exp(v2 - v1)
    w1 = g_w / (1.0 + t)
    w2 = w1 * t
    return jnp.where(lane == i1, w1, 0.0) + jnp.where(lane == i2, w2, 0.0)


def _post_kernel(o_ref, p_ref, g_ref, x_ref, mod_ref, wa_ref, pw_ref, ps_ref, wp_ref, wo_ref,
                 g2_ref, wrh_ref, wrl_ref, br_ref, x1_out, h2_out, gates_out, *, seq, tm):
    r0 = pl.multiple_of(pl.program_id(1) * tm, tm)
    attn_out = _dot(o_ref[...], wa_ref[...])

    p_all = p_ref[...]
    p_self = p_ref[pl.ds(r0, tm), :].astype(F32)
    tok = r0 + lax.broadcasted_iota(jnp.int32, (tm, seq), 0)
    col = lax.broadcasted_iota(jnp.int32, (tm, seq), 1)
    dist = jnp.abs(2 * (col - tok) + 1)
    tok1 = r0 + lax.broadcasted_iota(jnp.int32, (tm, 1), 0)
    mixed = []
    for gi, w in enumerate(POOL_WINDOWS):
        half = w // 2
        band = jnp.where(dist < w, 1.0, 0.0).astype(BF16)
        cnt = (jnp.minimum(tok1 + half, seq) - jnp.maximum(tok1 - half, 0)).astype(F32)
        sl = slice(gi * POOL_GROUP_W, (gi + 1) * POOL_GROUP_W)
        pooled = _dot(band, p_all[:, sl]) / cnt - p_self[:, sl]
        mixed.append(_dot(pooled.astype(BF16), pw_ref[gi]))
    mixed = jnp.concatenate(mixed, axis=1) * ps_ref[...]
    pool_out = _dot(mixed.astype(BF16), wp_ref[...])

    g = g_ref[...]
    merged = g[:, :D_MODEL].astype(F32) * attn_out + g[:, D_MODEL:].astype(F32) * pool_out
    gate1 = mod_ref[2:3, :]
    x1 = x_ref[...] + gate1 * _dot(merged.astype(BF16), wo_ref[...])
    x1_out[...] = x1

    shift2 = mod_ref[3:4, :]
    scale2 = mod_ref[4:5, :]
    xn = x1 * lax.rsqrt(jnp.mean(x1 * x1, axis=-1, keepdims=True) + EPS) * g2_ref[...]
    h2 = xn * (1.0 + scale2) + shift2
    h2_hi, h2_lo = _split_bf16(h2)
    h2_out[...] = h2_hi

    logits = (_dot(h2_hi, wrh_ref[...]) + _dot(h2_lo, wrh_ref[...]) + _dot(h2_hi, wrl_ref[...])
              + br_ref[...])
    gates_out[...] = _router_gates(logits)[:, :N_EXPERTS]


def _post_mixer(o, p, g, x, mod, lp, *, tm):
    b, n, _ = x.shape
    single_mod = mod.shape[0] == 1
    mod_idx = (lambda bi, ri: (0, 0, 0)) if single_mod else (lambda bi, ri: (bi, 0, 0))
    tile = lambda bi, ri: (bi, ri, 0)
    const2 = lambda bi, ri: (0, 0)
    const3 = lambda bi, ri: (0, 0, 0)
    in_specs = [pl.BlockSpec((None, tm, ATTN_W), tile),
                pl.BlockSpec((None, n, POOL_W), lambda bi, ri: (bi, 0, 0)),
                pl.BlockSpec((None, tm, 2 * D_MODEL), tile),
                pl.BlockSpec((None, tm, D_MODEL), tile),
                pl.BlockSpec((None, ADA_CHUNKS, D_MODEL), mod_idx),
                pl.BlockSpec((ATTN_W, D_MODEL), const2),
                pl.BlockSpec((POOL_GROUPS, POOL_GROUP_W, POOL_GROUP_W), const3),
                pl.BlockSpec((1, POOL_W), const2),
                pl.BlockSpec((POOL_W, D_MODEL), const2),
                pl.BlockSpec((D_MODEL, D_MODEL), const2),
                pl.BlockSpec((1, D_MODEL), const2),
                pl.BlockSpec((D_MODEL, ROUTER_W), const2),
                pl.BlockSpec((D_MODEL, ROUTER_W), const2),
                pl.BlockSpec((1, ROUTER_W), const2)]
    out_shape = [jax.ShapeDtypeStruct((b, n, D_MODEL), F32),
                 jax.ShapeDtypeStruct((b, n, D_MODEL), BF16),
                 jax.ShapeDtypeStruct((b, n, N_EXPERTS), F32)]
    out_specs = [pl.BlockSpec((None, tm, D_MODEL), tile),
                 pl.BlockSpec((None, tm, D_MODEL), tile),
                 pl.BlockSpec((None, tm, N_EXPERTS), tile)]
    return pl.pallas_call(
        functools.partial(_post_kernel, seq=n, tm=tm),
        grid=(b, n // tm),
        in_specs=in_specs, out_specs=out_specs, out_shape=out_shape,
        compiler_params=_cparams("arbitrary", "arbitrary"),
        name="post_mixer",
    )(o, p, g, x, mod, lp["w_br_attn"], lp["pool_w"], lp["pool_scale"], lp["w_br_pool"], lp["w_out"],
      lp["norm2_g"], lp["w_router_hi"], lp["w_router_lo"], lp["b_router"])


def _moe_kernel(h_ref, gates_ref, x1_ref, mod_ref, wg_ref, wu_ref, wd_ref, y_ref, acc_ref):
    e = pl.program_id(1)

    @pl.when(e == 0)
    def _():
        acc_ref[...] = jnp.zeros_like(acc_ref)

    h = h_ref[...]
    gates = gates_ref[...]
    lane = lax.broadcasted_iota(jnp.int32, gates.shape, 1)
    gate = jnp.sum(jnp.where(lane == e, gates, 0.0), axis=-1, keepdims=True)
    a = _dot(h, wg_ref[...])
    u = _dot(h, wu_ref[...])
    hid = a * jax.nn.sigmoid(a) * u * gate
    acc_ref[...] += _dot(hid.astype(BF16), wd_ref[...])

    @pl.when(e == pl.num_programs(1) - 1)
    def _():
        y_ref[...] = x1_ref[...] + mod_ref[5:6, :] * acc_ref[...]


def _moe(h2, gates, x1, mod, lp, *, seq, tm):
    t = h2.shape[0]
    tiles_per_seq = seq // tm
    single_mod = mod.shape[0] == 1
    mod_idx = (lambda i, e: (0, 0, 0)) if single_mod else (lambda i, e: (i // tiles_per_seq, 0, 0))
    row = lambda i, e: (i, 0)
    expert = lambda i, e: (e, 0, 0)
    return pl.pallas_call(
        _moe_kernel,
        grid=(t // tm, N_EXPERTS),
        in_specs=[pl.BlockSpec((tm, D_MODEL), row),
                  pl.BlockSpec((tm, N_EXPERTS), row),
                  pl.BlockSpec((tm, D_MODEL), row),
                  pl.BlockSpec((None, ADA_CHUNKS, D_MODEL), mod_idx),
                  pl.BlockSpec((None, D_MODEL, D_EXPERT), expert),
                  pl.BlockSpec((None, D_MODEL, D_EXPERT), expert),
                  pl.BlockSpec((None, D_EXPERT, D_MODEL), expert)],
        out_specs=pl.BlockSpec((tm, D_MODEL), row),
        out_shape=jax.ShapeDtypeStruct((t, D_MODEL), F32),
        scratch_shapes=[pltpu.VMEM((tm, D_MODEL), F32)],
        compiler_params=_cparams("arbitrary", "arbitrary"),
        name="moe",
    )(h2, gates, x1, mod, lp["expert_w_gate"], lp["expert_w_up"], lp["expert_w_down"])


def _rope_tables(n_tokens):
    rows = n_tokens // GRID_W
    row_ids = jnp.repeat(jnp.arange(rows, dtype=F32), GRID_W)
    col_ids = jnp.tile(jnp.arange(GRID_W, dtype=F32), rows)
    inv_freq = jnp.power(ROPE_THETA, -jnp.arange(ROPE_PAIRS, dtype=F32) / ROPE_PAIRS)
    ang_r = row_ids[:, None] * inv_freq[None, :]
    ang_c = col_ids[:, None] * inv_freq[None, :]
    ang = jnp.concatenate([ang_r, ang_r, ang_c, ang_c] * 2, axis=-1)
    return jnp.cos(ang), jnp.sin(ang)


def _layer(x, mod, lp, rope_tabs, cache, lam_rows, *, kv_dtype):
    b, n, _ = x.shape
    t = b * n
    q, k, v, p, g = _pre_mixer(x.reshape(t, D_MODEL), mod, lp, rope_tabs, seq=n, tm=256, kv_dtype=kv_dtype)
    o = _attention(q.reshape(b, n, ATTN_W), k.reshape(b, n, ATTN_W), v.reshape(b, n, ATTN_W),
                   cache, lam_rows, lp["subln_g"], tq=256)
    x1, h2, gates = _post_mixer(o, p.reshape(b, n, POOL_W), g.reshape(b, n, 2 * D_MODEL), x, mod, lp, tm=256)
    y = _moe(h2.reshape(t, D_MODEL), gates.reshape(t, N_EXPERTS), x1.reshape(t, D_MODEL), mod, lp,
             seq=n, tm=512)
    return y.reshape(b, n, D_MODEL), k, v


def kernel(x_prompt, x_sample, c, cache_k, cache_v, c_ctx, w_ada, b_ada, norm1_g, w_in, b_gate, q_norm_g, k_norm_g, lambda_q1, lambda_k1, lambda_q2, lambda_k2, subln_g, pool_w, pool_scale, w_br_attn, w_br_pool, w_out, norm2_g, router_group_w, router_group_b, router_expert_w, router_expert_b, expert_w_gate, expert_w_up, expert_w_down):
    b_ctx, n_ctx, _ = x_prompt.shape
    b_lat, n_lat, _ = x_sample.shape

    def router_layout(we, wg):
        pad = jnp.zeros(we.shape[:-1] + (LANES - N_EXPERTS,), F32)
        return jnp.concatenate([we, pad, jnp.repeat(wg, EXPERTS_PER_GROUP, axis=-1), pad], axis=-1)

    w_router = router_layout(router_expert_w[0], router_group_w[0])
    w_router_hi, w_router_lo = _split_bf16(w_router)
    seg = (jnp.arange(256)[:, None] // HEAD_DIM == jnp.arange(256)[None, :] // HEAD_DIM)
    lp = dict(
        norm1_g=norm1_g[0].reshape(1, D_MODEL),
        w_in=w_in[0].astype(BF16),
        b_gate=b_gate[0].reshape(1, 2 * D_MODEL),
        q_gain=jnp.tile(q_norm_g[0], ATTN_W // HEAD_DIM).reshape(1, ATTN_W),
        k_gain=jnp.tile(k_norm_g[0], ATTN_W // HEAD_DIM).reshape(1, ATTN_W),
        seg=(seg.astype(F32) / HEAD_DIM).astype(BF16),
        subln_g=subln_g[0].reshape(1, HEAD_W),
        w_br_attn=w_br_attn[0].astype(BF16),
        pool_w=pool_w[0].astype(BF16),
        pool_scale=pool_scale[0].reshape(1, POOL_W),
        w_br_pool=w_br_pool[0].astype(BF16),
        w_out=w_out[0].astype(BF16),
        norm2_g=norm2_g[0].reshape(1, D_MODEL),
        w_router_hi=w_router_hi, w_router_lo=w_router_lo,
        b_router=router_layout(router_expert_b[0], router_group_b[0]).reshape(1, ROUTER_W),
        expert_w_gate=expert_w_gate[0].astype(BF16),
        expert_w_up=expert_w_up[0].astype(BF16),
        expert_w_down=expert_w_down[0].astype(BF16),
    )
    lam_rows = jnp.stack([lambda_q1[0], lambda_k1[0], lambda_q2[0], lambda_k2[0]], axis=0)

    n_cond = 1 + b_lat
    cond = jnp.concatenate([c_ctx[None, :], c, jnp.zeros((16 - n_cond, D_MODEL), F32)], axis=0)
    mod = _adaln(cond, w_ada[0], b_ada[0])[:n_cond].reshape(n_cond, ADA_CHUNKS, D_MODEL)

    y_prompt, k_ctx, v_ctx = _layer(x_prompt, mod[:1], lp, None, None, lam_rows, kv_dtype=F32)
    cache = (cache_k, cache_v)
    y_sample, _, _ = _layer(x_sample, mod[1:], lp, _rope_tables(n_lat), cache, lam_rows, kv_dtype=BF16)

    new_cache_k = k_ctx.reshape(b_ctx, 1, n_ctx, N_HEADS, HEAD_W)
    new_cache_v = v_ctx.reshape(b_ctx, 1, n_ctx, N_HEADS, HEAD_W)
    return (y_prompt, y_sample, new_cache_k, new_cache_v)
```

```python
import functools
import math

import jax
import jax.numpy as jnp
from jax import lax
from jax.experimental import pallas as pl
from jax.experimental.pallas import tpu as pltpu

D_MODEL = 1024
GRID_W = 64
N_HEADS = 8
HEAD_DIM = 64
HEAD_W = 2 * HEAD_DIM
ATTN_W = N_HEADS * HEAD_W
POOL_GROUPS = 4
POOL_WINDOWS = (2, 4, 8, 16)
POOL_W = 512
POOL_GROUP_W = 128
IN_W = 3 * ATTN_W + POOL_W + 2 * D_MODEL
ROPE_THETA = 10000.0
ROPE_PAIRS = 16
N_GROUPS = 4
EXPERTS_PER_GROUP = 4
N_EXPERTS = 16
PAIRS_PER_GROUP = 6
N_CLASSES = N_GROUPS * PAIRS_PER_GROUP
PAIR_LO = (0, 0, 0, 1, 1, 2)
PAIR_HI = (1, 2, 3, 2, 3, 3)
D_EXPERT = 512
PACKED_W = D_MODEL // 2
MOE_TILE = 256
ADA_CHUNKS = 6
EPS = 1e-6
LAMBDA_INIT = 0.8 - 0.6 * math.exp(-0.0)

LANES = 128
ROUTER_W = 2 * LANES
NEG_BIG = -1e30
Q_SCALE = math.log2(math.e) * HEAD_DIM ** -0.5

F32 = jnp.float32
BF16 = jnp.bfloat16

VMEM_LIMIT = 56 * 1024 * 1024


def _cparams(*sem):
    return pltpu.CompilerParams(dimension_semantics=sem, vmem_limit_bytes=VMEM_LIMIT)


def _split_bf16(x):
    hi = x.astype(BF16)
    lo = (x - hi.astype(F32)).astype(BF16)
    return hi, lo


def _dot(a, b):
    return jnp.dot(a, b, preferred_element_type=F32)


def _adaln_kernel(cond_ref, w_ref, b_ref, o_ref):
    c = cond_ref[...]
    s = c * jax.nn.sigmoid(c)
    s_hi, s_lo = _split_bf16(s)
    w_hi, w_lo = _split_bf16(w_ref[...])
    rows = s.shape[0]
    both = _dot(jnp.concatenate([s_hi, s_lo], axis=0), w_hi)
    o_ref[...] = both[:rows] + both[rows:] + _dot(s_hi, w_lo) + b_ref[...]


def _adaln(cond, w_ada, b_ada):
    rows = cond.shape[0]
    n = w_ada.shape[1]
    tn = 1536
    return pl.pallas_call(
        _adaln_kernel,
        grid=(n // tn,),
        in_specs=[pl.BlockSpec((rows, D_MODEL), lambda j: (0, 0)),
                  pl.BlockSpec((D_MODEL, tn), lambda j: (0, j)),
                  pl.BlockSpec((1, tn), lambda j: (0, j))],
        out_specs=pl.BlockSpec((rows, tn), lambda j: (0, j)),
        out_shape=jax.ShapeDtypeStruct((rows, n), F32),
        compiler_params=_cparams("arbitrary"),
        name="adaln",
    )(cond, w_ada, b_ada.reshape(1, n))


def _chunk_rms(z, seg):
    zz = (z * z).astype(BF16)
    parts = [_dot(zz[:, c * 256:(c + 1) * 256], seg) for c in range(z.shape[1] // 256)]
    return jnp.concatenate(parts, axis=1)


def _rope(x, cos, sin):
    lane = lax.broadcasted_iota(jnp.int32, (x.shape[0], LANES), 1)
    first = (lane % (2 * ROPE_PAIRS)) < ROPE_PAIRS
    parts = []
    for c in range(x.shape[1] // LANES):
        xc = x[:, c * LANES:(c + 1) * LANES]
        up = pltpu.roll(xc, LANES - ROPE_PAIRS, 1)
        dn = pltpu.roll(xc, ROPE_PAIRS, 1)
        parts.append(xc * cos + jnp.where(first, -up, dn) * sin)
    return jnp.concatenate(parts, axis=1)


def _pre_kernel(*refs, rope, kv_dtype):
    if rope:
        (x_ref, mod_ref, g1_ref, w_ref, bg_ref, qg_ref, kg_ref, seg_ref, cos_ref, sin_ref,
         q_out, k_out, v_out, p_out, g_out) = refs
    else:
        (x_ref, mod_ref, g1_ref, w_ref, bg_ref, qg_ref, kg_ref, seg_ref,
         q_out, k_out, v_out, p_out, g_out) = refs
    x = x_ref[...]
    shift = mod_ref[0:1, :]
    scale = mod_ref[1:2, :]
    xn = x * lax.rsqrt(jnp.mean(x * x, axis=-1, keepdims=True) + EPS) * g1_ref[...]
    h = (xn * (1.0 + scale) + shift).astype(BF16)
    seg = seg_ref[...]

    zq = _dot(h, w_ref[:, 0:ATTN_W])
    qn = zq * lax.rsqrt(_chunk_rms(zq, seg) + EPS) * qg_ref[...]
    if rope:
        qn = _rope(qn, cos_ref[...], sin_ref[...])
    q_out[...] = (qn * Q_SCALE).astype(BF16)

    zk = _dot(h, w_ref[:, ATTN_W:2 * ATTN_W])
    kn = zk * lax.rsqrt(_chunk_rms(zk, seg) + EPS) * kg_ref[...]
    if rope:
        kn = _rope(kn, cos_ref[...], sin_ref[...])
    k_out[...] = kn.astype(kv_dtype)

    v_out[...] = _dot(h, w_ref[:, 2 * ATTN_W:3 * ATTN_W]).astype(kv_dtype)
    p_out[...] = _dot(h, w_ref[:, 3 * ATTN_W:3 * ATTN_W + POOL_W]).astype(BF16)
    gl = _dot(h, w_ref[:, 3 * ATTN_W + POOL_W:IN_W]) + bg_ref[...]
    g_out[...] = jax.nn.sigmoid(gl).astype(BF16)


def _pre_mixer(x, mod, lp, rope_tabs, *, seq, tm, kv_dtype):
    t = x.shape[0]
    tiles_per_seq = seq // tm
    single_mod = mod.shape[0] == 1
    mod_idx = (lambda i: (0, 0, 0)) if single_mod else (lambda i: (i // tiles_per_seq, 0, 0))
    const = lambda i: (0, 0)
    row = lambda i: (i, 0)
    in_specs = [pl.BlockSpec((tm, D_MODEL), row),
                pl.BlockSpec((None, ADA_CHUNKS, D_MODEL), mod_idx),
                pl.BlockSpec((1, D_MODEL), const),
                pl.BlockSpec((D_MODEL, IN_W), const),
                pl.BlockSpec((1, 2 * D_MODEL), const),
                pl.BlockSpec((1, ATTN_W), const),
                pl.BlockSpec((1, ATTN_W), const),
                pl.BlockSpec((256, 256), const)]
    args = [x, mod, lp["norm1_g"], lp["w_in"], lp["b_gate"], lp["q_gain"], lp["k_gain"], lp["seg"]]
    rope = rope_tabs is not None
    if rope:
        in_specs += [pl.BlockSpec((tm, LANES), lambda i: (i % tiles_per_seq, 0))] * 2
        args += list(rope_tabs)
    out_shape = [jax.ShapeDtypeStruct((t, ATTN_W), BF16),
                 jax.ShapeDtypeStruct((t, ATTN_W), kv_dtype),
                 jax.ShapeDtypeStruct((t, ATTN_W), kv_dtype),
                 jax.ShapeDtypeStruct((t, POOL_W), BF16),
                 jax.ShapeDtypeStruct((t, 2 * D_MODEL), BF16)]
    out_specs = [pl.BlockSpec((tm, ATTN_W), row), pl.BlockSpec((tm, ATTN_W), row),
                 pl.BlockSpec((tm, ATTN_W), row), pl.BlockSpec((tm, POOL_W), row),
                 pl.BlockSpec((tm, 2 * D_MODEL), row)]
    return pl.pallas_call(
        functools.partial(_pre_kernel, rope=rope, kv_dtype=kv_dtype),
        grid=(t // tm,),
        in_specs=in_specs, out_specs=out_specs, out_shape=out_shape,
        compiler_params=_cparams("arbitrary"),
        name="pre_mixer_rope" if rope else "pre_mixer",
    )(*args)


def _attn_kernel(*refs, n_q_blocks, tq, has_cache):
    if has_cache:
        q_ref, k_ref, v_ref, ck_ref, cv_ref, lam_ref, sg_ref, o_ref, k1_s, k2_s, v_s = refs
    else:
        q_ref, k_ref, v_ref, lam_ref, sg_ref, o_ref, k1_s, k2_s, v_s = refs
    lv = lam_ref[...]
    lam = (jnp.exp(jnp.sum(lv[0:1] * lv[1:2], axis=-1, keepdims=True))
           - jnp.exp(jnp.sum(lv[2:3] * lv[3:4], axis=-1, keepdims=True)) + LAMBDA_INIT)
    nt = (((1,), (1,)), ((), ()))
    sg = sg_ref[...] * (1.0 - LAMBDA_INIT)
    n = k_ref.shape[0]

    def stage(rows, kf, vf):
        kf = kf.astype(F32)
        lane = lax.broadcasted_iota(jnp.int32, kf.shape, 1)
        k1_s[rows, :] = jnp.where(lane < HEAD_DIM, kf, 0.0).astype(BF16)
        k2_s[rows, :] = jnp.where(lane >= HEAD_DIM, kf, 0.0).astype(BF16)
        v_s[rows, :] = vf.astype(BF16)

    for h in range(N_HEADS):
        cols = slice(h * HEAD_W, (h + 1) * HEAD_W)
        stage(slice(0, n), k_ref[:, cols], v_ref[:, cols])
        if has_cache:
            stage(slice(n, k1_s.shape[0]), ck_ref[:, h, :], cv_ref[:, h, :])

        def softmax_parts(q, k_s):
            s = lax.dot_general(q, k_s[...], nt, preferred_element_type=F32)
            e = jnp.exp2(s - jnp.max(s, axis=-1, keepdims=True))
            return e.astype(BF16), jnp.sum(e, axis=-1, keepdims=True)

        def block(i, carry):
            qs = pl.multiple_of(i * tq, tq)
            q = q_ref[pl.ds(qs, tq), cols]
            e1, d1 = softmax_parts(q, k1_s)
            e2, d2 = softmax_parts(q, k2_s)
            c = (lam * d1 / d2).astype(BF16)
            o = _dot(e1 - c * e2, v_s[...]) / d1
            on = o * lax.rsqrt(jnp.mean(o * o, axis=-1, keepdims=True) + EPS) * sg
            o_ref[pl.ds(qs, tq), cols] = on.astype(o_ref.dtype)
            return carry

        if n_q_blocks == 1:
            block(0, 0)
        else:
            lax.fori_loop(0, n_q_blocks, block, 0, unroll=2)


def _attention(q, k, v, cache, lam_rows, subln_g, *, tq):
    b, n, _ = q.shape
    has_cache = cache is not None
    seq = lambda bi: (bi, 0, 0)
    const = lambda bi: (0, 0)
    in_specs = [pl.BlockSpec((None, n, ATTN_W), seq)] * 3
    args = [q, k, v]
    if has_cache:
        p_len = cache[0].shape[2]
        in_specs += [pl.BlockSpec((None, None, p_len, N_HEADS, HEAD_W), lambda bi: (bi, 0, 0, 0, 0))] * 2
        args += list(cache)
    in_specs += [pl.BlockSpec((4, HEAD_DIM), const), pl.BlockSpec((1, HEAD_W), const)]
    args += [lam_rows, subln_g]
    n_keys = n + (cache[0].shape[2] if has_cache else 0)
    return pl.pallas_call(
        functools.partial(_attn_kernel, n_q_blocks=n // tq, tq=tq, has_cache=has_cache),
        grid=(b,),
        in_specs=in_specs,
        out_specs=pl.BlockSpec((None, n, ATTN_W), seq),
        out_shape=jax.ShapeDtypeStruct((b, n, ATTN_W), BF16),
        scratch_shapes=[pltpu.VMEM((n_keys, HEAD_W), BF16)] * 3,
        compiler_params=_cparams("arbitrary"),
        name="diff_attn_cache" if has_cache else "diff_attn",
    )(*args)


def _route(logits):
    rows = logits.shape[0]
    lane_i = lax.broadcasted_iota(jnp.int32, (rows, LANES), 1)
    valid = lane_i < N_EXPERTS
    lane = lane_i.astype(F32)
    grp = (lane_i // EXPERTS_PER_GROUP).astype(F32)
    e_log = logits[:, :LANES]
    g_log = jnp.where(valid, logits[:, LANES:], NEG_BIG)
    g_max = jnp.max(g_log, axis=-1, keepdims=True)
    g_den = jnp.sum(jnp.exp(g_log - g_max), axis=-1, keepdims=True) * (1.0 / EXPERTS_PER_GROUP)
    g_w = 1.0 / g_den
    g_idx = jnp.min(jnp.where(g_log == g_max, grp, float(N_GROUPS)), axis=-1, keepdims=True)
    e_sel = jnp.where(grp == g_idx, jnp.where(valid, e_log, NEG_BIG), NEG_BIG)
    v1 = jnp.max(e_sel, axis=-1, keepdims=True)
    i1 = jnp.min(jnp.where(e_sel == v1, lane, float(LANES)), axis=-1, keepdims=True)
    e_rest = jnp.where(lane == i1, NEG_BIG, e_sel)
    v2 = jnp.max(e_rest, axis=-1, keepdims=True)
    i2 = jnp.min(jnp.where(e_rest == v2, lane, float(LANES)), axis=-1, keepdims=True)
    t = jnp.exp(v2 - v1)
    w1 = g_w / (1.0 + t)
    w2 = w1 * t
    first_low = i1 < i2
    a = jnp.minimum(i1, i2) - EXPERTS_PER_GROUP * g_idx
    b = jnp.maximum(i1, i2) - EXPERTS_PER_GROUP * g_idx
    pair = a * (7.0 - a) * 0.5 + (b - a - 1.0)
    cls = g_idx * PAIRS_PER_GROUP + pair
    return cls, jnp.where(first_low, w1, w2), jnp.where(first_low, w2, w1)


def _pack_bf16_pairs(x):
    n = x.shape[1] // 2
    hi = lax.bitcast_convert_type(x[:, :n].astype(BF16).astype(F32), jnp.uint32)
    lo = lax.bitcast_convert_type(x[:, n:].astype(BF16).astype(F32), jnp.uint32)
    return lax.bitcast_convert_type(hi | (lo >> 16), jnp.int32)


def _unpack_bf16_pairs(w):
    u = lax.bitcast_convert_type(w, jnp.uint32)
    hi = lax.bitcast_convert_type(u & jnp.uint32(0xFFFF0000), F32)
    lo = lax.bitcast_convert_type(u << 16, F32)
    return jnp.concatenate([hi, lo], axis=1)


def _post_kernel(o_ref, p_ref, g_ref, x_ref, mod_ref, wa_ref, pw_ref, ps_ref, wp_ref, wo_ref,
                 g2_ref, wrh_ref, wrl_ref, br_ref, x1_out, h2p_out, gw_out, cls_out, rank_out, counts_out,
                 carry_ref, *, seq, tm):
    first_step = (pl.program_id(0) == 0) & (pl.program_id(1) == 0)

    @pl.when(first_step)
    def _():
        carry_ref[...] = jnp.zeros_like(carry_ref)

    r0 = pl.multiple_of(pl.program_id(1) * tm, tm)
    attn_out = _dot(o_ref[...], wa_ref[...])

    p_all = p_ref[...]
    p_self = p_ref[pl.ds(r0, tm), :].astype(F32)
    tok = r0 + lax.broadcasted_iota(jnp.int32, (tm, seq), 0)
    col = lax.broadcasted_iota(jnp.int32, (tm, seq), 1)
    dist = jnp.abs(2 * (col - tok) + 1)
    tok1 = r0 + lax.broadcasted_iota(jnp.int32, (tm, 1), 0)
    mixed = []
    for gi, w in enumerate(POOL_WINDOWS):
        half = w // 2
        band = jnp.where(dist < w, 1.0, 0.0).astype(BF16)
        cnt = (jnp.minimum(tok1 + half, seq) - jnp.maximum(tok1 - half, 0)).astype(F32)
        sl = slice(gi * POOL_GROUP_W, (gi + 1) * POOL_GROUP_W)
        pooled = _dot(band, p_all[:, sl]) / cnt - p_self[:, sl]
        mixed.append(_dot(pooled.astype(BF16), pw_ref[gi]))
    mixed = jnp.concatenate(mixed, axis=1) * ps_ref[...]
    pool_out = _dot(mixed.astype(BF16), wp_ref[...])

    g = g_ref[...]
    merged = g[:, :D_MODEL].astype(F32) * attn_out + g[:, D_MODEL:].astype(F32) * pool_out
    gate1 = mod_ref[2:3, :]
    x1 = x_ref[...] + gate1 * _dot(merged.astype(BF16), wo_ref[...])
    x1_out[...] = x1

    shift2 = mod_ref[3:4, :]
    scale2 = mod_ref[4:5, :]
    xn = x1 * lax.rsqrt(jnp.mean(x1 * x1, axis=-1, keepdims=True) + EPS) * g2_ref[...]
    h2 = xn * (1.0 + scale2) + shift2
    h2_hi, h2_lo = _split_bf16(h2)
    h2p_out[...] = _pack_bf16_pairs(h2).reshape(tm, PACKED_W // LANES, LANES)

    logits = (_dot(h2_hi, wrh_ref[...]) + _dot(h2_lo, wrh_ref[...]) + _dot(h2_hi, wrl_ref[...])
              + br_ref[...])
    cls, w_lo, w_hi = _route(logits)
    lane = lax.broadcasted_iota(jnp.int32, (tm, LANES), 1)
    gw_out[...] = jnp.where(lane == 0, w_lo, jnp.where(lane == 1, w_hi, 0.0)).reshape(tm, 1, LANES)

    onehot = jnp.where(lane.astype(F32) == cls, 1.0, 0.0)
    row = lax.broadcasted_iota(jnp.int32, (tm, tm), 0)
    col = lax.broadcasted_iota(jnp.int32, (tm, tm), 1)
    before = jnp.where(col < row, 1.0, 0.0).astype(BF16)
    rank = _dot(before, onehot.astype(BF16)) + carry_ref[...]
    cls_out[...] = cls.astype(jnp.int32)
    rank_out[...] = jnp.sum(onehot * rank, axis=-1, keepdims=True).astype(jnp.int32)
    carry_ref[...] += jnp.sum(onehot, axis=0, keepdims=True)
    counts_out[...] = carry_ref[...]


def _post_mixer(o, p, g, x, mod, lp, *, tm):
    b, n, _ = x.shape
    single_mod = mod.shape[0] == 1
    mod_idx = (lambda bi, ri: (0, 0, 0)) if single_mod else (lambda bi, ri: (bi, 0, 0))
    tile = lambda bi, ri: (bi, ri, 0)
    const2 = lambda bi, ri: (0, 0)
    const3 = lambda bi, ri: (0, 0, 0)
    in_specs = [pl.BlockSpec((None, tm, ATTN_W), tile),
                pl.BlockSpec((None, n, POOL_W), lambda bi, ri: (bi, 0, 0)),
                pl.BlockSpec((None, tm, 2 * D_MODEL), tile),
                pl.BlockSpec((None, tm, D_MODEL), tile),
                pl.BlockSpec((None, ADA_CHUNKS, D_MODEL), mod_idx),
                pl.BlockSpec((ATTN_W, D_MODEL), const2),
                pl.BlockSpec((POOL_GROUPS, POOL_GROUP_W, POOL_GROUP_W), const3),
                pl.BlockSpec((1, POOL_W), const2),
                pl.BlockSpec((POOL_W, D_MODEL), const2),
                pl.BlockSpec((D_MODEL, D_MODEL), const2),
                pl.BlockSpec((1, D_MODEL), const2),
                pl.BlockSpec((D_MODEL, ROUTER_W), const2),
                pl.BlockSpec((D_MODEL, ROUTER_W), const2),
                pl.BlockSpec((1, ROUTER_W), const2)]
    tile4 = lambda bi, ri: (bi, ri, 0, 0)
    sub = PACKED_W // LANES
    out_shape = [jax.ShapeDtypeStruct((b, n, D_MODEL), F32),
                 jax.ShapeDtypeStruct((b, n, sub, LANES), jnp.int32),
                 jax.ShapeDtypeStruct((b, n, 1, LANES), F32),
                 jax.ShapeDtypeStruct((b, n, 1), jnp.int32),
                 jax.ShapeDtypeStruct((b, n, 1), jnp.int32),
                 jax.ShapeDtypeStruct((1, LANES), F32)]
    out_specs = [pl.BlockSpec((None, tm, D_MODEL), tile),
                 pl.BlockSpec((None, tm, sub, LANES), tile4),
                 pl.BlockSpec((None, tm, 1, LANES), tile4),
                 pl.BlockSpec((None, tm, 1), tile),
                 pl.BlockSpec((None, tm, 1), tile),
                 pl.BlockSpec((1, LANES), const2)]
    return pl.pallas_call(
        functools.partial(_post_kernel, seq=n, tm=tm),
        grid=(b, n // tm),
        in_specs=in_specs, out_specs=out_specs, out_shape=out_shape,
        scratch_shapes=[pltpu.VMEM((1, LANES), F32)],
        compiler_params=_cparams("arbitrary", "arbitrary"),
        name="post_mixer",
    )(o, p, g, x, mod, lp["w_br_attn"], lp["pool_w"], lp["pool_scale"], lp["w_br_pool"], lp["w_out"],
      lp["norm2_g"], lp["w_router_hi"], lp["w_router_lo"], lp["b_router"])


def _moe_tables(counts, n_tokens):
    cnt = counts[0, :N_CLASSES].astype(jnp.int32)
    tiles = (cnt + MOE_TILE - 1) // MOE_TILE
    ends = jnp.cumsum(tiles)
    off = (ends - tiles) * MOE_TILE
    n_tiles = ends[-1:]
    max_tiles = n_tokens // MOE_TILE + N_CLASSES
    tile_cls = jnp.searchsorted(ends, jnp.minimum(jnp.arange(max_tiles), n_tiles - 1), side="right")
    group, pair = tile_cls // PAIRS_PER_GROUP, tile_cls % PAIRS_PER_GROUP
    e_lo = group * EXPERTS_PER_GROUP + jnp.asarray(PAIR_LO, jnp.int32)[pair]
    e_hi = group * EXPERTS_PER_GROUP + jnp.asarray(PAIR_HI, jnp.int32)[pair]
    return off.astype(jnp.int32), e_lo.astype(jnp.int32), e_hi.astype(jnp.int32), n_tiles.astype(jnp.int32)


def _moe_kernel(cls_s, rank_s, off_s, elo_s, ehi_s, nt_s, h_ref, gw_ref, wg1_ref, wu1_ref, wd1_ref,
                wg2_ref, wu2_ref, wd2_ref, o_ref, src_s, xg_ref, gwg_ref, *, n_tokens):
    j = pl.program_id(0)

    @pl.when(j == 0)
    def _():
        def clear(i, c):
            src_s[i] = 0
            return c

        lax.fori_loop(0, src_s.shape[0], clear, 0, unroll=8)

        def place(t, c):
            src_s[off_s[cls_s[t]] + rank_s[t]] = t
            return c

        lax.fori_loop(0, n_tokens, place, 0, unroll=8)

    @pl.when(j < nt_s[0])
    def _():
        base = j * MOE_TILE

        def gather(r, c):
            t = src_s[base + r]
            xg_ref[r] = h_ref[t]
            gwg_ref[r] = gw_ref[t]
            return c

        lax.fori_loop(0, MOE_TILE, gather, 0, unroll=8)
        x = _unpack_bf16_pairs(xg_ref[...].reshape(MOE_TILE, PACKED_W)).astype(BF16)
        gw = gwg_ref[...].reshape(MOE_TILE, LANES)

        def hidden(wg_ref, wu_ref, gate):
            a = _dot(x, wg_ref[...])
            u = _dot(x, wu_ref[...])
            return (a * jax.nn.sigmoid(a) * u * gate).astype(BF16)

        o = (_dot(hidden(wg1_ref, wu1_ref, gw[:, 0:1]), wd1_ref[...])
             + _dot(hidden(wg2_ref, wu2_ref, gw[:, 1:2]), wd2_ref[...]))
        o_ref[...] = _pack_bf16_pairs(o).reshape(o_ref.shape)

    @pl.when(j >= nt_s[0])
    def _():
        o_ref[...] = jnp.zeros_like(o_ref)


def _moe(h2p, gw, cls, rank, tables, lp):
    t, sub, _ = h2p.shape
    off, e_lo, e_hi, n_tiles = tables
    max_tiles = e_lo.shape[0]
    whole = lambda j, *_: (0, 0, 0)
    lo = lambda j, cls_s, rank_s, off_s, elo_s, ehi_s, nt_s: (elo_s[j], 0, 0)
    hi = lambda j, cls_s, rank_s, off_s, elo_s, ehi_s, nt_s: (ehi_s[j], 0, 0)
    w_in = lambda idx: pl.BlockSpec((None, D_MODEL, D_EXPERT), idx)
    w_out = lambda idx: pl.BlockSpec((None, D_EXPERT, D_MODEL), idx)
    grid_spec = pltpu.PrefetchScalarGridSpec(
        num_scalar_prefetch=6,
        grid=(max_tiles,),
        in_specs=[pl.BlockSpec((t, sub, LANES), whole, pipeline_mode=pl.Buffered(1)),
                  pl.BlockSpec((t, 1, LANES), whole, pipeline_mode=pl.Buffered(1)),
                  w_in(lo), w_in(lo), w_out(lo), w_in(hi), w_in(hi), w_out(hi)],
        out_specs=pl.BlockSpec((MOE_TILE, sub, LANES), lambda j, *_: (j, 0, 0)),
        scratch_shapes=[pltpu.SMEM((max_tiles * MOE_TILE,), jnp.int32),
                        pltpu.VMEM((MOE_TILE, sub, LANES), jnp.int32),
                        pltpu.VMEM((MOE_TILE, 1, LANES), F32)])
    wg, wu, wd = lp["expert_w_gate"], lp["expert_w_up"], lp["expert_w_down"]
    return pl.pallas_call(
        functools.partial(_moe_kernel, n_tokens=t),
        grid_spec=grid_spec,
        out_shape=jax.ShapeDtypeStruct((max_tiles * MOE_TILE, sub, LANES), jnp.int32),
        compiler_params=_cparams("arbitrary"),
        name="moe",
    )(cls, rank, off, e_lo, e_hi, n_tiles, h2p, gw, wg, wu, wd, wg, wu, wd)


def _combine_kernel(cls_s, rank_s, off_s, o_ref, x1_ref, mod_ref, y_ref, og_ref, *, tm):
    base = pl.program_id(0) * tm

    def gather(r, c):
        t = base + r
        og_ref[r] = o_ref[off_s[cls_s[t]] + rank_s[t]]
        return c

    lax.fori_loop(0, tm, gather, 0, unroll=8)
    moe = _unpack_bf16_pairs(og_ref[...].reshape(tm, PACKED_W))
    y_ref[...] = x1_ref[...] + mod_ref[5:6, :] * moe


def _combine(o_sorted, x1, mod, cls, rank, off, *, seq, tm):
    t = x1.shape[0]
    rows, sub, _ = o_sorted.shape
    tiles_per_seq = seq // tm
    single_mod = mod.shape[0] == 1
    mod_idx = (lambda i, *_: (0, 0, 0)) if single_mod else (lambda i, *_: (i // tiles_per_seq, 0, 0))
    row = lambda i, *_: (i, 0)
    grid_spec = pltpu.PrefetchScalarGridSpec(
        num_scalar_prefetch=3,
        grid=(t // tm,),
        in_specs=[pl.BlockSpec((rows, sub, LANES), lambda i, *_: (0, 0, 0), pipeline_mode=pl.Buffered(1)),
                  pl.BlockSpec((tm, D_MODEL), row),
                  pl.BlockSpec((None, ADA_CHUNKS, D_MODEL), mod_idx)],
        out_specs=pl.BlockSpec((tm, D_MODEL), row),
        scratch_shapes=[pltpu.VMEM((tm, sub, LANES), jnp.int32)])
    return pl.pallas_call(
        functools.partial(_combine_kernel, tm=tm),
        grid_spec=grid_spec,
        out_shape=jax.ShapeDtypeStruct((t, D_MODEL), F32),
        compiler_params=_cparams("arbitrary"),
        name="moe_combine",
    )(cls, rank, off, o_sorted, x1, mod)


def _rope_tables(n_tokens):
    rows = n_tokens // GRID_W
    row_ids = jnp.repeat(jnp.arange(rows, dtype=F32), GRID_W)
    col_ids = jnp.tile(jnp.arange(GRID_W, dtype=F32), rows)
    inv_freq = jnp.power(ROPE_THETA, -jnp.arange(ROPE_PAIRS, dtype=F32) / ROPE_PAIRS)
    ang_r = row_ids[:, None] * inv_freq[None, :]
    ang_c = col_ids[:, None] * inv_freq[None, :]
    ang = jnp.concatenate([ang_r, ang_r, ang_c, ang_c] * 2, axis=-1)
    return jnp.cos(ang), jnp.sin(ang)


def _layer(x, mod, lp, rope_tabs, cache, lam_rows, *, kv_dtype):
    b, n, _ = x.shape
    t = b * n
    q, k, v, p, g = _pre_mixer(x.reshape(t, D_MODEL), mod, lp, rope_tabs, seq=n, tm=256, kv_dtype=kv_dtype)
    o = _attention(q.reshape(b, n, ATTN_W), k.reshape(b, n, ATTN_W), v.reshape(b, n, ATTN_W),
                   cache, lam_rows, lp["subln_g"], tq=256)
    x1, h2p, gw, cls, rank, counts = _post_mixer(o, p.reshape(b, n, POOL_W), g.reshape(b, n, 2 * D_MODEL),
                                                 x, mod, lp, tm=256)
    tables = _moe_tables(counts, t)
    cls, rank = cls.reshape(t), rank.reshape(t)
    o_sorted = _moe(h2p.reshape(t, PACKED_W // LANES, LANES), gw.reshape(t, 1, LANES), cls, rank, tables, lp)
    y = _combine(o_sorted, x1.reshape(t, D_MODEL), mod, cls, rank, tables[0], seq=n, tm=256)
    return y.reshape(b, n, D_MODEL), k, v


def kernel(x_prompt, x_sample, c, cache_k, cache_v, c_ctx, w_ada, b_ada, norm1_g, w_in, b_gate, q_norm_g, k_norm_g, lambda_q1, lambda_k1, lambda_q2, lambda_k2, subln_g, pool_w, pool_scale, w_br_attn, w_br_pool, w_out, norm2_g, router_group_w, router_group_b, router_expert_w, router_expert_b, expert_w_gate, expert_w_up, expert_w_down):
    b_ctx, n_ctx, _ = x_prompt.shape
    b_lat, n_lat, _ = x_sample.shape

    def router_layout(we, wg):
        pad = jnp.zeros(we.shape[:-1] + (LANES - N_EXPERTS,), F32)
        return jnp.concatenate([we, pad, jnp.repeat(wg, EXPERTS_PER_GROUP, axis=-1), pad], axis=-1)

    w_router = router_layout(router_expert_w[0], router_group_w[0])
    w_router_hi, w_router_lo = _split_bf16(w_router)
    seg = (jnp.arange(256)[:, None] // HEAD_DIM == jnp.arange(256)[None, :] // HEAD_DIM)
    lp = dict(
        norm1_g=norm1_g[0].reshape(1, D_MODEL),
        w_in=w_in[0].astype(BF16),
        b_gate=b_gate[0].reshape(1, 2 * D_MODEL),
        q_gain=jnp.tile(q_norm_g[0], ATTN_W // HEAD_DIM).reshape(1, ATTN_W),
        k_gain=jnp.tile(k_norm_g[0], ATTN_W // HEAD_DIM).reshape(1, ATTN_W),
        seg=(seg.astype(F32) / HEAD_DIM).astype(BF16),
        subln_g=subln_g[0].reshape(1, HEAD_W),
        w_br_attn=w_br_attn[0].astype(BF16),
        pool_w=pool_w[0].astype(BF16),
        pool_scale=pool_scale[0].reshape(1, POOL_W),
        w_br_pool=w_br_pool[0].astype(BF16),
        w_out=w_out[0].astype(BF16),
        norm2_g=norm2_g[0].reshape(1, D_MODEL),
        w_router_hi=w_router_hi, w_router_lo=w_router_lo,
        b_router=router_layout(router_expert_b[0], router_group_b[0]).reshape(1, ROUTER_W),
        expert_w_gate=expert_w_gate[0].astype(BF16),
        expert_w_up=expert_w_up[0].astype(BF16),
        expert_w_down=expert_w_down[0].astype(BF16),
    )
    lam_rows = jnp.stack([lambda_q1[0], lambda_k1[0], lambda_q2[0], lambda_k2[0]], axis=0)

    n_cond = 1 + b_lat
    cond = jnp.concatenate([c_ctx[None, :], c, jnp.zeros((16 - n_cond, D_MODEL), F32)], axis=0)
    mod = _adaln(cond, w_ada[0], b_ada[0])[:n_cond].reshape(n_cond, ADA_CHUNKS, D_MODEL)

    y_prompt, k_ctx, v_ctx = _layer(x_prompt, mod[:1], lp, None, None, lam_rows, kv_dtype=F32)
    cache = (cache_k, cache_v)
    y_sample, _, _ = _layer(x_sample, mod[1:], lp, _rope_tables(n_lat), cache, lam_rows, kv_dtype=BF16)

    new_cache_k = k_ctx.reshape(b_ctx, 1, n_ctx, N_HEADS, HEAD_W)
    new_cache_v = v_ctx.reshape(b_ctx, 1, n_ctx, N_HEADS, HEAD_W)
    return (y_prompt, y_sample, new_cache_k, new_cache_v)
```

```python
import functools
import math

import jax
import jax.numpy as jnp
from jax import lax
from jax.experimental import pallas as pl
from jax.experimental.pallas import tpu as pltpu

D_MODEL = 1024
GRID_W = 64
N_HEADS = 8
HEAD_DIM = 64
HEAD_W = 2 * HEAD_DIM
ATTN_W = N_HEADS * HEAD_W
POOL_GROUPS = 4
POOL_WINDOWS = (2, 4, 8, 16)
POOL_W = 512
POOL_GROUP_W = 128
IN_W = 3 * ATTN_W + POOL_W + 2 * D_MODEL
ROPE_THETA = 10000.0
ROPE_PAIRS = 16
N_GROUPS = 4
EXPERTS_PER_GROUP = 4
N_EXPERTS = 16
PAIRS_PER_GROUP = 6
N_CLASSES = N_GROUPS * PAIRS_PER_GROUP
PAIR_LO = (0, 0, 0, 1, 1, 2)
PAIR_HI = (1, 2, 3, 2, 3, 3)
D_EXPERT = 512
MOE_TILE = 256
ADA_CHUNKS = 6
EPS = 1e-6
LAMBDA_INIT = 0.8 - 0.6 * math.exp(-0.0)

LANES = 128
ROW_SUB = D_MODEL // LANES
ROUTER_W = 2 * LANES
NEG_BIG = -1e30
Q_SCALE = math.log2(math.e) * HEAD_DIM ** -0.5

F32 = jnp.float32
BF16 = jnp.bfloat16

VMEM_LIMIT = 56 * 1024 * 1024


def _cparams(*sem):
    return pltpu.CompilerParams(dimension_semantics=sem, vmem_limit_bytes=VMEM_LIMIT)


def _split_bf16(x):
    hi = x.astype(BF16)
    lo = (x - hi.astype(F32)).astype(BF16)
    return hi, lo


def _dot(a, b):
    return jnp.dot(a, b, preferred_element_type=F32)


def _adaln_kernel(cond_ref, w_ref, b_ref, o_ref):
    c = cond_ref[...]
    s = c * jax.nn.sigmoid(c)
    s_hi, s_lo = _split_bf16(s)
    w_hi, w_lo = _split_bf16(w_ref[...])
    rows = s.shape[0]
    both = _dot(jnp.concatenate([s_hi, s_lo], axis=0), w_hi)
    o_ref[...] = both[:rows] + both[rows:] + _dot(s_hi, w_lo) + b_ref[...]


def _adaln(cond, w_ada, b_ada):
    rows = cond.shape[0]
    n = w_ada.shape[1]
    tn = 1536
    return pl.pallas_call(
        _adaln_kernel,
        grid=(n // tn,),
        in_specs=[pl.BlockSpec((rows, D_MODEL), lambda j: (0, 0)),
                  pl.BlockSpec((D_MODEL, tn), lambda j: (0, j)),
                  pl.BlockSpec((1, tn), lambda j: (0, j))],
        out_specs=pl.BlockSpec((rows, tn), lambda j: (0, j)),
        out_shape=jax.ShapeDtypeStruct((rows, n), F32),
        compiler_params=_cparams("arbitrary"),
        name="adaln",
    )(cond, w_ada, b_ada.reshape(1, n))


def _chunk_rms(z, seg):
    zz = (z * z).astype(BF16)
    parts = [_dot(zz[:, c * 256:(c + 1) * 256], seg) for c in range(z.shape[1] // 256)]
    return jnp.concatenate(parts, axis=1)


def _rope(x, cos, sin):
    lane = lax.broadcasted_iota(jnp.int32, (x.shape[0], LANES), 1)
    first = (lane % (2 * ROPE_PAIRS)) < ROPE_PAIRS
    parts = []
    for c in range(x.shape[1] // LANES):
        xc = x[:, c * LANES:(c + 1) * LANES]
        up = pltpu.roll(xc, LANES - ROPE_PAIRS, 1)
        dn = pltpu.roll(xc, ROPE_PAIRS, 1)
        parts.append(xc * cos + jnp.where(first, -up, dn) * sin)
    return jnp.concatenate(parts, axis=1)


def _pre_kernel(*refs, rope, kv_dtype):
    if rope:
        (x_ref, mod_ref, g1_ref, w_ref, bg_ref, qg_ref, kg_ref, seg_ref, cos_ref, sin_ref,
         q_out, k_out, v_out, p_out, g_out) = refs
    else:
        (x_ref, mod_ref, g1_ref, w_ref, bg_ref, qg_ref, kg_ref, seg_ref,
         q_out, k_out, v_out, p_out, g_out) = refs
    x = x_ref[...]
    shift = mod_ref[0:1, :]
    scale = mod_ref[1:2, :]
    xn = x * lax.rsqrt(jnp.mean(x * x, axis=-1, keepdims=True) + EPS) * g1_ref[...]
    h = (xn * (1.0 + scale) + shift).astype(BF16)
    seg = seg_ref[...]

    zq = _dot(h, w_ref[:, 0:ATTN_W])
    qn = zq * lax.rsqrt(_chunk_rms(zq, seg) + EPS) * qg_ref[...]
    if rope:
        qn = _rope(qn, cos_ref[...], sin_ref[...])
    q_out[...] = (qn * Q_SCALE).astype(BF16)

    zk = _dot(h, w_ref[:, ATTN_W:2 * ATTN_W])
    kn = zk * lax.rsqrt(_chunk_rms(zk, seg) + EPS) * kg_ref[...]
    if rope:
        kn = _rope(kn, cos_ref[...], sin_ref[...])
    k_out[...] = kn.astype(kv_dtype)

    v_out[...] = _dot(h, w_ref[:, 2 * ATTN_W:3 * ATTN_W]).astype(kv_dtype)
    p_out[...] = _dot(h, w_ref[:, 3 * ATTN_W:3 * ATTN_W + POOL_W]).astype(BF16)
    gl = _dot(h, w_ref[:, 3 * ATTN_W + POOL_W:IN_W]) + bg_ref[...]
    g_out[...] = jax.nn.sigmoid(gl).astype(BF16)


def _pre_mixer(x, mod, lp, rope_tabs, *, seq, tm, kv_dtype):
    t = x.shape[0]
    tiles_per_seq = seq // tm
    single_mod = mod.shape[0] == 1
    mod_idx = (lambda i: (0, 0, 0)) if single_mod else (lambda i: (i // tiles_per_seq, 0, 0))
    const = lambda i: (0, 0)
    row = lambda i: (i, 0)
    in_specs = [pl.BlockSpec((tm, D_MODEL), row),
                pl.BlockSpec((None, ADA_CHUNKS, D_MODEL), mod_idx),
                pl.BlockSpec((1, D_MODEL), const),
                pl.BlockSpec((D_MODEL, IN_W), const),
                pl.BlockSpec((1, 2 * D_MODEL), const),
                pl.BlockSpec((1, ATTN_W), const),
                pl.BlockSpec((1, ATTN_W), const),
                pl.BlockSpec((256, 256), const)]
    args = [x, mod, lp["norm1_g"], lp["w_in"], lp["b_gate"], lp["q_gain"], lp["k_gain"], lp["seg"]]
    rope = rope_tabs is not None
    if rope:
        in_specs += [pl.BlockSpec((tm, LANES), lambda i: (i % tiles_per_seq, 0))] * 2
        args += list(rope_tabs)
    out_shape = [jax.ShapeDtypeStruct((t, ATTN_W), BF16),
                 jax.ShapeDtypeStruct((t, ATTN_W), kv_dtype),
                 jax.ShapeDtypeStruct((t, ATTN_W), kv_dtype),
                 jax.ShapeDtypeStruct((t, POOL_W), BF16),
                 jax.ShapeDtypeStruct((t, 2 * D_MODEL), BF16)]
    out_specs = [pl.BlockSpec((tm, ATTN_W), row), pl.BlockSpec((tm, ATTN_W), row),
                 pl.BlockSpec((tm, ATTN_W), row), pl.BlockSpec((tm, POOL_W), row),
                 pl.BlockSpec((tm, 2 * D_MODEL), row)]
    return pl.pallas_call(
        functools.partial(_pre_kernel, rope=rope, kv_dtype=kv_dtype),
        grid=(t // tm,),
        in_specs=in_specs, out_specs=out_specs, out_shape=out_shape,
        compiler_params=_cparams("arbitrary"),
        name="pre_mixer_rope" if rope else "pre_mixer",
    )(*args)


def _attn_kernel(*refs, n_q_blocks, tq, has_cache):
    if has_cache:
        q_ref, k_ref, v_ref, ck_ref, cv_ref, lam_ref, sg_ref, o_ref, k1_s, k2_s, v_s = refs
    else:
        q_ref, k_ref, v_ref, lam_ref, sg_ref, o_ref, k1_s, k2_s, v_s = refs
    lv = lam_ref[...]
    lam = (jnp.exp(jnp.sum(lv[0:1] * lv[1:2], axis=-1, keepdims=True))
           - jnp.exp(jnp.sum(lv[2:3] * lv[3:4], axis=-1, keepdims=True)) + LAMBDA_INIT)
    nt = (((1,), (1,)), ((), ()))
    sg = sg_ref[...] * (1.0 - LAMBDA_INIT)
    n = k_ref.shape[0]

    def stage(rows, kf, vf):
        kf = kf.astype(F32)
        lane = lax.broadcasted_iota(jnp.int32, kf.shape, 1)
        k1_s[rows, :] = jnp.where(lane < HEAD_DIM, kf, 0.0).astype(BF16)
        k2_s[rows, :] = jnp.where(lane >= HEAD_DIM, kf, 0.0).astype(BF16)
        v_s[rows, :] = vf.astype(BF16)

    for h in range(N_HEADS):
        cols = slice(h * HEAD_W, (h + 1) * HEAD_W)
        stage(slice(0, n), k_ref[:, cols], v_ref[:, cols])
        if has_cache:
            stage(slice(n, k1_s.shape[0]), ck_ref[:, h, :], cv_ref[:, h, :])

        def softmax_parts(q, k_s):
            s = lax.dot_general(q, k_s[...], nt, preferred_element_type=F32)
            e = jnp.exp2(s - jnp.max(s, axis=-1, keepdims=True))
            return e.astype(BF16), jnp.sum(e, axis=-1, keepdims=True)

        def block(i, carry):
            qs = pl.multiple_of(i * tq, tq)
            q = q_ref[pl.ds(qs, tq), cols]
            e1, d1 = softmax_parts(q, k1_s)
            e2, d2 = softmax_parts(q, k2_s)
            c = (lam * d1 / d2).astype(BF16)
            o = _dot(e1 - c * e2, v_s[...]) / d1
            on = o * lax.rsqrt(jnp.mean(o * o, axis=-1, keepdims=True) + EPS) * sg
            o_ref[pl.ds(qs, tq), cols] = on.astype(o_ref.dtype)
            return carry

        if n_q_blocks == 1:
            block(0, 0)
        else:
            lax.fori_loop(0, n_q_blocks, block, 0, unroll=2)


def _attention(q, k, v, cache, lam_rows, subln_g, *, tq):
    b, n, _ = q.shape
    has_cache = cache is not None
    seq = lambda bi: (bi, 0, 0)
    const = lambda bi: (0, 0)
    in_specs = [pl.BlockSpec((None, n, ATTN_W), seq)] * 3
    args = [q, k, v]
    if has_cache:
        p_len = cache[0].shape[2]
        in_specs += [pl.BlockSpec((None, None, p_len, N_HEADS, HEAD_W), lambda bi: (bi, 0, 0, 0, 0))] * 2
        args += list(cache)
    in_specs += [pl.BlockSpec((4, HEAD_DIM), const), pl.BlockSpec((1, HEAD_W), const)]
    args += [lam_rows, subln_g]
    n_keys = n + (cache[0].shape[2] if has_cache else 0)
    return pl.pallas_call(
        functools.partial(_attn_kernel, n_q_blocks=n // tq, tq=tq, has_cache=has_cache),
        grid=(b,),
        in_specs=in_specs,
        out_specs=pl.BlockSpec((None, n, ATTN_W), seq),
        out_shape=jax.ShapeDtypeStruct((b, n, ATTN_W), BF16),
        scratch_shapes=[pltpu.VMEM((n_keys, HEAD_W), BF16)] * 3,
        compiler_params=_cparams("arbitrary"),
        name="diff_attn_cache" if has_cache else "diff_attn",
    )(*args)


def _route(logits):
    rows = logits.shape[0]
    lane_i = lax.broadcasted_iota(jnp.int32, (rows, LANES), 1)
    valid = lane_i < N_EXPERTS
    lane = lane_i.astype(F32)
    grp = (lane_i // EXPERTS_PER_GROUP).astype(F32)
    e_log = logits[:, :LANES]
    g_log = jnp.where(valid, logits[:, LANES:], NEG_BIG)
    g_max = jnp.max(g_log, axis=-1, keepdims=True)
    g_den = jnp.sum(jnp.exp(g_log - g_max), axis=-1, keepdims=True) * (1.0 / EXPERTS_PER_GROUP)
    g_w = 1.0 / g_den
    g_idx = jnp.min(jnp.where(g_log == g_max, grp, float(N_GROUPS)), axis=-1, keepdims=True)
    e_sel = jnp.where(grp == g_idx, jnp.where(valid, e_log, NEG_BIG), NEG_BIG)
    v1 = jnp.max(e_sel, axis=-1, keepdims=True)
    i1 = jnp.min(jnp.where(e_sel == v1, lane, float(LANES)), axis=-1, keepdims=True)
    e_rest = jnp.where(lane == i1, NEG_BIG, e_sel)
    v2 = jnp.max(e_rest, axis=-1, keepdims=True)
    i2 = jnp.min(jnp.where(e_rest == v2, lane, float(LANES)), axis=-1, keepdims=True)
    t = jnp.exp(v2 - v1)
    w1 = g_w / (1.0 + t)
    w2 = w1 * t
    first_low = i1 < i2
    a = jnp.minimum(i1, i2) - EXPERTS_PER_GROUP * g_idx
    b = jnp.maximum(i1, i2) - EXPERTS_PER_GROUP * g_idx
    pair = a * (7.0 - a) * 0.5 + (b - a - 1.0)
    cls = g_idx * PAIRS_PER_GROUP + pair
    return cls, jnp.where(first_low, w1, w2), jnp.where(first_low, w2, w1)


def _post_kernel(o_ref, p_ref, g_ref, x_ref, mod_ref, wa_ref, pw_ref, ps_ref, wp_ref, wo_ref,
                 g2_ref, wrh_ref, wrl_ref, br_ref, x1_out, h2p_out, gw_out, cls_out, rank_out, counts_out,
                 carry_ref, *, seq, tm):
    first_step = (pl.program_id(0) == 0) & (pl.program_id(1) == 0)

    @pl.when(first_step)
    def _():
        carry_ref[...] = jnp.zeros_like(carry_ref)

    r0 = pl.multiple_of(pl.program_id(1) * tm, tm)
    attn_out = _dot(o_ref[...], wa_ref[...])

    p_all = p_ref[...]
    p_self = p_ref[pl.ds(r0, tm), :].astype(F32)
    tok = r0 + lax.broadcasted_iota(jnp.int32, (tm, seq), 0)
    col = lax.broadcasted_iota(jnp.int32, (tm, seq), 1)
    dist = jnp.abs(2 * (col - tok) + 1)
    tok1 = r0 + lax.broadcasted_iota(jnp.int32, (tm, 1), 0)
    mixed = []
    for gi, w in enumerate(POOL_WINDOWS):
        half = w // 2
        band = jnp.where(dist < w, 1.0, 0.0).astype(BF16)
        cnt = (jnp.minimum(tok1 + half, seq) - jnp.maximum(tok1 - half, 0)).astype(F32)
        sl = slice(gi * POOL_GROUP_W, (gi + 1) * POOL_GROUP_W)
        pooled = _dot(band, p_all[:, sl]) / cnt - p_self[:, sl]
        mixed.append(_dot(pooled.astype(BF16), pw_ref[gi]))
    mixed = jnp.concatenate(mixed, axis=1) * ps_ref[...]
    pool_out = _dot(mixed.astype(BF16), wp_ref[...])

    g = g_ref[...]
    merged = g[:, :D_MODEL].astype(F32) * attn_out + g[:, D_MODEL:].astype(F32) * pool_out
    gate1 = mod_ref[2:3, :]
    x1 = x_ref[...] + gate1 * _dot(merged.astype(BF16), wo_ref[...])
    x1_out[...] = x1

    shift2 = mod_ref[3:4, :]
    scale2 = mod_ref[4:5, :]
    xn = x1 * lax.rsqrt(jnp.mean(x1 * x1, axis=-1, keepdims=True) + EPS) * g2_ref[...]
    h2 = xn * (1.0 + scale2) + shift2
    h2_hi, h2_lo = _split_bf16(h2)
    h2p_out[...] = h2_hi.reshape(tm, ROW_SUB, LANES)

    logits = (_dot(h2_hi, wrh_ref[...]) + _dot(h2_lo, wrh_ref[...]) + _dot(h2_hi, wrl_ref[...])
              + br_ref[...])
    cls, w_lo, w_hi = _route(logits)
    lane = lax.broadcasted_iota(jnp.int32, (tm, LANES), 1)
    gw_out[...] = jnp.where(lane == 0, w_lo, jnp.where(lane == 1, w_hi, 0.0)).reshape(tm, 1, LANES)

    onehot = jnp.where(lane.astype(F32) == cls, 1.0, 0.0)
    row = lax.broadcasted_iota(jnp.int32, (tm, tm), 0)
    col = lax.broadcasted_iota(jnp.int32, (tm, tm), 1)
    before = jnp.where(col < row, 1.0, 0.0).astype(BF16)
    rank = _dot(before, onehot.astype(BF16)) + carry_ref[...]
    cls_out[...] = cls.astype(jnp.int32)
    rank_out[...] = jnp.sum(onehot * rank, axis=-1, keepdims=True).astype(jnp.int32)
    carry_ref[...] += jnp.sum(onehot, axis=0, keepdims=True)
    counts_out[...] = carry_ref[...]


def _post_mixer(o, p, g, x, mod, lp, *, tm):
    b, n, _ = x.shape
    single_mod = mod.shape[0] == 1
    mod_idx = (lambda bi, ri: (0, 0, 0)) if single_mod else (lambda bi, ri: (bi, 0, 0))
    tile = lambda bi, ri: (bi, ri, 0)
    const2 = lambda bi, ri: (0, 0)
    const3 = lambda bi, ri: (0, 0, 0)
    in_specs = [pl.BlockSpec((None, tm, ATTN_W), tile),
                pl.BlockSpec((None, n, POOL_W), lambda bi, ri: (bi, 0, 0)),
                pl.BlockSpec((None, tm, 2 * D_MODEL), tile),
                pl.BlockSpec((None, tm, D_MODEL), tile),
                pl.BlockSpec((None, ADA_CHUNKS, D_MODEL), mod_idx),
                pl.BlockSpec((ATTN_W, D_MODEL), const2),
                pl.BlockSpec((POOL_GROUPS, POOL_GROUP_W, POOL_GROUP_W), const3),
                pl.BlockSpec((1, POOL_W), const2),
                pl.BlockSpec((POOL_W, D_MODEL), const2),
                pl.BlockSpec((D_MODEL, D_MODEL), const2),
                pl.BlockSpec((1, D_MODEL), const2),
                pl.BlockSpec((D_MODEL, ROUTER_W), const2),
                pl.BlockSpec((D_MODEL, ROUTER_W), const2),
                pl.BlockSpec((1, ROUTER_W), const2)]
    tile4 = lambda bi, ri: (bi, ri, 0, 0)
    sub = ROW_SUB
    out_shape = [jax.ShapeDtypeStruct((b, n, D_MODEL), F32),
                 jax.ShapeDtypeStruct((b, n, sub, LANES), BF16),
                 jax.ShapeDtypeStruct((b, n, 1, LANES), F32),
                 jax.ShapeDtypeStruct((b, n, 1), jnp.int32),
                 jax.ShapeDtypeStruct((b, n, 1), jnp.int32),
                 jax.ShapeDtypeStruct((1, LANES), F32)]
    out_specs = [pl.BlockSpec((None, tm, D_MODEL), tile),
                 pl.BlockSpec((None, tm, sub, LANES), tile4),
                 pl.BlockSpec((None, tm, 1, LANES), tile4),
                 pl.BlockSpec((None, tm, 1), tile),
                 pl.BlockSpec((None, tm, 1), tile),
                 pl.BlockSpec((1, LANES), const2)]
    return pl.pallas_call(
        functools.partial(_post_kernel, seq=n, tm=tm),
        grid=(b, n // tm),
        in_specs=in_specs, out_specs=out_specs, out_shape=out_shape,
        scratch_shapes=[pltpu.VMEM((1, LANES), F32)],
        compiler_params=_cparams("arbitrary", "arbitrary"),
        name="post_mixer",
    )(o, p, g, x, mod, lp["w_br_attn"], lp["pool_w"], lp["pool_scale"], lp["w_br_pool"], lp["w_out"],
      lp["norm2_g"], lp["w_router_hi"], lp["w_router_lo"], lp["b_router"])


def _moe_tables(counts, n_tokens):
    cnt = counts[0, :N_CLASSES].astype(jnp.int32)
    tiles = (cnt + MOE_TILE - 1) // MOE_TILE
    ends = jnp.cumsum(tiles)
    off = (ends - tiles) * MOE_TILE
    n_tiles = ends[-1:]
    max_tiles = n_tokens // MOE_TILE + N_CLASSES
    tile_id = jnp.minimum(jnp.arange(max_tiles), n_tiles - 1)
    tile_cls = jnp.minimum(jnp.sum(ends[None, :] <= tile_id[:, None], axis=1), N_CLASSES - 1)
    group, pair = tile_cls // PAIRS_PER_GROUP, tile_cls % PAIRS_PER_GROUP
    e_lo = group * EXPERTS_PER_GROUP + jnp.asarray(PAIR_LO, jnp.int32)[pair]
    e_hi = group * EXPERTS_PER_GROUP + jnp.asarray(PAIR_HI, jnp.int32)[pair]
    return off.astype(jnp.int32), e_lo.astype(jnp.int32), e_hi.astype(jnp.int32), n_tiles.astype(jnp.int32)


def _moe_kernel(cls_s, rank_s, off_s, elo_s, ehi_s, nt_s, h_ref, gw_ref, wg1_ref, wu1_ref, wd1_ref,
                wg2_ref, wu2_ref, wd2_ref, o_ref, src_s, xg_ref, gwg_ref, *, n_tokens):
    j = pl.program_id(0)

    @pl.when(j == 0)
    def _():
        def clear(i, c):
            src_s[i] = 0
            return c

        lax.fori_loop(0, src_s.shape[0], clear, 0, unroll=8)

        def place(t, c):
            src_s[off_s[cls_s[t]] + rank_s[t]] = t
            return c

        lax.fori_loop(0, n_tokens, place, 0, unroll=8)

    @pl.when(j < nt_s[0])
    def _():
        base = j * MOE_TILE

        def gather(r, c):
            t = src_s[base + r]
            xg_ref[r] = h_ref[t]
            gwg_ref[r] = gw_ref[t]
            return c

        lax.fori_loop(0, MOE_TILE, gather, 0, unroll=8)
        x = xg_ref[...].reshape(MOE_TILE, D_MODEL)
        gw = gwg_ref[...].reshape(MOE_TILE, LANES)

        def hidden(wg_ref, wu_ref, gate):
            a = _dot(x, wg_ref[...])
            u = _dot(x, wu_ref[...])
            return (a * jax.nn.sigmoid(a) * u * gate).astype(BF16)

        o = (_dot(hidden(wg1_ref, wu1_ref, gw[:, 0:1]), wd1_ref[...])
             + _dot(hidden(wg2_ref, wu2_ref, gw[:, 1:2]), wd2_ref[...]))
        o_ref[...] = o.astype(BF16).reshape(o_ref.shape)

    @pl.when(j >= nt_s[0])
    def _():
        o_ref[...] = jnp.zeros_like(o_ref)


def _moe(h2p, gw, cls, rank, tables, lp):
    t, sub, _ = h2p.shape
    off, e_lo, e_hi, n_tiles = tables
    max_tiles = e_lo.shape[0]
    whole = lambda j, *_: (0, 0, 0)
    lo = lambda j, cls_s, rank_s, off_s, elo_s, ehi_s, nt_s: (elo_s[j], 0, 0)
    hi = lambda j, cls_s, rank_s, off_s, elo_s, ehi_s, nt_s: (ehi_s[j], 0, 0)
    w_in = lambda idx: pl.BlockSpec((None, D_MODEL, D_EXPERT), idx)
    w_out = lambda idx: pl.BlockSpec((None, D_EXPERT, D_MODEL), idx)
    grid_spec = pltpu.PrefetchScalarGridSpec(
        num_scalar_prefetch=6,
        grid=(max_tiles,),
        in_specs=[pl.BlockSpec((t, sub, LANES), whole, pipeline_mode=pl.Buffered(1)),
                  pl.BlockSpec((t, 1, LANES), whole, pipeline_mode=pl.Buffered(1)),
                  w_in(lo), w_in(lo), w_out(lo), w_in(hi), w_in(hi), w_out(hi)],
        out_specs=pl.BlockSpec((MOE_TILE, sub, LANES), lambda j, *_: (j, 0, 0)),
        scratch_shapes=[pltpu.SMEM((max_tiles * MOE_TILE,), jnp.int32),
                        pltpu.VMEM((MOE_TILE, sub, LANES), BF16),
                        pltpu.VMEM((MOE_TILE, 1, LANES), F32)])
    wg, wu, wd = lp["expert_w_gate"], lp["expert_w_up"], lp["expert_w_down"]
    return pl.pallas_call(
        functools.partial(_moe_kernel, n_tokens=t),
        grid_spec=grid_spec,
        out_shape=jax.ShapeDtypeStruct((max_tiles * MOE_TILE, sub, LANES), BF16),
        compiler_params=_cparams("arbitrary"),
        name="moe",
    )(cls, rank, off, e_lo, e_hi, n_tiles, h2p, gw, wg, wu, wd, wg, wu, wd)


def _combine_kernel(cls_s, rank_s, off_s, o_ref, x1_ref, mod_ref, y_ref, og_ref, *, tm):
    base = pl.program_id(0) * tm

    def gather(r, c):
        t = base + r
        og_ref[r] = o_ref[off_s[cls_s[t]] + rank_s[t]]
        return c

    lax.fori_loop(0, tm, gather, 0, unroll=8)
    moe = og_ref[...].reshape(tm, D_MODEL).astype(F32)
    y_ref[...] = x1_ref[...] + mod_ref[5:6, :] * moe


def _combine(o_sorted, x1, mod, cls, rank, off, *, seq, tm):
    t = x1.shape[0]
    rows, sub, _ = o_sorted.shape
    tiles_per_seq = seq // tm
    single_mod = mod.shape[0] == 1
    mod_idx = (lambda i, *_: (0, 0, 0)) if single_mod else (lambda i, *_: (i // tiles_per_seq, 0, 0))
    row = lambda i, *_: (i, 0)
    grid_spec = pltpu.PrefetchScalarGridSpec(
        num_scalar_prefetch=3,
        grid=(t // tm,),
        in_specs=[pl.BlockSpec((rows, sub, LANES), lambda i, *_: (0, 0, 0), pipeline_mode=pl.Buffered(1)),
                  pl.BlockSpec((tm, D_MODEL), row),
                  pl.BlockSpec((None, ADA_CHUNKS, D_MODEL), mod_idx)],
        out_specs=pl.BlockSpec((tm, D_MODEL), row),
        scratch_shapes=[pltpu.VMEM((tm, sub, LANES), BF16)])
    return pl.pallas_call(
        functools.partial(_combine_kernel, tm=tm),
        grid_spec=grid_spec,
        out_shape=jax.ShapeDtypeStruct((t, D_MODEL), F32),
        compiler_params=_cparams("arbitrary"),
        name="moe_combine",
    )(cls, rank, off, o_sorted, x1, mod)


def _rope_tables(n_tokens):
    rows = n_tokens // GRID_W
    row_ids = jnp.repeat(jnp.arange(rows, dtype=F32), GRID_W)
    col_ids = jnp.tile(jnp.arange(GRID_W, dtype=F32), rows)
    inv_freq = jnp.power(ROPE_THETA, -jnp.arange(ROPE_PAIRS, dtype=F32) / ROPE_PAIRS)
    ang_r = row_ids[:, None] * inv_freq[None, :]
    ang_c = col_ids[:, None] * inv_freq[None, :]
    ang = jnp.concatenate([ang_r, ang_r, ang_c, ang_c] * 2, axis=-1)
    return jnp.cos(ang), jnp.sin(ang)


def _layer(x, mod, lp, rope_tabs, cache, lam_rows, *, kv_dtype):
    b, n, _ = x.shape
    t = b * n
    q, k, v, p, g = _pre_mixer(x.reshape(t, D_MODEL), mod, lp, rope_tabs, seq=n, tm=256, kv_dtype=kv_dtype)
    o = _attention(q.reshape(b, n, ATTN_W), k.reshape(b, n, ATTN_W), v.reshape(b, n, ATTN_W),
                   cache, lam_rows, lp["subln_g"], tq=256)
    x1, h2p, gw, cls, rank, counts = _post_mixer(o, p.reshape(b, n, POOL_W), g.reshape(b, n, 2 * D_MODEL),
                                                 x, mod, lp, tm=256)
    tables = _moe_tables(counts, t)
    cls, rank = cls.reshape(t), rank.reshape(t)
    o_sorted = _moe(h2p.reshape(t, ROW_SUB, LANES), gw.reshape(t, 1, LANES), cls, rank, tables, lp)
    y = _combine(o_sorted, x1.reshape(t, D_MODEL), mod, cls, rank, tables[0], seq=n, tm=256)
    return y.reshape(b, n, D_MODEL), k, v


def kernel(x_prompt, x_sample, c, cache_k, cache_v, c_ctx, w_ada, b_ada, norm1_g, w_in, b_gate, q_norm_g, k_norm_g, lambda_q1, lambda_k1, lambda_q2, lambda_k2, subln_g, pool_w, pool_scale, w_br_attn, w_br_pool, w_out, norm2_g, router_group_w, router_group_b, router_expert_w, router_expert_b, expert_w_gate, expert_w_up, expert_w_down):
    b_ctx, n_ctx, _ = x_prompt.shape
    b_lat, n_lat, _ = x_sample.shape

    def router_layout(we, wg):
        pad = jnp.zeros(we.shape[:-1] + (LANES - N_EXPERTS,), F32)
        return jnp.concatenate([we, pad, jnp.repeat(wg, EXPERTS_PER_GROUP, axis=-1), pad], axis=-1)

    w_router = router_layout(router_expert_w[0], router_group_w[0])
    w_router_hi, w_router_lo = _split_bf16(w_router)
    seg = (jnp.arange(256)[:, None] // HEAD_DIM == jnp.arange(256)[None, :] // HEAD_DIM)
    lp = dict(
        norm1_g=norm1_g[0].reshape(1, D_MODEL),
        w_in=w_in[0].astype(BF16),
        b_gate=b_gate[0].reshape(1, 2 * D_MODEL),
        q_gain=jnp.tile(q_norm_g[0], ATTN_W // HEAD_DIM).reshape(1, ATTN_W),
        k_gain=jnp.tile(k_norm_g[0], ATTN_W // HEAD_DIM).reshape(1, ATTN_W),
        seg=(seg.astype(F32) / HEAD_DIM).astype(BF16),
        subln_g=subln_g[0].reshape(1, HEAD_W),
        w_br_attn=w_br_attn[0].astype(BF16),
        pool_w=pool_w[0].astype(BF16),
        pool_scale=pool_scale[0].reshape(1, POOL_W),
        w_br_pool=w_br_pool[0].astype(BF16),
        w_out=w_out[0].astype(BF16),
        norm2_g=norm2_g[0].reshape(1, D_MODEL),
        w_router_hi=w_router_hi, w_router_lo=w_router_lo,
        b_router=router_layout(router_expert_b[0], router_group_b[0]).reshape(1, ROUTER_W),
        expert_w_gate=expert_w_gate[0].astype(BF16),
        expert_w_up=expert_w_up[0].astype(BF16),
        expert_w_down=expert_w_down[0].astype(BF16),
    )
    lam_rows = jnp.stack([lambda_q1[0], lambda_k1[0], lambda_q2[0], lambda_k2[0]], axis=0)

    n_cond = 1 + b_lat
    cond = jnp.concatenate([c_ctx[None, :], c, jnp.zeros((16 - n_cond, D_MODEL), F32)], axis=0)
    mod = _adaln(cond, w_ada[0], b_ada[0])[:n_cond].reshape(n_cond, ADA_CHUNKS, D_MODEL)

    y_prompt, k_ctx, v_ctx = _layer(x_prompt, mod[:1], lp, None, None, lam_rows, kv_dtype=F32)
    cache = (cache_k, cache_v)
    y_sample, _, _ = _layer(x_sample, mod[1:], lp, _rope_tables(n_lat), cache, lam_rows, kv_dtype=BF16)

    new_cache_k = k_ctx.reshape(b_ctx, 1, n_ctx, N_HEADS, HEAD_W)
    new_cache_v = v_ctx.reshape(b_ctx, 1, n_ctx, N_HEADS, HEAD_W)
    return (y_prompt, y_sample, new_cache_k, new_cache_v)
```

```python
import functools
import math

import jax
import jax.numpy as jnp
from jax import lax
from jax.experimental import pallas as pl
from jax.experimental.pallas import tpu as pltpu

D_MODEL = 1024
GRID_W = 64
N_HEADS = 8
HEAD_DIM = 64
HEAD_W = 2 * HEAD_DIM
ATTN_W = N_HEADS * HEAD_W
POOL_GROUPS = 4
POOL_WINDOWS = (2, 4, 8, 16)
POOL_W = 512
POOL_GROUP_W = 128
IN_W = 3 * ATTN_W + POOL_W + 2 * D_MODEL
ROPE_THETA = 10000.0
ROPE_PAIRS = 16
N_GROUPS = 4
EXPERTS_PER_GROUP = 4
N_EXPERTS = 16
PAIRS_PER_GROUP = 6
N_CLASSES = N_GROUPS * PAIRS_PER_GROUP
PAIR_LO = (0, 0, 0, 1, 1, 2)
PAIR_HI = (1, 2, 3, 2, 3, 3)
D_EXPERT = 512
MOE_TILE = 128
ADA_CHUNKS = 6
EPS = 1e-6
LAMBDA_INIT = 0.8 - 0.6 * math.exp(-0.0)

LANES = 128
ROW_SUB = D_MODEL // LANES
ROUTER_W = 2 * LANES
NEG_BIG = -1e30
Q_SCALE = math.log2(math.e) * HEAD_DIM ** -0.5

F32 = jnp.float32
BF16 = jnp.bfloat16

VMEM_LIMIT = 56 * 1024 * 1024


def _cparams(*sem):
    return pltpu.CompilerParams(dimension_semantics=sem, vmem_limit_bytes=VMEM_LIMIT)


def _split_bf16(x):
    hi = x.astype(BF16)
    lo = (x - hi.astype(F32)).astype(BF16)
    return hi, lo


def _dot(a, b):
    return jnp.dot(a, b, preferred_element_type=F32)


def _adaln_kernel(cond_ref, w_ref, b_ref, o_ref):
    c = cond_ref[...]
    s = c * jax.nn.sigmoid(c)
    s_hi, s_lo = _split_bf16(s)
    w_hi, w_lo = _split_bf16(w_ref[...])
    rows = s.shape[0]
    both = _dot(jnp.concatenate([s_hi, s_lo], axis=0), w_hi)
    o_ref[...] = both[:rows] + both[rows:] + _dot(s_hi, w_lo) + b_ref[...]


def _adaln(cond, w_ada, b_ada):
    rows = cond.shape[0]
    n = w_ada.shape[1]
    tn = 1536
    return pl.pallas_call(
        _adaln_kernel,
        grid=(n // tn,),
        in_specs=[pl.BlockSpec((rows, D_MODEL), lambda j: (0, 0)),
                  pl.BlockSpec((D_MODEL, tn), lambda j: (0, j)),
                  pl.BlockSpec((1, tn), lambda j: (0, j))],
        out_specs=pl.BlockSpec((rows, tn), lambda j: (0, j)),
        out_shape=jax.ShapeDtypeStruct((rows, n), F32),
        compiler_params=_cparams("arbitrary"),
        name="adaln",
    )(cond, w_ada, b_ada.reshape(1, n))


def _chunk_rms(z, seg):
    zz = (z * z).astype(BF16)
    parts = [_dot(zz[:, c * 256:(c + 1) * 256], seg) for c in range(z.shape[1] // 256)]
    return jnp.concatenate(parts, axis=1)


def _rope(x, cos, sin):
    lane = lax.broadcasted_iota(jnp.int32, (x.shape[0], LANES), 1)
    first = (lane % (2 * ROPE_PAIRS)) < ROPE_PAIRS
    parts = []
    for c in range(x.shape[1] // LANES):
        xc = x[:, c * LANES:(c + 1) * LANES]
        up = pltpu.roll(xc, LANES - ROPE_PAIRS, 1)
        dn = pltpu.roll(xc, ROPE_PAIRS, 1)
        parts.append(xc * cos + jnp.where(first, -up, dn) * sin)
    return jnp.concatenate(parts, axis=1)


def _pre_kernel(*refs, rope, kv_dtype):
    if rope:
        (x_ref, mod_ref, g1_ref, w_ref, bg_ref, qg_ref, kg_ref, seg_ref, cos_ref, sin_ref,
         q_out, k_out, v_out, p_out, g_out) = refs
    else:
        (x_ref, mod_ref, g1_ref, w_ref, bg_ref, qg_ref, kg_ref, seg_ref,
         q_out, k_out, v_out, p_out, g_out) = refs
    x = x_ref[...]
    shift = mod_ref[0:1, :]
    scale = mod_ref[1:2, :]
    xn = x * lax.rsqrt(jnp.mean(x * x, axis=-1, keepdims=True) + EPS) * g1_ref[...]
    h = (xn * (1.0 + scale) + shift).astype(BF16)
    seg = seg_ref[...]

    zq = _dot(h, w_ref[:, 0:ATTN_W])
    qn = zq * lax.rsqrt(_chunk_rms(zq, seg) + EPS) * qg_ref[...]
    if rope:
        qn = _rope(qn, cos_ref[...], sin_ref[...])
    q_out[...] = (qn * Q_SCALE).astype(BF16)

    zk = _dot(h, w_ref[:, ATTN_W:2 * ATTN_W])
    kn = zk * lax.rsqrt(_chunk_rms(zk, seg) + EPS) * kg_ref[...]
    if rope:
        kn = _rope(kn, cos_ref[...], sin_ref[...])
    k_out[...] = kn.astype(kv_dtype)

    v_out[...] = _dot(h, w_ref[:, 2 * ATTN_W:3 * ATTN_W]).astype(kv_dtype)
    p_out[...] = _dot(h, w_ref[:, 3 * ATTN_W:3 * ATTN_W + POOL_W]).astype(BF16)
    gl = _dot(h, w_ref[:, 3 * ATTN_W + POOL_W:IN_W]) + bg_ref[...]
    g_out[...] = jax.nn.sigmoid(gl).astype(BF16)


def _pre_mixer(x, mod, lp, rope_tabs, *, seq, tm, kv_dtype):
    t = x.shape[0]
    tiles_per_seq = seq // tm
    single_mod = mod.shape[0] == 1
    mod_idx = (lambda i: (0, 0, 0)) if single_mod else (lambda i: (i // tiles_per_seq, 0, 0))
    const = lambda i: (0, 0)
    row = lambda i: (i, 0)
    in_specs = [pl.BlockSpec((tm, D_MODEL), row),
                pl.BlockSpec((None, ADA_CHUNKS, D_MODEL), mod_idx),
                pl.BlockSpec((1, D_MODEL), const),
                pl.BlockSpec((D_MODEL, IN_W), const),
                pl.BlockSpec((1, 2 * D_MODEL), const),
                pl.BlockSpec((1, ATTN_W), const),
                pl.BlockSpec((1, ATTN_W), const),
                pl.BlockSpec((256, 256), const)]
    args = [x, mod, lp["norm1_g"], lp["w_in"], lp["b_gate"], lp["q_gain"], lp["k_gain"], lp["seg"]]
    rope = rope_tabs is not None
    if rope:
        in_specs += [pl.BlockSpec((tm, LANES), lambda i: (i % tiles_per_seq, 0))] * 2
        args += list(rope_tabs)
    out_shape = [jax.ShapeDtypeStruct((t, ATTN_W), BF16),
                 jax.ShapeDtypeStruct((t, ATTN_W), kv_dtype),
                 jax.ShapeDtypeStruct((t, ATTN_W), kv_dtype),
                 jax.ShapeDtypeStruct((t, POOL_W), BF16),
                 jax.ShapeDtypeStruct((t, 2 * D_MODEL), BF16)]
    out_specs = [pl.BlockSpec((tm, ATTN_W), row), pl.BlockSpec((tm, ATTN_W), row),
                 pl.BlockSpec((tm, ATTN_W), row), pl.BlockSpec((tm, POOL_W), row),
                 pl.BlockSpec((tm, 2 * D_MODEL), row)]
    return pl.pallas_call(
        functools.partial(_pre_kernel, rope=rope, kv_dtype=kv_dtype),
        grid=(t // tm,),
        in_specs=in_specs, out_specs=out_specs, out_shape=out_shape,
        compiler_params=_cparams("arbitrary"),
        name="pre_mixer_rope" if rope else "pre_mixer",
    )(*args)


def _attn_kernel(*refs, n_q_blocks, tq, has_cache):
    if has_cache:
        q_ref, k_ref, v_ref, ck_ref, cv_ref, lam_ref, sg_ref, o_ref, k1_s, k2_s, v_s = refs
    else:
        q_ref, k_ref, v_ref, lam_ref, sg_ref, o_ref, k1_s, k2_s, v_s = refs
    lv = lam_ref[...]
    lam = (jnp.exp(jnp.sum(lv[0:1] * lv[1:2], axis=-1, keepdims=True))
           - jnp.exp(jnp.sum(lv[2:3] * lv[3:4], axis=-1, keepdims=True)) + LAMBDA_INIT)
    nt = (((1,), (1,)), ((), ()))
    sg = sg_ref[...] * (1.0 - LAMBDA_INIT)
    n = k_ref.shape[0]

    def stage(rows, kf, vf):
        kf = kf.astype(F32)
        lane = lax.broadcasted_iota(jnp.int32, kf.shape, 1)
        k1_s[rows, :] = jnp.where(lane < HEAD_DIM, kf, 0.0).astype(BF16)
        k2_s[rows, :] = jnp.where(lane >= HEAD_DIM, kf, 0.0).astype(BF16)
        v_s[rows, :] = vf.astype(BF16)

    for h in range(N_HEADS):
        cols = slice(h * HEAD_W, (h + 1) * HEAD_W)
        stage(slice(0, n), k_ref[:, cols], v_ref[:, cols])
        if has_cache:
            stage(slice(n, k1_s.shape[0]), ck_ref[:, h, :], cv_ref[:, h, :])

        def softmax_parts(q, k_s):
            s = lax.dot_general(q, k_s[...], nt, preferred_element_type=F32)
            e = jnp.exp2(s - jnp.max(s, axis=-1, keepdims=True))
            return e.astype(BF16), jnp.sum(e, axis=-1, keepdims=True)

        def block(i, carry):
            qs = pl.multiple_of(i * tq, tq)
            q = q_ref[pl.ds(qs, tq), cols]
            e1, d1 = softmax_parts(q, k1_s)
            e2, d2 = softmax_parts(q, k2_s)
            c = (lam * d1 / d2).astype(BF16)
            o = _dot(e1 - c * e2, v_s[...]) / d1
            on = o * lax.rsqrt(jnp.mean(o * o, axis=-1, keepdims=True) + EPS) * sg
            o_ref[pl.ds(qs, tq), cols] = on.astype(o_ref.dtype)
            return carry

        if n_q_blocks == 1:
            block(0, 0)
        else:
            lax.fori_loop(0, n_q_blocks, block, 0, unroll=2)


def _attention(q, k, v, cache, lam_rows, subln_g, *, tq):
    b, n, _ = q.shape
    has_cache = cache is not None
    seq = lambda bi: (bi, 0, 0)
    const = lambda bi: (0, 0)
    in_specs = [pl.BlockSpec((None, n, ATTN_W), seq)] * 3
    args = [q, k, v]
    if has_cache:
        p_len = cache[0].shape[2]
        in_specs += [pl.BlockSpec((None, None, p_len, N_HEADS, HEAD_W), lambda bi: (bi, 0, 0, 0, 0))] * 2
        args += list(cache)
    in_specs += [pl.BlockSpec((4, HEAD_DIM), const), pl.BlockSpec((1, HEAD_W), const)]
    args += [lam_rows, subln_g]
    n_keys = n + (cache[0].shape[2] if has_cache else 0)
    return pl.pallas_call(
        functools.partial(_attn_kernel, n_q_blocks=n // tq, tq=tq, has_cache=has_cache),
        grid=(b,),
        in_specs=in_specs,
        out_specs=pl.BlockSpec((None, n, ATTN_W), seq),
        out_shape=jax.ShapeDtypeStruct((b, n, ATTN_W), BF16),
        scratch_shapes=[pltpu.VMEM((n_keys, HEAD_W), BF16)] * 3,
        compiler_params=_cparams("arbitrary"),
        name="diff_attn_cache" if has_cache else "diff_attn",
    )(*args)


def _route(logits):
    rows = logits.shape[0]
    lane_i = lax.broadcasted_iota(jnp.int32, (rows, LANES), 1)
    valid = lane_i < N_EXPERTS
    lane = lane_i.astype(F32)
    grp = (lane_i // EXPERTS_PER_GROUP).astype(F32)
    e_log = logits[:, :LANES]
    g_log = jnp.where(valid, logits[:, LANES:], NEG_BIG)
    g_max = jnp.max(g_log, axis=-1, keepdims=True)
    g_den = jnp.sum(jnp.exp(g_log - g_max), axis=-1, keepdims=True) * (1.0 / EXPERTS_PER_GROUP)
    g_w = 1.0 / g_den
    g_idx = jnp.min(jnp.where(g_log == g_max, grp, float(N_GROUPS)), axis=-1, keepdims=True)
    e_sel = jnp.where(grp == g_idx, jnp.where(valid, e_log, NEG_BIG), NEG_BIG)
    v1 = jnp.max(e_sel, axis=-1, keepdims=True)
    i1 = jnp.min(jnp.where(e_sel == v1, lane, float(LANES)), axis=-1, keepdims=True)
    e_rest = jnp.where(lane == i1, NEG_BIG, e_sel)
    v2 = jnp.max(e_rest, axis=-1, keepdims=True)
    i2 = jnp.min(jnp.where(e_rest == v2, lane, float(LANES)), axis=-1, keepdims=True)
    t = jnp.exp(v2 - v1)
    w1 = g_w / (1.0 + t)
    w2 = w1 * t
    first_low = i1 < i2
    a = jnp.minimum(i1, i2) - EXPERTS_PER_GROUP * g_idx
    b = jnp.maximum(i1, i2) - EXPERTS_PER_GROUP * g_idx
    pair = a * (7.0 - a) * 0.5 + (b - a - 1.0)
    cls = g_idx * PAIRS_PER_GROUP + pair
    return cls, jnp.where(first_low, w1, w2), jnp.where(first_low, w2, w1)


def _post_kernel(o_ref, p_ref, g_ref, x_ref, mod_ref, wa_ref, pw_ref, ps_ref, wp_ref, wo_ref,
                 g2_ref, wrh_ref, wrl_ref, br_ref, x1_out, h2p_out, gw_out, cls_out, rank_out, counts_out,
                 carry_ref, *, seq, tm):
    first_step = (pl.program_id(0) == 0) & (pl.program_id(1) == 0)

    @pl.when(first_step)
    def _():
        carry_ref[...] = jnp.zeros_like(carry_ref)

    r0 = pl.multiple_of(pl.program_id(1) * tm, tm)
    attn_out = _dot(o_ref[...], wa_ref[...])

    p_all = p_ref[...]
    p_self = p_ref[pl.ds(r0, tm), :].astype(F32)
    tok = r0 + lax.broadcasted_iota(jnp.int32, (tm, seq), 0)
    col = lax.broadcasted_iota(jnp.int32, (tm, seq), 1)
    dist = jnp.abs(2 * (col - tok) + 1)
    tok1 = r0 + lax.broadcasted_iota(jnp.int32, (tm, 1), 0)
    mixed = []
    for gi, w in enumerate(POOL_WINDOWS):
        half = w // 2
        band = jnp.where(dist < w, 1.0, 0.0).astype(BF16)
        cnt = (jnp.minimum(tok1 + half, seq) - jnp.maximum(tok1 - half, 0)).astype(F32)
        sl = slice(gi * POOL_GROUP_W, (gi + 1) * POOL_GROUP_W)
        pooled = _dot(band, p_all[:, sl]) / cnt - p_self[:, sl]
        mixed.append(_dot(pooled.astype(BF16), pw_ref[gi]))
    mixed = jnp.concatenate(mixed, axis=1) * ps_ref[...]
    pool_out = _dot(mixed.astype(BF16), wp_ref[...])

    g = g_ref[...]
    merged = g[:, :D_MODEL].astype(F32) * attn_out + g[:, D_MODEL:].astype(F32) * pool_out
    gate1 = mod_ref[2:3, :]
    x1 = x_ref[...] + gate1 * _dot(merged.astype(BF16), wo_ref[...])
    x1_out[...] = x1

    shift2 = mod_ref[3:4, :]
    scale2 = mod_ref[4:5, :]
    xn = x1 * lax.rsqrt(jnp.mean(x1 * x1, axis=-1, keepdims=True) + EPS) * g2_ref[...]
    h2 = xn * (1.0 + scale2) + shift2
    h2_hi, h2_lo = _split_bf16(h2)
    h2p_out[...] = h2_hi.reshape(tm, ROW_SUB, LANES)

    logits = (_dot(h2_hi, wrh_ref[...]) + _dot(h2_lo, wrh_ref[...]) + _dot(h2_hi, wrl_ref[...])
              + br_ref[...])
    cls, w_lo, w_hi = _route(logits)
    lane = lax.broadcasted_iota(jnp.int32, (tm, LANES), 1)
    gw_out[...] = jnp.where(lane == 0, w_lo, jnp.where(lane == 1, w_hi, 0.0)).reshape(tm, 1, LANES)

    onehot = jnp.where(lane.astype(F32) == cls, 1.0, 0.0)
    row = lax.broadcasted_iota(jnp.int32, (tm, tm), 0)
    col = lax.broadcasted_iota(jnp.int32, (tm, tm), 1)
    before = jnp.where(col < row, 1.0, 0.0).astype(BF16)
    rank = _dot(before, onehot.astype(BF16)) + carry_ref[...]
    cls_out[...] = cls.astype(jnp.int32)
    rank_out[...] = jnp.sum(onehot * rank, axis=-1, keepdims=True).astype(jnp.int32)
    carry_ref[...] += jnp.sum(onehot, axis=0, keepdims=True)
    counts_out[...] = carry_ref[...]


def _post_mixer(o, p, g, x, mod, lp, *, tm):
    b, n, _ = x.shape
    single_mod = mod.shape[0] == 1
    mod_idx = (lambda bi, ri: (0, 0, 0)) if single_mod else (lambda bi, ri: (bi, 0, 0))
    tile = lambda bi, ri: (bi, ri, 0)
    const2 = lambda bi, ri: (0, 0)
    const3 = lambda bi, ri: (0, 0, 0)
    in_specs = [pl.BlockSpec((None, tm, ATTN_W), tile),
                pl.BlockSpec((None, n, POOL_W), lambda bi, ri: (bi, 0, 0)),
                pl.BlockSpec((None, tm, 2 * D_MODEL), tile),
                pl.BlockSpec((None, tm, D_MODEL), tile),
                pl.BlockSpec((None, ADA_CHUNKS, D_MODEL), mod_idx),
                pl.BlockSpec((ATTN_W, D_MODEL), const2),
                pl.BlockSpec((POOL_GROUPS, POOL_GROUP_W, POOL_GROUP_W), const3),
                pl.BlockSpec((1, POOL_W), const2),
                pl.BlockSpec((POOL_W, D_MODEL), const2),
                pl.BlockSpec((D_MODEL, D_MODEL), const2),
                pl.BlockSpec((1, D_MODEL), const2),
                pl.BlockSpec((D_MODEL, ROUTER_W), const2),
                pl.BlockSpec((D_MODEL, ROUTER_W), const2),
                pl.BlockSpec((1, ROUTER_W), const2)]
    tile4 = lambda bi, ri: (bi, ri, 0, 0)
    sub = ROW_SUB
    out_shape = [jax.ShapeDtypeStruct((b, n, D_MODEL), F32),
                 jax.ShapeDtypeStruct((b, n, sub, LANES), BF16),
                 jax.ShapeDtypeStruct((b, n, 1, LANES), F32),
                 jax.ShapeDtypeStruct((b, n, 1), jnp.int32),
                 jax.ShapeDtypeStruct((b, n, 1), jnp.int32),
                 jax.ShapeDtypeStruct((1, LANES), F32)]
    out_specs = [pl.BlockSpec((None, tm, D_MODEL), tile),
                 pl.BlockSpec((None, tm, sub, LANES), tile4),
                 pl.BlockSpec((None, tm, 1, LANES), tile4),
                 pl.BlockSpec((None, tm, 1), tile),
                 pl.BlockSpec((None, tm, 1), tile),
                 pl.BlockSpec((1, LANES), const2)]
    return pl.pallas_call(
        functools.partial(_post_kernel, seq=n, tm=tm),
        grid=(b, n // tm),
        in_specs=in_specs, out_specs=out_specs, out_shape=out_shape,
        scratch_shapes=[pltpu.VMEM((1, LANES), F32)],
        compiler_params=_cparams("arbitrary", "arbitrary"),
        name="post_mixer",
    )(o, p, g, x, mod, lp["w_br_attn"], lp["pool_w"], lp["pool_scale"], lp["w_br_pool"], lp["w_out"],
      lp["norm2_g"], lp["w_router_hi"], lp["w_router_lo"], lp["b_router"])


def _moe_tables(counts, n_tokens):
    cnt = counts[0, :N_CLASSES].astype(jnp.int32)
    tiles = (cnt + MOE_TILE - 1) // MOE_TILE
    upto = jnp.arange(N_CLASSES)[:, None] <= jnp.arange(N_CLASSES)[None, :]
    ends = jnp.sum(jnp.where(upto, tiles[:, None], 0), axis=0)
    off = jnp.concatenate([jnp.zeros((1,), jnp.int32), ends]) * MOE_TILE
    n_tiles = ends[-1:]
    max_tiles = n_tokens // MOE_TILE + N_CLASSES
    tile_id = jnp.minimum(jnp.arange(max_tiles), n_tiles - 1)
    tile_cls = jnp.minimum(jnp.sum(ends[None, :] <= tile_id[:, None], axis=1), N_CLASSES - 1)
    group, pair = tile_cls // PAIRS_PER_GROUP, tile_cls % PAIRS_PER_GROUP
    e_lo = group * EXPERTS_PER_GROUP + jnp.asarray(PAIR_LO, jnp.int32)[pair]
    e_hi = group * EXPERTS_PER_GROUP + jnp.asarray(PAIR_HI, jnp.int32)[pair]
    return off.astype(jnp.int32), e_lo.astype(jnp.int32), e_hi.astype(jnp.int32), n_tiles.astype(jnp.int32)


def _moe_kernel(cls_s, rank_s, off_s, elo_s, ehi_s, nt_s, h_ref, gw_ref, wg1_ref, wu1_ref, wd1_ref,
                wg2_ref, wu2_ref, wd2_ref, o_ref, src_s, xg_ref, gwg_ref, *, n_tokens):
    j = pl.program_id(0)
    n_tiles = nt_s[0]

    def gather_tile(tile, slot):
        base = tile * MOE_TILE
        for r in range(MOE_TILE):
            t = src_s[base + r]
            xg_ref[slot, r] = h_ref[t]
            gwg_ref[slot, r] = gw_ref[t]

    @pl.when(j == 0)
    def _():
        def clear_tail(c, carry):
            start = jnp.maximum(off_s[c + 1] - MOE_TILE, 0)
            for i in range(MOE_TILE):
                src_s[start + i] = 0
            return carry

        lax.fori_loop(0, N_CLASSES, clear_tail, 0)

        def place(t, c):
            src_s[off_s[cls_s[t]] + rank_s[t]] = t
            return c

        lax.fori_loop(0, n_tokens, place, 0, unroll=16)
        gather_tile(0, 0)

    @pl.when(j < n_tiles)
    def _():
        slot = j % 2
        gather_tile(jnp.minimum(j + 1, n_tiles - 1), 1 - slot)
        x = xg_ref[slot].reshape(MOE_TILE, D_MODEL)
        gw = gwg_ref[slot].reshape(MOE_TILE, LANES)

        def hidden(wg_ref, wu_ref, gate):
            a = _dot(x, wg_ref[...])
            u = _dot(x, wu_ref[...])
            return (a * jax.nn.sigmoid(a) * u * gate).astype(BF16)

        o = (_dot(hidden(wg1_ref, wu1_ref, gw[:, 0:1]), wd1_ref[...])
             + _dot(hidden(wg2_ref, wu2_ref, gw[:, 1:2]), wd2_ref[...]))
        o_ref[...] = o.astype(BF16).reshape(o_ref.shape)

    @pl.when(j >= n_tiles)
    def _():
        o_ref[...] = jnp.zeros_like(o_ref)


def _moe(h2p, gw, cls, rank, tables, lp):
    t, sub, _ = h2p.shape
    off, e_lo, e_hi, n_tiles = tables
    max_tiles = e_lo.shape[0]
    whole = lambda j, *_: (0, 0, 0)
    lo = lambda j, cls_s, rank_s, off_s, elo_s, ehi_s, nt_s: (elo_s[j], 0, 0)
    hi = lambda j, cls_s, rank_s, off_s, elo_s, ehi_s, nt_s: (ehi_s[j], 0, 0)
    w_in = lambda idx: pl.BlockSpec((None, D_MODEL, D_EXPERT), idx)
    w_out = lambda idx: pl.BlockSpec((None, D_EXPERT, D_MODEL), idx)
    grid_spec = pltpu.PrefetchScalarGridSpec(
        num_scalar_prefetch=6,
        grid=(max_tiles,),
        in_specs=[pl.BlockSpec((t, sub, LANES), whole, pipeline_mode=pl.Buffered(1)),
                  pl.BlockSpec((t, 1, LANES), whole, pipeline_mode=pl.Buffered(1)),
                  w_in(lo), w_in(lo), w_out(lo), w_in(hi), w_in(hi), w_out(hi)],
        out_specs=pl.BlockSpec((MOE_TILE, sub, LANES), lambda j, *_: (j, 0, 0)),
        scratch_shapes=[pltpu.SMEM((max_tiles * MOE_TILE,), jnp.int32),
                        pltpu.VMEM((2, MOE_TILE, sub, LANES), BF16),
                        pltpu.VMEM((2, MOE_TILE, 1, LANES), F32)])
    wg, wu, wd = lp["expert_w_gate"], lp["expert_w_up"], lp["expert_w_down"]
    return pl.pallas_call(
        functools.partial(_moe_kernel, n_tokens=t),
        grid_spec=grid_spec,
        out_shape=jax.ShapeDtypeStruct((max_tiles * MOE_TILE, sub, LANES), BF16),
        compiler_params=_cparams("arbitrary"),
        name="moe",
    )(cls, rank, off, e_lo, e_hi, n_tiles, h2p, gw, wg, wu, wd, wg, wu, wd)


def _combine_kernel(cls_s, rank_s, off_s, o_ref, x1_ref, mod_ref, y_ref, og_ref, *, tm):
    base = pl.program_id(0) * tm

    def gather(r, c):
        t = base + r
        og_ref[r] = o_ref[off_s[cls_s[t]] + rank_s[t]]
        return c

    lax.fori_loop(0, tm, gather, 0, unroll=8)
    moe = og_ref[...].reshape(tm, D_MODEL).astype(F32)
    y_ref[...] = x1_ref[...] + mod_ref[5:6, :] * moe


def _combine(o_sorted, x1, mod, cls, rank, off, *, seq, tm):
    t = x1.shape[0]
    rows, sub, _ = o_sorted.shape
    tiles_per_seq = seq // tm
    single_mod = mod.shape[0] == 1
    mod_idx = (lambda i, *_: (0, 0, 0)) if single_mod else (lambda i, *_: (i // tiles_per_seq, 0, 0))
    row = lambda i, *_: (i, 0)
    grid_spec = pltpu.PrefetchScalarGridSpec(
        num_scalar_prefetch=3,
        grid=(t // tm,),
        in_specs=[pl.BlockSpec((rows, sub, LANES), lambda i, *_: (0, 0, 0), pipeline_mode=pl.Buffered(1)),
                  pl.BlockSpec((tm, D_MODEL), row),
                  pl.BlockSpec((None, ADA_CHUNKS, D_MODEL), mod_idx)],
        out_specs=pl.BlockSpec((tm, D_MODEL), row),
        scratch_shapes=[pltpu.VMEM((tm, sub, LANES), BF16)])
    return pl.pallas_call(
        functools.partial(_combine_kernel, tm=tm),
        grid_spec=grid_spec,
        out_shape=jax.ShapeDtypeStruct((t, D_MODEL), F32),
        compiler_params=_cparams("arbitrary"),
        name="moe_combine",
    )(cls, rank, off, o_sorted, x1, mod)


def _rope_tables(n_tokens):
    rows = n_tokens // GRID_W
    row_ids = jnp.repeat(jnp.arange(rows, dtype=F32), GRID_W)
    col_ids = jnp.tile(jnp.arange(GRID_W, dtype=F32), rows)
    inv_freq = jnp.power(ROPE_THETA, -jnp.arange(ROPE_PAIRS, dtype=F32) / ROPE_PAIRS)
    ang_r = row_ids[:, None] * inv_freq[None, :]
    ang_c = col_ids[:, None] * inv_freq[None, :]
    ang = jnp.concatenate([ang_r, ang_r, ang_c, ang_c] * 2, axis=-1)
    return jnp.cos(ang), jnp.sin(ang)


def _layer(x, mod, lp, rope_tabs, cache, lam_rows, *, kv_dtype):
    b, n, _ = x.shape
    t = b * n
    q, k, v, p, g = _pre_mixer(x.reshape(t, D_MODEL), mod, lp, rope_tabs, seq=n, tm=256, kv_dtype=kv_dtype)
    o = _attention(q.reshape(b, n, ATTN_W), k.reshape(b, n, ATTN_W), v.reshape(b, n, ATTN_W),
                   cache, lam_rows, lp["subln_g"], tq=256)
    x1, h2p, gw, cls, rank, counts = _post_mixer(o, p.reshape(b, n, POOL_W), g.reshape(b, n, 2 * D_MODEL),
                                                 x, mod, lp, tm=256)
    tables = _moe_tables(counts, t)
    cls, rank = cls.reshape(t), rank.reshape(t)
    o_sorted = _moe(h2p.reshape(t, ROW_SUB, LANES), gw.reshape(t, 1, LANES), cls, rank, tables, lp)
    y = _combine(o_sorted, x1.reshape(t, D_MODEL), mod, cls, rank, tables[0], seq=n, tm=256)
    return y.reshape(b, n, D_MODEL), k, v


def kernel(x_prompt, x_sample, c, cache_k, cache_v, c_ctx, w_ada, b_ada, norm1_g, w_in, b_gate, q_norm_g, k_norm_g, lambda_q1, lambda_k1, lambda_q2, lambda_k2, subln_g, pool_w, pool_scale, w_br_attn, w_br_pool, w_out, norm2_g, router_group_w, router_group_b, router_expert_w, router_expert_b, expert_w_gate, expert_w_up, expert_w_down):
    b_ctx, n_ctx, _ = x_prompt.shape
    b_lat, n_lat, _ = x_sample.shape

    def router_layout(we, wg):
        pad = jnp.zeros(we.shape[:-1] + (LANES - N_EXPERTS,), F32)
        return jnp.concatenate([we, pad, jnp.repeat(wg, EXPERTS_PER_GROUP, axis=-1), pad], axis=-1)

    w_router = router_layout(router_expert_w[0], router_group_w[0])
    w_router_hi, w_router_lo = _split_bf16(w_router)
    seg = (jnp.arange(256)[:, None] // HEAD_DIM == jnp.arange(256)[None, :] // HEAD_DIM)
    lp = dict(
        norm1_g=norm1_g[0].reshape(1, D_MODEL),
        w_in=w_in[0].astype(BF16),
        b_gate=b_gate[0].reshape(1, 2 * D_MODEL),
        q_gain=jnp.tile(q_norm_g[0], ATTN_W // HEAD_DIM).reshape(1, ATTN_W),
        k_gain=jnp.tile(k_norm_g[0], ATTN_W // HEAD_DIM).reshape(1, ATTN_W),
        seg=(seg.astype(F32) / HEAD_DIM).astype(BF16),
        subln_g=subln_g[0].reshape(1, HEAD_W),
        w_br_attn=w_br_attn[0].astype(BF16),
        pool_w=pool_w[0].astype(BF16),
        pool_scale=pool_scale[0].reshape(1, POOL_W),
        w_br_pool=w_br_pool[0].astype(BF16),
        w_out=w_out[0].astype(BF16),
        norm2_g=norm2_g[0].reshape(1, D_MODEL),
        w_router_hi=w_router_hi, w_router_lo=w_router_lo,
        b_router=router_layout(router_expert_b[0], router_group_b[0]).reshape(1, ROUTER_W),
        expert_w_gate=expert_w_gate[0].astype(BF16),
        expert_w_up=expert_w_up[0].astype(BF16),
        expert_w_down=expert_w_down[0].astype(BF16),
    )
    lam_rows = jnp.stack([lambda_q1[0], lambda_k1[0], lambda_q2[0], lambda_k2[0]], axis=0)

    n_cond = 1 + b_lat
    cond = jnp.concatenate([c_ctx[None, :], c, jnp.zeros((16 - n_cond, D_MODEL), F32)], axis=0)
    mod = _adaln(cond, w_ada[0], b_ada[0])[:n_cond].reshape(n_cond, ADA_CHUNKS, D_MODEL)

    y_prompt, k_ctx, v_ctx = _layer(x_prompt, mod[:1], lp, None, None, lam_rows, kv_dtype=F32)
    cache = (cache_k, cache_v)
    y_sample, _, _ = _layer(x_sample, mod[1:], lp, _rope_tables(n_lat), cache, lam_rows, kv_dtype=BF16)

    new_cache_k = k_ctx.reshape(b_ctx, 1, n_ctx, N_HEADS, HEAD_W)
    new_cache_v = v_ctx.reshape(b_ctx, 1, n_ctx, N_HEADS, HEAD_W)
    return (y_prompt, y_sample, new_cache_k, new_cache_v)
```

```python
import functools
import math

import jax
import jax.numpy as jnp
from jax import lax
from jax.experimental import pallas as pl
from jax.experimental.pallas import tpu as pltpu

D_MODEL = 1024
GRID_W = 64
N_HEADS = 8
HEAD_DIM = 64
HEAD_W = 2 * HEAD_DIM
ATTN_W = N_HEADS * HEAD_W
POOL_GROUPS = 4
POOL_WINDOWS = (2, 4, 8, 16)
POOL_W = 512
POOL_GROUP_W = 128
IN_W = 3 * ATTN_W + POOL_W + 2 * D_MODEL
ROPE_THETA = 10000.0
ROPE_PAIRS = 16
N_GROUPS = 4
EXPERTS_PER_GROUP = 4
N_EXPERTS = 16
PAIRS_PER_GROUP = 6
N_CLASSES = N_GROUPS * PAIRS_PER_GROUP
PAIR_LO = (0, 0, 0, 1, 1, 2)
PAIR_HI = (1, 2, 3, 2, 3, 3)
D_EXPERT = 512
MOE_TILE = 128
ADA_CHUNKS = 6
EPS = 1e-6
LAMBDA_INIT = 0.8 - 0.6 * math.exp(-0.0)

LANES = 128
ROW_SUB = D_MODEL // LANES
ROUTER_W = 2 * LANES
NEG_BIG = -1e30
Q_SCALE = math.log2(math.e) * HEAD_DIM ** -0.5

F32 = jnp.float32
BF16 = jnp.bfloat16

VMEM_LIMIT = 56 * 1024 * 1024


def _cparams(*sem):
    return pltpu.CompilerParams(dimension_semantics=sem, vmem_limit_bytes=VMEM_LIMIT)


def _split_bf16(x):
    hi = x.astype(BF16)
    lo = (x - hi.astype(F32)).astype(BF16)
    return hi, lo


def _dot(a, b):
    return jnp.dot(a, b, preferred_element_type=F32)


def _adaln_kernel(cond_ref, w_ref, b_ref, o_ref):
    c = cond_ref[...]
    s = c * jax.nn.sigmoid(c)
    s_hi, s_lo = _split_bf16(s)
    w_hi, w_lo = _split_bf16(w_ref[...])
    rows = s.shape[0]
    both = _dot(jnp.concatenate([s_hi, s_lo], axis=0), w_hi)
    o_ref[...] = both[:rows] + both[rows:] + _dot(s_hi, w_lo) + b_ref[...]


def _adaln(cond, w_ada, b_ada):
    rows = cond.shape[0]
    n = w_ada.shape[1]
    tn = 1536
    return pl.pallas_call(
        _adaln_kernel,
        grid=(n // tn,),
        in_specs=[pl.BlockSpec((rows, D_MODEL), lambda j: (0, 0)),
                  pl.BlockSpec((D_MODEL, tn), lambda j: (0, j)),
                  pl.BlockSpec((1, tn), lambda j: (0, j))],
        out_specs=pl.BlockSpec((rows, tn), lambda j: (0, j)),
        out_shape=jax.ShapeDtypeStruct((rows, n), F32),
        compiler_params=_cparams("arbitrary"),
        name="adaln",
    )(cond, w_ada, b_ada.reshape(1, n))


def _chunk_rms(z, seg):
    zz = (z * z).astype(BF16)
    parts = [_dot(zz[:, c * 256:(c + 1) * 256], seg) for c in range(z.shape[1] // 256)]
    return jnp.concatenate(parts, axis=1)


def _rope(x, cos, sin):
    lane = lax.broadcasted_iota(jnp.int32, (x.shape[0], LANES), 1)
    first = (lane % (2 * ROPE_PAIRS)) < ROPE_PAIRS
    parts = []
    for c in range(x.shape[1] // LANES):
        xc = x[:, c * LANES:(c + 1) * LANES]
        up = pltpu.roll(xc, LANES - ROPE_PAIRS, 1)
        dn = pltpu.roll(xc, ROPE_PAIRS, 1)
        parts.append(xc * cos + jnp.where(first, -up, dn) * sin)
    return jnp.concatenate(parts, axis=1)


def _pre_kernel(*refs, rope, kv_dtype):
    if rope:
        (x_ref, mod_ref, g1_ref, w_ref, bg_ref, qg_ref, kg_ref, seg_ref, cos_ref, sin_ref,
         q_out, k_out, v_out, p_out, g_out) = refs
    else:
        (x_ref, mod_ref, g1_ref, w_ref, bg_ref, qg_ref, kg_ref, seg_ref,
         q_out, k_out, v_out, p_out, g_out) = refs
    x = x_ref[...]
    shift = mod_ref[0:1, :]
    scale = mod_ref[1:2, :]
    xn = x * lax.rsqrt(jnp.mean(x * x, axis=-1, keepdims=True) + EPS) * g1_ref[...]
    h = (xn * (1.0 + scale) + shift).astype(BF16)
    seg = seg_ref[...]

    zq = _dot(h, w_ref[:, 0:ATTN_W])
    qn = zq * lax.rsqrt(_chunk_rms(zq, seg) + EPS) * qg_ref[...]
    if rope:
        qn = _rope(qn, cos_ref[...], sin_ref[...])
    q_out[...] = (qn * Q_SCALE).astype(BF16)

    zk = _dot(h, w_ref[:, ATTN_W:2 * ATTN_W])
    kn = zk * lax.rsqrt(_chunk_rms(zk, seg) + EPS) * kg_ref[...]
    if rope:
        kn = _rope(kn, cos_ref[...], sin_ref[...])
    k_out[...] = kn.astype(kv_dtype)

    v_out[...] = _dot(h, w_ref[:, 2 * ATTN_W:3 * ATTN_W]).astype(kv_dtype)
    p_out[...] = _dot(h, w_ref[:, 3 * ATTN_W:3 * ATTN_W + POOL_W]).astype(BF16)
    gl = _dot(h, w_ref[:, 3 * ATTN_W + POOL_W:IN_W]) + bg_ref[...]
    g_out[...] = jax.nn.sigmoid(gl).astype(BF16)


def _pre_mixer(x, mod, lp, rope_tabs, *, seq, tm, kv_dtype):
    t = x.shape[0]
    tiles_per_seq = seq // tm
    single_mod = mod.shape[0] == 1
    mod_idx = (lambda i: (0, 0, 0)) if single_mod else (lambda i: (i // tiles_per_seq, 0, 0))
    const = lambda i: (0, 0)
    row = lambda i: (i, 0)
    in_specs = [pl.BlockSpec((tm, D_MODEL), row),
                pl.BlockSpec((None, ADA_CHUNKS, D_MODEL), mod_idx),
                pl.BlockSpec((1, D_MODEL), const),
                pl.BlockSpec((D_MODEL, IN_W), const),
                pl.BlockSpec((1, 2 * D_MODEL), const),
                pl.BlockSpec((1, ATTN_W), const),
                pl.BlockSpec((1, ATTN_W), const),
                pl.BlockSpec((256, 256), const)]
    args = [x, mod, lp["norm1_g"], lp["w_in"], lp["b_gate"], lp["q_gain"], lp["k_gain"], lp["seg"]]
    rope = rope_tabs is not None
    if rope:
        in_specs += [pl.BlockSpec((tm, LANES), lambda i: (i % tiles_per_seq, 0))] * 2
        args += list(rope_tabs)
    out_shape = [jax.ShapeDtypeStruct((t, ATTN_W), BF16),
                 jax.ShapeDtypeStruct((t, ATTN_W), kv_dtype),
                 jax.ShapeDtypeStruct((t, ATTN_W), kv_dtype),
                 jax.ShapeDtypeStruct((t, POOL_W), BF16),
                 jax.ShapeDtypeStruct((t, 2 * D_MODEL), BF16)]
    out_specs = [pl.BlockSpec((tm, ATTN_W), row), pl.BlockSpec((tm, ATTN_W), row),
                 pl.BlockSpec((tm, ATTN_W), row), pl.BlockSpec((tm, POOL_W), row),
                 pl.BlockSpec((tm, 2 * D_MODEL), row)]
    return pl.pallas_call(
        functools.partial(_pre_kernel, rope=rope, kv_dtype=kv_dtype),
        grid=(t // tm,),
        in_specs=in_specs, out_specs=out_specs, out_shape=out_shape,
        compiler_params=_cparams("arbitrary"),
        name="pre_mixer_rope" if rope else "pre_mixer",
    )(*args)


def _attn_kernel(*refs, n_q_blocks, tq, has_cache, heads_per_step):
    if has_cache:
        q_ref, k_ref, v_ref, ck_ref, cv_ref, lam_ref, sg_ref, o_ref, k1_s, k2_s, v_s = refs
    else:
        q_ref, k_ref, v_ref, lam_ref, sg_ref, o_ref, k1_s, k2_s, v_s = refs
    lv = lam_ref[...]
    lam = (jnp.exp(jnp.sum(lv[0:1] * lv[1:2], axis=-1, keepdims=True))
           - jnp.exp(jnp.sum(lv[2:3] * lv[3:4], axis=-1, keepdims=True)) + LAMBDA_INIT)
    nt = (((1,), (1,)), ((), ()))
    sg = sg_ref[...] * (1.0 - LAMBDA_INIT)
    n = k_ref.shape[0]

    def stage(hh, rows, kf, vf):
        kf = kf.astype(F32)
        lane = lax.broadcasted_iota(jnp.int32, kf.shape, 1)
        k1_s[hh, rows, :] = jnp.where(lane < HEAD_DIM, kf, 0.0).astype(BF16)
        k2_s[hh, rows, :] = jnp.where(lane >= HEAD_DIM, kf, 0.0).astype(BF16)
        v_s[hh, rows, :HEAD_W] = vf.astype(BF16)

    v_s[:, :, HEAD_W:] = jnp.ones(v_s.shape[:2] + (HEAD_W,), BF16)

    for hh in range(heads_per_step):
        cols = slice(hh * HEAD_W, (hh + 1) * HEAD_W)
        stage(hh, slice(0, n), k_ref[:, cols], v_ref[:, cols])
        if has_cache:
            head = pl.program_id(1) * heads_per_step + hh
            stage(hh, slice(n, k1_s.shape[1]), ck_ref[:, head, :], cv_ref[:, head, :])

        def softmax_av(q, k_s):
            s = lax.dot_general(q, k_s[hh], nt, preferred_element_type=F32)
            e = jnp.exp2(s - jnp.max(s, axis=-1, keepdims=True)).astype(BF16)
            ov = _dot(e, v_s[hh])
            return ov[:, :HEAD_W] / ov[:, HEAD_W:]

        def block(i, carry):
            qs = pl.multiple_of(i * tq, tq)
            q = q_ref[pl.ds(qs, tq), cols]
            o = softmax_av(q, k1_s) - lam * softmax_av(q, k2_s)
            on = o * lax.rsqrt(jnp.mean(o * o, axis=-1, keepdims=True) + EPS) * sg
            o_ref[pl.ds(qs, tq), cols] = on.astype(o_ref.dtype)
            return carry

        if n_q_blocks == 1:
            block(0, 0)
        else:
            lax.fori_loop(0, n_q_blocks, block, 0, unroll=True)


def _attention(q, k, v, cache, lam_rows, subln_g, *, tq, heads_per_step):
    b, n, _ = q.shape
    has_cache = cache is not None
    heads = lambda bi, hi: (bi, 0, hi)
    const = lambda bi, hi: (0, 0)
    in_specs = [pl.BlockSpec((None, n, heads_per_step * HEAD_W), heads)] * 3
    args = [q, k, v]
    if has_cache:
        p_len = cache[0].shape[2]
        in_specs += [pl.BlockSpec((None, None, p_len, N_HEADS, HEAD_W), lambda bi, hi: (bi, 0, 0, 0, 0))] * 2
        args += list(cache)
    in_specs += [pl.BlockSpec((4, HEAD_DIM), const), pl.BlockSpec((1, HEAD_W), const)]
    args += [lam_rows, subln_g]
    n_keys = n + (cache[0].shape[2] if has_cache else 0)
    return pl.pallas_call(
        functools.partial(_attn_kernel, n_q_blocks=n // tq, tq=tq, has_cache=has_cache,
                          heads_per_step=heads_per_step),
        grid=(b, N_HEADS // heads_per_step),
        in_specs=in_specs,
        out_specs=pl.BlockSpec((None, n, heads_per_step * HEAD_W), heads),
        out_shape=jax.ShapeDtypeStruct((b, n, ATTN_W), BF16),
        scratch_shapes=[pltpu.VMEM((heads_per_step, n_keys, HEAD_W), BF16)] * 2
        + [pltpu.VMEM((heads_per_step, n_keys, 2 * HEAD_W), BF16)],
        compiler_params=_cparams("arbitrary", "arbitrary"),
        name="diff_attn_cache" if has_cache else "diff_attn",
    )(*args)


def _route(logits):
    rows = logits.shape[0]
    lane_i = lax.broadcasted_iota(jnp.int32, (rows, LANES), 1)
    valid = lane_i < N_EXPERTS
    lane = lane_i.astype(F32)
    grp = (lane_i // EXPERTS_PER_GROUP).astype(F32)
    e_log = logits[:, :LANES]
    g_log = jnp.where(valid, logits[:, LANES:], NEG_BIG)
    g_max = jnp.max(g_log, axis=-1, keepdims=True)
    g_den = jnp.sum(jnp.exp(g_log - g_max), axis=-1, keepdims=True) * (1.0 / EXPERTS_PER_GROUP)
    g_w = 1.0 / g_den
    g_idx = jnp.min(jnp.where(g_log == g_max, grp, float(N_GROUPS)), axis=-1, keepdims=True)
    e_sel = jnp.where(grp == g_idx, jnp.where(valid, e_log, NEG_BIG), NEG_BIG)
    v1 = jnp.max(e_sel, axis=-1, keepdims=True)
    i1 = jnp.min(jnp.where(e_sel == v1, lane, float(LANES)), axis=-1, keepdims=True)
    e_rest = jnp.where(lane == i1, NEG_BIG, e_sel)
    v2 = jnp.max(e_rest, axis=-1, keepdims=True)
    i2 = jnp.min(jnp.where(e_rest == v2, lane, float(LANES)), axis=-1, keepdims=True)
    t = jnp.exp(v2 - v1)
    w1 = g_w / (1.0 + t)
    w2 = w1 * t
    first_low = i1 < i2
    a = jnp.minimum(i1, i2) - EXPERTS_PER_GROUP * g_idx
    b = jnp.maximum(i1, i2) - EXPERTS_PER_GROUP * g_idx
    pair = a * (7.0 - a) * 0.5 + (b - a - 1.0)
    cls = g_idx * PAIRS_PER_GROUP + pair
    return cls, jnp.where(first_low, w1, w2), jnp.where(first_low, w2, w1)


def _post_kernel(o_ref, p_ref, g_ref, x_ref, mod_ref, wa_ref, pw_ref, ps_ref, wp_ref, wo_ref,
                 g2_ref, wrh_ref, wrl_ref, br_ref, x1_out, h2p_out, gw_out, cls_out, rank_out, counts_out,
                 carry_ref, *, seq, tm):
    first_step = (pl.program_id(0) == 0) & (pl.program_id(1) == 0)

    @pl.when(first_step)
    def _():
        carry_ref[...] = jnp.zeros_like(carry_ref)

    r0 = pl.multiple_of(pl.program_id(1) * tm, tm)
    attn_out = _dot(o_ref[...], wa_ref[...])

    p_all = p_ref[...]
    p_self = p_ref[pl.ds(r0, tm), :].astype(F32)
    tok = r0 + lax.broadcasted_iota(jnp.int32, (tm, seq), 0)
    col = lax.broadcasted_iota(jnp.int32, (tm, seq), 1)
    dist = jnp.abs(2 * (col - tok) + 1)
    tok1 = r0 + lax.broadcasted_iota(jnp.int32, (tm, 1), 0)
    mixed = []
    for gi, w in enumerate(POOL_WINDOWS):
        half = w // 2
        band = jnp.where(dist < w, 1.0, 0.0).astype(BF16)
        cnt = (jnp.minimum(tok1 + half, seq) - jnp.maximum(tok1 - half, 0)).astype(F32)
        sl = slice(gi * POOL_GROUP_W, (gi + 1) * POOL_GROUP_W)
        pooled = _dot(band, p_all[:, sl]) / cnt - p_self[:, sl]
        mixed.append(_dot(pooled.astype(BF16), pw_ref[gi]))
    mixed = jnp.concatenate(mixed, axis=1) * ps_ref[...]
    pool_out = _dot(mixed.astype(BF16), wp_ref[...])

    g = g_ref[...]
    merged = g[:, :D_MODEL].astype(F32) * attn_out + g[:, D_MODEL:].astype(F32) * pool_out
    gate1 = mod_ref[2:3, :]
    x1 = x_ref[...] + gate1 * _dot(merged.astype(BF16), wo_ref[...])
    x1_out[...] = x1

    shift2 = mod_ref[3:4, :]
    scale2 = mod_ref[4:5, :]
    xn = x1 * lax.rsqrt(jnp.mean(x1 * x1, axis=-1, keepdims=True) + EPS) * g2_ref[...]
    h2 = xn * (1.0 + scale2) + shift2
    h2_hi, h2_lo = _split_bf16(h2)
    h2p_out[...] = h2_hi.reshape(tm, ROW_SUB, LANES)

    logits = (_dot(h2_hi, wrh_ref[...]) + _dot(h2_lo, wrh_ref[...]) + _dot(h2_hi, wrl_ref[...])
              + br_ref[...])
    cls, w_lo, w_hi = _route(logits)
    lane = lax.broadcasted_iota(jnp.int32, (tm, LANES), 1)
    gw_out[...] = jnp.where(lane == 0, w_lo, jnp.where(lane == 1, w_hi, 0.0)).reshape(tm, 1, LANES)

    onehot = jnp.where(lane.astype(F32) == cls, 1.0, 0.0)
    row = lax.broadcasted_iota(jnp.int32, (tm, tm), 0)
    col = lax.broadcasted_iota(jnp.int32, (tm, tm), 1)
    before = jnp.where(col < row, 1.0, 0.0).astype(BF16)
    rank = _dot(before, onehot.astype(BF16)) + carry_ref[...]
    cls_out[...] = cls.astype(jnp.int32)
    rank_out[...] = jnp.sum(onehot * rank, axis=-1, keepdims=True).astype(jnp.int32)
    carry_ref[...] += jnp.sum(onehot, axis=0, keepdims=True)
    counts_out[...] = carry_ref[...]


def _post_mixer(o, p, g, x, mod, lp, *, tm):
    b, n, _ = x.shape
    single_mod = mod.shape[0] == 1
    mod_idx = (lambda bi, ri: (0, 0, 0)) if single_mod else (lambda bi, ri: (bi, 0, 0))
    tile = lambda bi, ri: (bi, ri, 0)
    const2 = lambda bi, ri: (0, 0)
    const3 = lambda bi, ri: (0, 0, 0)
    in_specs = [pl.BlockSpec((None, tm, ATTN_W), tile),
                pl.BlockSpec((None, n, POOL_W), lambda bi, ri: (bi, 0, 0)),
                pl.BlockSpec((None, tm, 2 * D_MODEL), tile),
                pl.BlockSpec((None, tm, D_MODEL), tile),
                pl.BlockSpec((None, ADA_CHUNKS, D_MODEL), mod_idx),
                pl.BlockSpec((ATTN_W, D_MODEL), const2),
                pl.BlockSpec((POOL_GROUPS, POOL_GROUP_W, POOL_GROUP_W), const3),
                pl.BlockSpec((1, POOL_W), const2),
                pl.BlockSpec((POOL_W, D_MODEL), const2),
                pl.BlockSpec((D_MODEL, D_MODEL), const2),
                pl.BlockSpec((1, D_MODEL), const2),
                pl.BlockSpec((D_MODEL, ROUTER_W), const2),
                pl.BlockSpec((D_MODEL, ROUTER_W), const2),
                pl.BlockSpec((1, ROUTER_W), const2)]
    tile4 = lambda bi, ri: (bi, ri, 0, 0)
    sub = ROW_SUB
    out_shape = [jax.ShapeDtypeStruct((b, n, D_MODEL), F32),
                 jax.ShapeDtypeStruct((b, n, sub, LANES), BF16),
                 jax.ShapeDtypeStruct((b, n, 1, LANES), F32),
                 jax.ShapeDtypeStruct((b, n, 1), jnp.int32),
                 jax.ShapeDtypeStruct((b, n, 1), jnp.int32),
                 jax.ShapeDtypeStruct((1, LANES), F32)]
    out_specs = [pl.BlockSpec((None, tm, D_MODEL), tile),
                 pl.BlockSpec((None, tm, sub, LANES), tile4),
                 pl.BlockSpec((None, tm, 1, LANES), tile4),
                 pl.BlockSpec((None, tm, 1), tile),
                 pl.BlockSpec((None, tm, 1), tile),
                 pl.BlockSpec((1, LANES), const2)]
    return pl.pallas_call(
        functools.partial(_post_kernel, seq=n, tm=tm),
        grid=(b, n // tm),
        in_specs=in_specs, out_specs=out_specs, out_shape=out_shape,
        scratch_shapes=[pltpu.VMEM((1, LANES), F32)],
        compiler_params=_cparams("arbitrary", "arbitrary"),
        name="post_mixer",
    )(o, p, g, x, mod, lp["w_br_attn"], lp["pool_w"], lp["pool_scale"], lp["w_br_pool"], lp["w_out"],
      lp["norm2_g"], lp["w_router_hi"], lp["w_router_lo"], lp["b_router"])


def _moe_tables(counts, n_tokens):
    cnt = counts[0, :N_CLASSES].astype(jnp.int32)
    tiles = (cnt + MOE_TILE - 1) // MOE_TILE
    upto = jnp.arange(N_CLASSES)[:, None] <= jnp.arange(N_CLASSES)[None, :]
    ends = jnp.sum(jnp.where(upto, tiles[:, None], 0), axis=0)
    off = jnp.concatenate([jnp.zeros((1,), jnp.int32), ends]) * MOE_TILE
    n_tiles = ends[-1:]
    max_tiles = n_tokens // MOE_TILE + N_CLASSES
    tile_id = jnp.minimum(jnp.arange(max_tiles), n_tiles - 1)
    tile_cls = jnp.minimum(jnp.sum(ends[None, :] <= tile_id[:, None], axis=1), N_CLASSES - 1)
    group, pair = tile_cls // PAIRS_PER_GROUP, tile_cls % PAIRS_PER_GROUP
    e_lo = group * EXPERTS_PER_GROUP + jnp.asarray(PAIR_LO, jnp.int32)[pair]
    e_hi = group * EXPERTS_PER_GROUP + jnp.asarray(PAIR_HI, jnp.int32)[pair]
    return off.astype(jnp.int32), e_lo.astype(jnp.int32), e_hi.astype(jnp.int32), n_tiles.astype(jnp.int32)


def _moe_kernel(cls_s, rank_s, off_s, elo_s, ehi_s, nt_s, h_ref, gw_ref, wg1_ref, wu1_ref, wd1_ref,
                wg2_ref, wu2_ref, wd2_ref, o_ref, src_s, xg_ref, gwg_ref, *, n_tokens):
    j = pl.program_id(0)
    n_tiles = nt_s[0]

    def gather_tile(tile, slot):
        base = tile * MOE_TILE
        for r in range(MOE_TILE):
            t = src_s[base + r]
            xg_ref[slot, r] = h_ref[t]
            gwg_ref[slot, r] = gw_ref[t]

    @pl.when(j == 0)
    def _():
        def clear_tail(c, carry):
            start = jnp.maximum(off_s[c + 1] - MOE_TILE, 0)
            for i in range(MOE_TILE):
                src_s[start + i] = 0
            return carry

        lax.fori_loop(0, N_CLASSES, clear_tail, 0)

        def place(t, c):
            src_s[off_s[cls_s[t]] + rank_s[t]] = t
            return c

        lax.fori_loop(0, n_tokens, place, 0, unroll=16)
        gather_tile(0, 0)

    @pl.when(j < n_tiles)
    def _():
        slot = j % 2
        gather_tile(jnp.minimum(j + 1, n_tiles - 1), 1 - slot)
        x = xg_ref[slot].reshape(MOE_TILE, D_MODEL)
        gw = gwg_ref[slot].reshape(MOE_TILE, LANES)

        def hidden(wg_ref, wu_ref, gate):
            a = _dot(x, wg_ref[...])
            u = _dot(x, wu_ref[...])
            return (a * jax.nn.sigmoid(a) * u * gate).astype(BF16)

        o = (_dot(hidden(wg1_ref, wu1_ref, gw[:, 0:1]), wd1_ref[...])
             + _dot(hidden(wg2_ref, wu2_ref, gw[:, 1:2]), wd2_ref[...]))
        o_ref[...] = o.astype(BF16).reshape(o_ref.shape)

    @pl.when(j >= n_tiles)
    def _():
        o_ref[...] = jnp.zeros_like(o_ref)


def _moe(h2p, gw, cls, rank, tables, lp):
    t, sub, _ = h2p.shape
    off, e_lo, e_hi, n_tiles = tables
    max_tiles = e_lo.shape[0]
    whole = lambda j, *_: (0, 0, 0)
    lo = lambda j, cls_s, rank_s, off_s, elo_s, ehi_s, nt_s: (elo_s[j], 0, 0)
    hi = lambda j, cls_s, rank_s, off_s, elo_s, ehi_s, nt_s: (ehi_s[j], 0, 0)
    w_in = lambda idx: pl.BlockSpec((None, D_MODEL, D_EXPERT), idx)
    w_out = lambda idx: pl.BlockSpec((None, D_EXPERT, D_MODEL), idx)
    grid_spec = pltpu.PrefetchScalarGridSpec(
        num_scalar_prefetch=6,
        grid=(max_tiles,),
        in_specs=[pl.BlockSpec((t, sub, LANES), whole, pipeline_mode=pl.Buffered(1)),
                  pl.BlockSpec((t, 1, LANES), whole, pipeline_mode=pl.Buffered(1)),
                  w_in(lo), w_in(lo), w_out(lo), w_in(hi), w_in(hi), w_out(hi)],
        out_specs=pl.BlockSpec((MOE_TILE, sub, LANES), lambda j, *_: (j, 0, 0)),
        scratch_shapes=[pltpu.SMEM((max_tiles * MOE_TILE,), jnp.int32),
                        pltpu.VMEM((2, MOE_TILE, sub, LANES), BF16),
                        pltpu.VMEM((2, MOE_TILE, 1, LANES), F32)])
    wg, wu, wd = lp["expert_w_gate"], lp["expert_w_up"], lp["expert_w_down"]
    return pl.pallas_call(
        functools.partial(_moe_kernel, n_tokens=t),
        grid_spec=grid_spec,
        out_shape=jax.ShapeDtypeStruct((max_tiles * MOE_TILE, sub, LANES), BF16),
        compiler_params=_cparams("arbitrary"),
        name="moe",
    )(cls, rank, off, e_lo, e_hi, n_tiles, h2p, gw, wg, wu, wd, wg, wu, wd)


def _combine_kernel(cls_s, rank_s, off_s, o_ref, x1_ref, mod_ref, y_ref, og_ref, *, tm):
    base = pl.program_id(0) * tm

    def gather(r, c):
        t = base + r
        og_ref[r] = o_ref[off_s[cls_s[t]] + rank_s[t]]
        return c

    lax.fori_loop(0, tm, gather, 0, unroll=8)
    moe = og_ref[...].reshape(tm, D_MODEL).astype(F32)
    y_ref[...] = x1_ref[...] + mod_ref[5:6, :] * moe


def _combine(o_sorted, x1, mod, cls, rank, off, *, seq, tm):
    t = x1.shape[0]
    rows, sub, _ = o_sorted.shape
    tiles_per_seq = seq // tm
    single_mod = mod.shape[0] == 1
    mod_idx = (lambda i, *_: (0, 0, 0)) if single_mod else (lambda i, *_: (i // tiles_per_seq, 0, 0))
    row = lambda i, *_: (i, 0)
    grid_spec = pltpu.PrefetchScalarGridSpec(
        num_scalar_prefetch=3,
        grid=(t // tm,),
        in_specs=[pl.BlockSpec((rows, sub, LANES), lambda i, *_: (0, 0, 0), pipeline_mode=pl.Buffered(1)),
                  pl.BlockSpec((tm, D_MODEL), row),
                  pl.BlockSpec((None, ADA_CHUNKS, D_MODEL), mod_idx)],
        out_specs=pl.BlockSpec((tm, D_MODEL), row),
        scratch_shapes=[pltpu.VMEM((tm, sub, LANES), BF16)])
    return pl.pallas_call(
        functools.partial(_combine_kernel, tm=tm),
        grid_spec=grid_spec,
        out_shape=jax.ShapeDtypeStruct((t, D_MODEL), F32),
        compiler_params=_cparams("arbitrary"),
        name="moe_combine",
    )(cls, rank, off, o_sorted, x1, mod)


def _rope_tables(n_tokens):
    rows = n_tokens // GRID_W
    row_ids = jnp.repeat(jnp.arange(rows, dtype=F32), GRID_W)
    col_ids = jnp.tile(jnp.arange(GRID_W, dtype=F32), rows)
    inv_freq = jnp.power(ROPE_THETA, -jnp.arange(ROPE_PAIRS, dtype=F32) / ROPE_PAIRS)
    ang_r = row_ids[:, None] * inv_freq[None, :]
    ang_c = col_ids[:, None] * inv_freq[None, :]
    ang = jnp.concatenate([ang_r, ang_r, ang_c, ang_c] * 2, axis=-1)
    return jnp.cos(ang), jnp.sin(ang)


def _layer(x, mod, lp, rope_tabs, cache, lam_rows, *, kv_dtype):
    b, n, _ = x.shape
    t = b * n
    q, k, v, p, g = _pre_mixer(x.reshape(t, D_MODEL), mod, lp, rope_tabs, seq=n, tm=256, kv_dtype=kv_dtype)
    o = _attention(q.reshape(b, n, ATTN_W), k.reshape(b, n, ATTN_W), v.reshape(b, n, ATTN_W),
                   cache, lam_rows, lp["subln_g"], tq=256, heads_per_step=N_HEADS if cache is None else 2)
    x1, h2p, gw, cls, rank, counts = _post_mixer(o, p.reshape(b, n, POOL_W), g.reshape(b, n, 2 * D_MODEL),
                                                 x, mod, lp, tm=256)
    tables = _moe_tables(counts, t)
    cls, rank = cls.reshape(t), rank.reshape(t)
    o_sorted = _moe(h2p.reshape(t, ROW_SUB, LANES), gw.reshape(t, 1, LANES), cls, rank, tables, lp)
    y = _combine(o_sorted, x1.reshape(t, D_MODEL), mod, cls, rank, tables[0], seq=n, tm=256)
    return y.reshape(b, n, D_MODEL), k, v


def kernel(x_prompt, x_sample, c, cache_k, cache_v, c_ctx, w_ada, b_ada, norm1_g, w_in, b_gate, q_norm_g, k_norm_g, lambda_q1, lambda_k1, lambda_q2, lambda_k2, subln_g, pool_w, pool_scale, w_br_attn, w_br_pool, w_out, norm2_g, router_group_w, router_group_b, router_expert_w, router_expert_b, expert_w_gate, expert_w_up, expert_w_down):
    b_ctx, n_ctx, _ = x_prompt.shape
    b_lat, n_lat, _ = x_sample.shape

    def router_layout(we, wg):
        pad = jnp.zeros(we.shape[:-1] + (LANES - N_EXPERTS,), F32)
        return jnp.concatenate([we, pad, jnp.repeat(wg, EXPERTS_PER_GROUP, axis=-1), pad], axis=-1)

    w_router = router_layout(router_expert_w[0], router_group_w[0])
    w_router_hi, w_router_lo = _split_bf16(w_router)
    seg = (jnp.arange(256)[:, None] // HEAD_DIM == jnp.arange(256)[None, :] // HEAD_DIM)
    lp = dict(
        norm1_g=norm1_g[0].reshape(1, D_MODEL),
        w_in=w_in[0].astype(BF16),
        b_gate=b_gate[0].reshape(1, 2 * D_MODEL),
        q_gain=jnp.tile(q_norm_g[0], ATTN_W // HEAD_DIM).reshape(1, ATTN_W),
        k_gain=jnp.tile(k_norm_g[0], ATTN_W // HEAD_DIM).reshape(1, ATTN_W),
        seg=(seg.astype(F32) / HEAD_DIM).astype(BF16),
        subln_g=subln_g[0].reshape(1, HEAD_W),
        w_br_attn=w_br_attn[0].astype(BF16),
        pool_w=pool_w[0].astype(BF16),
        pool_scale=pool_scale[0].reshape(1, POOL_W),
        w_br_pool=w_br_pool[0].astype(BF16),
        w_out=w_out[0].astype(BF16),
        norm2_g=norm2_g[0].reshape(1, D_MODEL),
        w_router_hi=w_router_hi, w_router_lo=w_router_lo,
        b_router=router_layout(router_expert_b[0], router_group_b[0]).reshape(1, ROUTER_W),
        expert_w_gate=expert_w_gate[0].astype(BF16),
        expert_w_up=expert_w_up[0].astype(BF16),
        expert_w_down=expert_w_down[0].astype(BF16),
    )
    lam_rows = jnp.stack([lambda_q1[0], lambda_k1[0], lambda_q2[0], lambda_k2[0]], axis=0)

    n_cond = 1 + b_lat
    cond = jnp.concatenate([c_ctx[None, :], c, jnp.zeros((16 - n_cond, D_MODEL), F32)], axis=0)
    mod = _adaln(cond, w_ada[0], b_ada[0])[:n_cond].reshape(n_cond, ADA_CHUNKS, D_MODEL)

    y_prompt, k_ctx, v_ctx = _layer(x_prompt, mod[:1], lp, None, None, lam_rows, kv_dtype=F32)
    cache = (cache_k, cache_v)
    y_sample, _, _ = _layer(x_sample, mod[1:], lp, _rope_tables(n_lat), cache, lam_rows, kv_dtype=BF16)

    new_cache_k = k_ctx.reshape(b_ctx, 1, n_ctx, N_HEADS, HEAD_W)
    new_cache_v = v_ctx.reshape(b_ctx, 1, n_ctx, N_HEADS, HEAD_W)
    return (y_prompt, y_sample, new_cache_k, new_cache_v)
```

```python
import functools
import math

import jax
import jax.numpy as jnp
from jax import lax
from jax.experimental import pallas as pl
from jax.experimental.pallas import tpu as pltpu

D_MODEL = 1024
GRID_W = 64
N_HEADS = 8
HEAD_DIM = 64
HEAD_W = 2 * HEAD_DIM
ATTN_W = N_HEADS * HEAD_W
POOL_GROUPS = 4
POOL_WINDOWS = (2, 4, 8, 16)
POOL_W = 512
POOL_GROUP_W = 128
IN_W = 3 * ATTN_W + POOL_W + 2 * D_MODEL
ROPE_THETA = 10000.0
ROPE_PAIRS = 16
N_GROUPS = 4
EXPERTS_PER_GROUP = 4
N_EXPERTS = 16
PAIRS_PER_GROUP = 6
N_CLASSES = N_GROUPS * PAIRS_PER_GROUP
PAIR_LO = (0, 0, 0, 1, 1, 2)
PAIR_HI = (1, 2, 3, 2, 3, 3)
D_EXPERT = 512
MOE_TILE = 128
POST_ROWS = 256
POOL_HALO = 16
ADA_CHUNKS = 6
EPS = 1e-6
LAMBDA_INIT = 0.8 - 0.6 * math.exp(-0.0)

LANES = 128
ROW_SUB = D_MODEL // LANES
ROUTER_W = 2 * LANES
NEG_BIG = -1e30
Q_SCALE = math.log2(math.e) * HEAD_DIM ** -0.5

F32 = jnp.float32
BF16 = jnp.bfloat16

VMEM_LIMIT = 56 * 1024 * 1024


def _cparams(*sem):
    return pltpu.CompilerParams(dimension_semantics=sem, vmem_limit_bytes=VMEM_LIMIT)


def _split_bf16(x):
    hi = x.astype(BF16)
    lo = (x - hi.astype(F32)).astype(BF16)
    return hi, lo


def _dot(a, b):
    return jnp.dot(a, b, preferred_element_type=F32)


def _adaln_kernel(cond_ref, w_ref, b_ref, o_ref):
    c = cond_ref[...]
    s = c * jax.nn.sigmoid(c)
    s_hi, s_lo = _split_bf16(s)
    w_hi, w_lo = _split_bf16(w_ref[...])
    rows = s.shape[0]
    both = _dot(jnp.concatenate([s_hi, s_lo], axis=0), w_hi)
    o_ref[...] = both[:rows] + both[rows:] + _dot(s_hi, w_lo) + b_ref[...]


def _adaln(cond, w_ada, b_ada):
    rows = cond.shape[0]
    n = w_ada.shape[1]
    tn = 1536
    return pl.pallas_call(
        _adaln_kernel,
        grid=(n // tn,),
        in_specs=[pl.BlockSpec((rows, D_MODEL), lambda j: (0, 0)),
                  pl.BlockSpec((D_MODEL, tn), lambda j: (0, j)),
                  pl.BlockSpec((1, tn), lambda j: (0, j))],
        out_specs=pl.BlockSpec((rows, tn), lambda j: (0, j)),
        out_shape=jax.ShapeDtypeStruct((rows, n), F32),
        compiler_params=_cparams("arbitrary"),
        name="adaln",
    )(cond, w_ada, b_ada.reshape(1, n))


def _chunk_rms(z, seg):
    zz = (z * z).astype(BF16)
    parts = [_dot(zz[:, c * 256:(c + 1) * 256], seg) for c in range(z.shape[1] // 256)]
    return jnp.concatenate(parts, axis=1)


def _rope(x, cos, sin):
    lane = lax.broadcasted_iota(jnp.int32, (x.shape[0], LANES), 1)
    first = (lane % (2 * ROPE_PAIRS)) < ROPE_PAIRS
    parts = []
    for c in range(x.shape[1] // LANES):
        xc = x[:, c * LANES:(c + 1) * LANES]
        up = pltpu.roll(xc, LANES - ROPE_PAIRS, 1)
        dn = pltpu.roll(xc, ROPE_PAIRS, 1)
        parts.append(xc * cos + jnp.where(first, -up, dn) * sin)
    return jnp.concatenate(parts, axis=1)


def _pre_kernel(*refs, rope, kv_dtype):
    if rope:
        (x_ref, mod_ref, g1_ref, w_ref, bg_ref, qg_ref, kg_ref, seg_ref, cos_ref, sin_ref,
         q_out, k_out, v_out, p_out, g_out) = refs
    else:
        (x_ref, mod_ref, g1_ref, w_ref, bg_ref, qg_ref, kg_ref, seg_ref,
         q_out, k_out, v_out, p_out, g_out) = refs
    x = x_ref[...]
    shift = mod_ref[0:1, :]
    scale = mod_ref[1:2, :]
    xn = x * lax.rsqrt(jnp.mean(x * x, axis=-1, keepdims=True) + EPS) * g1_ref[...]
    h = (xn * (1.0 + scale) + shift).astype(BF16)
    seg = seg_ref[...]

    zq = _dot(h, w_ref[:, 0:ATTN_W])
    qn = zq * lax.rsqrt(_chunk_rms(zq, seg) + EPS) * qg_ref[...]
    if rope:
        qn = _rope(qn, cos_ref[...], sin_ref[...])
    q_out[...] = (qn * Q_SCALE).astype(BF16)

    zk = _dot(h, w_ref[:, ATTN_W:2 * ATTN_W])
    kn = zk * lax.rsqrt(_chunk_rms(zk, seg) + EPS) * kg_ref[...]
    if rope:
        kn = _rope(kn, cos_ref[...], sin_ref[...])
    k_out[...] = kn.astype(kv_dtype)

    v_out[...] = _dot(h, w_ref[:, 2 * ATTN_W:3 * ATTN_W]).astype(kv_dtype)
    p_out[...] = _dot(h, w_ref[:, 3 * ATTN_W:3 * ATTN_W + POOL_W]).astype(BF16)
    gl = _dot(h, w_ref[:, 3 * ATTN_W + POOL_W:IN_W]) + bg_ref[...]
    g_out[...] = jax.nn.sigmoid(gl).astype(BF16)


def _pre_mixer(x, mod, lp, rope_tabs, *, seq, tm, kv_dtype):
    t = x.shape[0]
    tiles_per_seq = seq // tm
    single_mod = mod.shape[0] == 1
    mod_idx = (lambda i: (0, 0, 0)) if single_mod else (lambda i: (i // tiles_per_seq, 0, 0))
    const = lambda i: (0, 0)
    row = lambda i: (i, 0)
    in_specs = [pl.BlockSpec((tm, D_MODEL), row),
                pl.BlockSpec((None, ADA_CHUNKS, D_MODEL), mod_idx),
                pl.BlockSpec((1, D_MODEL), const),
                pl.BlockSpec((D_MODEL, IN_W), const),
                pl.BlockSpec((1, 2 * D_MODEL), const),
                pl.BlockSpec((1, ATTN_W), const),
                pl.BlockSpec((1, ATTN_W), const),
                pl.BlockSpec((256, 256), const)]
    args = [x, mod, lp["norm1_g"], lp["w_in"], lp["b_gate"], lp["q_gain"], lp["k_gain"], lp["seg"]]
    rope = rope_tabs is not None
    if rope:
        in_specs += [pl.BlockSpec((tm, LANES), lambda i: (i % tiles_per_seq, 0))] * 2
        args += list(rope_tabs)
    out_shape = [jax.ShapeDtypeStruct((t, ATTN_W), BF16),
                 jax.ShapeDtypeStruct((t, ATTN_W), kv_dtype),
                 jax.ShapeDtypeStruct((t, ATTN_W), kv_dtype),
                 jax.ShapeDtypeStruct((t, POOL_W), BF16),
                 jax.ShapeDtypeStruct((t, 2 * D_MODEL), BF16)]
    out_specs = [pl.BlockSpec((tm, ATTN_W), row), pl.BlockSpec((tm, ATTN_W), row),
                 pl.BlockSpec((tm, ATTN_W), row), pl.BlockSpec((tm, POOL_W), row),
                 pl.BlockSpec((tm, 2 * D_MODEL), row)]
    return pl.pallas_call(
        functools.partial(_pre_kernel, rope=rope, kv_dtype=kv_dtype),
        grid=(t // tm,),
        in_specs=in_specs, out_specs=out_specs, out_shape=out_shape,
        compiler_params=_cparams("arbitrary"),
        name="pre_mixer_rope" if rope else "pre_mixer",
    )(*args)


def _attn_kernel(*refs, n_q_blocks, tq, has_cache, heads_per_step):
    if has_cache:
        q_ref, k_ref, v_ref, ck_ref, cv_ref, lam_ref, sg_ref, o_ref, k1_s, k2_s, v_s = refs
    else:
        q_ref, k_ref, v_ref, lam_ref, sg_ref, o_ref, k1_s, k2_s, v_s = refs
    lv = lam_ref[...]
    lam = (jnp.exp(jnp.sum(lv[0:1] * lv[1:2], axis=-1, keepdims=True))
           - jnp.exp(jnp.sum(lv[2:3] * lv[3:4], axis=-1, keepdims=True)) + LAMBDA_INIT)
    nt = (((1,), (1,)), ((), ()))
    sg = sg_ref[...] * (1.0 - LAMBDA_INIT)
    n = k_ref.shape[0]

    def stage(hh, rows, kf, vf):
        kf = kf.astype(F32)
        lane = lax.broadcasted_iota(jnp.int32, kf.shape, 1)
        k1_s[hh, rows, :] = jnp.where(lane < HEAD_DIM, kf, 0.0).astype(BF16)
        k2_s[hh, rows, :] = jnp.where(lane >= HEAD_DIM, kf, 0.0).astype(BF16)
        v_s[hh, rows, :HEAD_W] = vf.astype(BF16)

    v_s[:, :, HEAD_W:] = jnp.ones(v_s.shape[:2] + (HEAD_W,), BF16)

    for hh in range(heads_per_step):
        cols = slice(hh * HEAD_W, (hh + 1) * HEAD_W)
        stage(hh, slice(0, n), k_ref[:, cols], v_ref[:, cols])
        if has_cache:
            head = pl.program_id(1) * heads_per_step + hh
            stage(hh, slice(n, k1_s.shape[1]), ck_ref[:, head, :], cv_ref[:, head, :])

        def softmax_av(q, k_s):
            s = lax.dot_general(q, k_s[hh], nt, preferred_element_type=F32)
            e = jnp.exp2(s - jnp.max(s, axis=-1, keepdims=True)).astype(BF16)
            ov = _dot(e, v_s[hh])
            return ov[:, :HEAD_W] / ov[:, HEAD_W:]

        def block(i, carry):
            qs = pl.multiple_of(i * tq, tq)
            q = q_ref[pl.ds(qs, tq), cols]
            o = softmax_av(q, k1_s) - lam * softmax_av(q, k2_s)
            on = o * lax.rsqrt(jnp.mean(o * o, axis=-1, keepdims=True) + EPS) * sg
            o_ref[pl.ds(qs, tq), cols] = on.astype(o_ref.dtype)
            return carry

        if n_q_blocks == 1:
            block(0, 0)
        else:
            lax.fori_loop(0, n_q_blocks, block, 0, unroll=True)


def _attention(q, k, v, cache, lam_rows, subln_g, *, tq, heads_per_step):
    b, n, _ = q.shape
    has_cache = cache is not None
    heads = lambda bi, hi: (bi, 0, hi)
    const = lambda bi, hi: (0, 0)
    in_specs = [pl.BlockSpec((None, n, heads_per_step * HEAD_W), heads)] * 3
    args = [q, k, v]
    if has_cache:
        p_len = cache[0].shape[2]
        in_specs += [pl.BlockSpec((None, None, p_len, N_HEADS, HEAD_W), lambda bi, hi: (bi, 0, 0, 0, 0))] * 2
        args += list(cache)
    in_specs += [pl.BlockSpec((4, HEAD_DIM), const), pl.BlockSpec((1, HEAD_W), const)]
    args += [lam_rows, subln_g]
    n_keys = n + (cache[0].shape[2] if has_cache else 0)
    return pl.pallas_call(
        functools.partial(_attn_kernel, n_q_blocks=n // tq, tq=tq, has_cache=has_cache,
                          heads_per_step=heads_per_step),
        grid=(b, N_HEADS // heads_per_step),
        in_specs=in_specs,
        out_specs=pl.BlockSpec((None, n, heads_per_step * HEAD_W), heads),
        out_shape=jax.ShapeDtypeStruct((b, n, ATTN_W), BF16),
        scratch_shapes=[pltpu.VMEM((heads_per_step, n_keys, HEAD_W), BF16)] * 2
        + [pltpu.VMEM((heads_per_step, n_keys, 2 * HEAD_W), BF16)],
        compiler_params=_cparams("arbitrary", "arbitrary"),
        name="diff_attn_cache" if has_cache else "diff_attn",
    )(*args)


def _route(logits):
    rows = logits.shape[0]
    lane_i = lax.broadcasted_iota(jnp.int32, (rows, LANES), 1)
    valid = lane_i < N_EXPERTS
    lane = lane_i.astype(F32)
    grp = (lane_i // EXPERTS_PER_GROUP).astype(F32)
    e_log = logits[:, :LANES]
    g_log = jnp.where(valid, logits[:, LANES:], NEG_BIG)
    g_max = jnp.max(g_log, axis=-1, keepdims=True)
    g_den = jnp.sum(jnp.exp(g_log - g_max), axis=-1, keepdims=True) * (1.0 / EXPERTS_PER_GROUP)
    g_w = 1.0 / g_den
    g_idx = jnp.min(jnp.where(g_log == g_max, grp, float(N_GROUPS)), axis=-1, keepdims=True)
    e_sel = jnp.where(grp == g_idx, jnp.where(valid, e_log, NEG_BIG), NEG_BIG)
    v1 = jnp.max(e_sel, axis=-1, keepdims=True)
    i1 = jnp.min(jnp.where(e_sel == v1, lane, float(LANES)), axis=-1, keepdims=True)
    e_rest = jnp.where(lane == i1, NEG_BIG, e_sel)
    v2 = jnp.max(e_rest, axis=-1, keepdims=True)
    i2 = jnp.min(jnp.where(e_rest == v2, lane, float(LANES)), axis=-1, keepdims=True)
    t = jnp.exp(v2 - v1)
    w1 = g_w / (1.0 + t)
    w2 = w1 * t
    first_low = i1 < i2
    a = jnp.minimum(i1, i2) - EXPERTS_PER_GROUP * g_idx
    b = jnp.maximum(i1, i2) - EXPERTS_PER_GROUP * g_idx
    pair = a * (7.0 - a) * 0.5 + (b - a - 1.0)
    cls = g_idx * PAIRS_PER_GROUP + pair
    return cls, jnp.where(first_low, w1, w2), jnp.where(first_low, w2, w1)


def _post_kernel(o_ref, p_ref, g_ref, x_ref, mod_ref, wa_ref, pw_ref, ps_ref, wp_ref, wo_ref,
                 g2_ref, wrh_ref, wrl_ref, br_ref, x1_out, h2p_out, gw_out, cls_out, rank_out, counts_out,
                 carry_ref, *, seq, n_sub):
    step = pl.program_id(0)

    @pl.when(step == 0)
    def _():
        carry_ref[...] = jnp.zeros_like(carry_ref)

    tm = POST_ROWS
    tiles_per_seq = max(seq // (n_sub * tm), 1)
    ext = tm + 2 * POOL_HALO
    for s in range(n_sub):
        rows = slice(s * tm, (s + 1) * tm)
        if seq == tm:
            blk0, t0 = s * tm, 0
        else:
            blk0 = t0 = pl.multiple_of(((step % tiles_per_seq) * n_sub + s) * tm, tm)
        attn_out = _dot(o_ref[rows, :], wa_ref[...])

        p_mid = p_ref[pl.ds(blk0, tm), :].astype(F32)
        if seq == tm:
            halo_top = halo_bot = jnp.zeros((POOL_HALO, POOL_W), F32)
        else:
            top0 = pl.multiple_of(jnp.maximum(blk0 - POOL_HALO, 0), POOL_HALO)
            bot0 = pl.multiple_of(jnp.minimum(blk0 + tm, seq - POOL_HALO), POOL_HALO)
            halo_top = p_ref[pl.ds(top0, POOL_HALO), :].astype(F32) * jnp.where(t0 > 0, 1.0, 0.0)
            halo_bot = p_ref[pl.ds(bot0, POOL_HALO), :].astype(F32) * jnp.where(t0 + tm < seq, 1.0, 0.0)
        p_ext = jnp.concatenate([halo_top, p_mid, halo_bot], axis=0)
        tok1 = t0 + lax.broadcasted_iota(jnp.int32, (tm, 1), 0)
        mixed = []
        for gi, w in enumerate(POOL_WINDOWS):
            half = w // 2
            sl = slice(gi * POOL_GROUP_W, (gi + 1) * POOL_GROUP_W)
            run = p_ext[:, sl]
            k = 1
            while k < w:
                run = run + pltpu.roll(run, ext - k, 0)
                k *= 2
            win = pltpu.roll(run, ext - (POOL_HALO - half), 0)[:tm]
            cnt = (jnp.minimum(tok1 + half, seq) - jnp.maximum(tok1 - half, 0)).astype(F32)
            pooled = win / cnt - p_mid[:, sl]
            mixed.append(_dot(pooled.astype(BF16), pw_ref[gi]))
        mixed = jnp.concatenate(mixed, axis=1) * ps_ref[...]
        pool_out = _dot(mixed.astype(BF16), wp_ref[...])

        g = g_ref[rows, :]
        merged = g[:, :D_MODEL].astype(F32) * attn_out + g[:, D_MODEL:].astype(F32) * pool_out
        gate1 = mod_ref[2:3, :]
        x1 = x_ref[rows, :] + gate1 * _dot(merged.astype(BF16), wo_ref[...])
        x1_out[rows, :] = x1

        shift2 = mod_ref[3:4, :]
        scale2 = mod_ref[4:5, :]
        xn = x1 * lax.rsqrt(jnp.mean(x1 * x1, axis=-1, keepdims=True) + EPS) * g2_ref[...]
        h2 = xn * (1.0 + scale2) + shift2
        h2_hi, h2_lo = _split_bf16(h2)
        h2p_out[rows] = h2_hi.reshape(tm, ROW_SUB, LANES)

        logits = (_dot(h2_hi, wrh_ref[...]) + _dot(h2_lo, wrh_ref[...]) + _dot(h2_hi, wrl_ref[...])
                  + br_ref[...])
        cls, w_lo, w_hi = _route(logits)
        lane = lax.broadcasted_iota(jnp.int32, (tm, LANES), 1)
        gw_out[rows] = jnp.where(lane == 0, w_lo, jnp.where(lane == 1, w_hi, 0.0)).reshape(tm, 1, LANES)

        onehot = jnp.where(lane.astype(F32) == cls, 1.0, 0.0)
        row = lax.broadcasted_iota(jnp.int32, (tm, tm), 0)
        col = lax.broadcasted_iota(jnp.int32, (tm, tm), 1)
        before = jnp.where(col < row, 1.0, 0.0).astype(BF16)
        rank = _dot(before, onehot.astype(BF16)) + carry_ref[...]
        cls_out[rows, :] = cls.astype(jnp.int32)
        rank_out[rows, :] = jnp.sum(onehot * rank, axis=-1, keepdims=True).astype(jnp.int32)
        carry_ref[...] += jnp.sum(onehot, axis=0, keepdims=True)
    counts_out[...] = carry_ref[...]


def _post_mixer(o, p, g, x, mod, lp, *, seq, n_sub):
    t = x.shape[0]
    tm = n_sub * POST_ROWS
    tiles_per_seq = max(seq // tm, 1)
    single_mod = mod.shape[0] == 1
    mod_idx = (lambda i: (0, 0, 0)) if single_mod else (lambda i: (i // tiles_per_seq, 0, 0))
    tile = lambda i: (i, 0)
    const2 = lambda i: (0, 0)
    const3 = lambda i: (0, 0, 0)
    p_spec = (pl.BlockSpec((tm, POOL_W), tile) if seq == POST_ROWS
              else pl.BlockSpec((seq, POOL_W), lambda i: (i // tiles_per_seq, 0)))
    in_specs = [pl.BlockSpec((tm, ATTN_W), tile),
                p_spec,
                pl.BlockSpec((tm, 2 * D_MODEL), tile),
                pl.BlockSpec((tm, D_MODEL), tile),
                pl.BlockSpec((None, ADA_CHUNKS, D_MODEL), mod_idx),
                pl.BlockSpec((ATTN_W, D_MODEL), const2),
                pl.BlockSpec((POOL_GROUPS, POOL_GROUP_W, POOL_GROUP_W), const3),
                pl.BlockSpec((1, POOL_W), const2),
                pl.BlockSpec((POOL_W, D_MODEL), const2),
                pl.BlockSpec((D_MODEL, D_MODEL), const2),
                pl.BlockSpec((1, D_MODEL), const2),
                pl.BlockSpec((D_MODEL, ROUTER_W), const2),
                pl.BlockSpec((D_MODEL, ROUTER_W), const2),
                pl.BlockSpec((1, ROUTER_W), const2)]
    tile3 = lambda i: (i, 0, 0)
    out_shape = [jax.ShapeDtypeStruct((t, D_MODEL), F32),
                 jax.ShapeDtypeStruct((t, ROW_SUB, LANES), BF16),
                 jax.ShapeDtypeStruct((t, 1, LANES), F32),
                 jax.ShapeDtypeStruct((t, 1), jnp.int32),
                 jax.ShapeDtypeStruct((t, 1), jnp.int32),
                 jax.ShapeDtypeStruct((1, LANES), F32)]
    out_specs = [pl.BlockSpec((tm, D_MODEL), tile),
                 pl.BlockSpec((tm, ROW_SUB, LANES), tile3),
                 pl.BlockSpec((tm, 1, LANES), tile3),
                 pl.BlockSpec((tm, 1), tile),
                 pl.BlockSpec((tm, 1), tile),
                 pl.BlockSpec((1, LANES), const2)]
    return pl.pallas_call(
        functools.partial(_post_kernel, seq=seq, n_sub=n_sub),
        grid=(t // tm,),
        in_specs=in_specs, out_specs=out_specs, out_shape=out_shape,
        scratch_shapes=[pltpu.VMEM((1, LANES), F32)],
        compiler_params=_cparams("arbitrary"),
        name="post_mixer",
    )(o, p, g, x, mod, lp["w_br_attn"], lp["pool_w"], lp["pool_scale"], lp["w_br_pool"], lp["w_out"],
      lp["norm2_g"], lp["w_router_hi"], lp["w_router_lo"], lp["b_router"])


def _moe_tables(counts, n_tokens):
    cnt = counts[0, :N_CLASSES].astype(jnp.int32)
    tiles = (cnt + MOE_TILE - 1) // MOE_TILE
    upto = jnp.arange(N_CLASSES)[:, None] <= jnp.arange(N_CLASSES)[None, :]
    ends = jnp.sum(jnp.where(upto, tiles[:, None], 0), axis=0)
    off = jnp.concatenate([jnp.zeros((1,), jnp.int32), ends]) * MOE_TILE
    n_tiles = ends[-1:]
    max_tiles = n_tokens // MOE_TILE + N_CLASSES
    tile_id = jnp.minimum(jnp.arange(max_tiles), n_tiles - 1)
    tile_cls = jnp.minimum(jnp.sum(ends[None, :] <= tile_id[:, None], axis=1), N_CLASSES - 1)
    group, pair = tile_cls // PAIRS_PER_GROUP, tile_cls % PAIRS_PER_GROUP
    e_lo = group * EXPERTS_PER_GROUP + jnp.asarray(PAIR_LO, jnp.int32)[pair]
    e_hi = group * EXPERTS_PER_GROUP + jnp.asarray(PAIR_HI, jnp.int32)[pair]
    return off.astype(jnp.int32), e_lo.astype(jnp.int32), e_hi.astype(jnp.int32), n_tiles.astype(jnp.int32)


def _moe_kernel(cls_s, rank_s, off_s, elo_s, ehi_s, nt_s, h_ref, gw_ref, wg1_ref, wu1_ref, wd1_ref,
                wg2_ref, wu2_ref, wd2_ref, o_ref, src_s, xg_ref, gwg_ref, *, n_tokens):
    j = pl.program_id(0)
    n_tiles = nt_s[0]

    def gather_tile(tile, slot):
        base = tile * MOE_TILE
        for r in range(MOE_TILE):
            t = src_s[base + r]
            xg_ref[slot, r] = h_ref[t]
            gwg_ref[slot, r] = gw_ref[t]

    @pl.when(j == 0)
    def _():
        def clear_tail(c, carry):
            start = jnp.maximum(off_s[c + 1] - MOE_TILE, 0)
            for i in range(MOE_TILE):
                src_s[start + i] = 0
            return carry

        lax.fori_loop(0, N_CLASSES, clear_tail, 0)

        def place(t, c):
            src_s[off_s[cls_s[t]] + rank_s[t]] = t
            return c

        lax.fori_loop(0, n_tokens, place, 0, unroll=16)
        gather_tile(0, 0)

    @pl.when(j < n_tiles)
    def _():
        slot = j % 2
        gather_tile(jnp.minimum(j + 1, n_tiles - 1), 1 - slot)
        x = xg_ref[slot].reshape(MOE_TILE, D_MODEL)
        gw = gwg_ref[slot].reshape(MOE_TILE, LANES)

        def hidden(wg_ref, wu_ref, gate):
            a = _dot(x, wg_ref[...])
            u = _dot(x, wu_ref[...])
            return (a * jax.nn.sigmoid(a) * u * gate).astype(BF16)

        o = (_dot(hidden(wg1_ref, wu1_ref, gw[:, 0:1]), wd1_ref[...])
             + _dot(hidden(wg2_ref, wu2_ref, gw[:, 1:2]), wd2_ref[...]))
        o_ref[...] = o.astype(BF16).reshape(o_ref.shape)

    @pl.when(j >= n_tiles)
    def _():
        o_ref[...] = jnp.zeros_like(o_ref)


def _moe(h2p, gw, cls, rank, tables, lp):
    t, sub, _ = h2p.shape
    off, e_lo, e_hi, n_tiles = tables
    max_tiles = e_lo.shape[0]
    whole = lambda j, *_: (0, 0, 0)
    lo = lambda j, cls_s, rank_s, off_s, elo_s, ehi_s, nt_s: (elo_s[j], 0, 0)
    hi = lambda j, cls_s, rank_s, off_s, elo_s, ehi_s, nt_s: (ehi_s[j], 0, 0)
    w_in = lambda idx: pl.BlockSpec((None, D_MODEL, D_EXPERT), idx)
    w_out = lambda idx: pl.BlockSpec((None, D_EXPERT, D_MODEL), idx)
    grid_spec = pltpu.PrefetchScalarGridSpec(
        num_scalar_prefetch=6,
        grid=(max_tiles,),
        in_specs=[pl.BlockSpec((t, sub, LANES), whole, pipeline_mode=pl.Buffered(1)),
                  pl.BlockSpec((t, 1, LANES), whole, pipeline_mode=pl.Buffered(1)),
                  w_in(lo), w_in(lo), w_out(lo), w_in(hi), w_in(hi), w_out(hi)],
        out_specs=pl.BlockSpec((MOE_TILE, sub, LANES), lambda j, *_: (j, 0, 0)),
        scratch_shapes=[pltpu.SMEM((max_tiles * MOE_TILE,), jnp.int32),
                        pltpu.VMEM((2, MOE_TILE, sub, LANES), BF16),
                        pltpu.VMEM((2, MOE_TILE, 1, LANES), F32)])
    wg, wu, wd = lp["expert_w_gate"], lp["expert_w_up"], lp["expert_w_down"]
    return pl.pallas_call(
        functools.partial(_moe_kernel, n_tokens=t),
        grid_spec=grid_spec,
        out_shape=jax.ShapeDtypeStruct((max_tiles * MOE_TILE, sub, LANES), BF16),
        compiler_params=_cparams("arbitrary"),
        name="moe",
    )(cls, rank, off, e_lo, e_hi, n_tiles, h2p, gw, wg, wu, wd, wg, wu, wd)


def _combine_kernel(cls_s, rank_s, off_s, o_ref, x1_ref, mod_ref, y_ref, og_ref, *, tm):
    base = pl.program_id(0) * tm

    def gather(r, c):
        t = base + r
        og_ref[r] = o_ref[off_s[cls_s[t]] + rank_s[t]]
        return c

    lax.fori_loop(0, tm, gather, 0, unroll=8)
    moe = og_ref[...].reshape(tm, D_MODEL).astype(F32)
    y_ref[...] = x1_ref[...] + mod_ref[5:6, :] * moe


def _combine(o_sorted, x1, mod, cls, rank, off, *, seq, tm):
    t = x1.shape[0]
    rows, sub, _ = o_sorted.shape
    tiles_per_seq = seq // tm
    single_mod = mod.shape[0] == 1
    mod_idx = (lambda i, *_: (0, 0, 0)) if single_mod else (lambda i, *_: (i // tiles_per_seq, 0, 0))
    row = lambda i, *_: (i, 0)
    grid_spec = pltpu.PrefetchScalarGridSpec(
        num_scalar_prefetch=3,
        grid=(t // tm,),
        in_specs=[pl.BlockSpec((rows, sub, LANES), lambda i, *_: (0, 0, 0), pipeline_mode=pl.Buffered(1)),
                  pl.BlockSpec((tm, D_MODEL), row),
                  pl.BlockSpec((None, ADA_CHUNKS, D_MODEL), mod_idx)],
        out_specs=pl.BlockSpec((tm, D_MODEL), row),
        scratch_shapes=[pltpu.VMEM((tm, sub, LANES), BF16)])
    return pl.pallas_call(
        functools.partial(_combine_kernel, tm=tm),
        grid_spec=grid_spec,
        out_shape=jax.ShapeDtypeStruct((t, D_MODEL), F32),
        compiler_params=_cparams("arbitrary"),
        name="moe_combine",
    )(cls, rank, off, o_sorted, x1, mod)


def _rope_tables(n_tokens):
    rows = n_tokens // GRID_W
    row_ids = jnp.repeat(jnp.arange(rows, dtype=F32), GRID_W)
    col_ids = jnp.tile(jnp.arange(GRID_W, dtype=F32), rows)
    inv_freq = jnp.power(ROPE_THETA, -jnp.arange(ROPE_PAIRS, dtype=F32) / ROPE_PAIRS)
    ang_r = row_ids[:, None] * inv_freq[None, :]
    ang_c = col_ids[:, None] * inv_freq[None, :]
    ang = jnp.concatenate([ang_r, ang_r, ang_c, ang_c] * 2, axis=-1)
    return jnp.cos(ang), jnp.sin(ang)


def _layer(x, mod, lp, rope_tabs, cache, lam_rows, *, kv_dtype):
    b, n, _ = x.shape
    t = b * n
    q, k, v, p, g = _pre_mixer(x.reshape(t, D_MODEL), mod, lp, rope_tabs, seq=n, tm=256, kv_dtype=kv_dtype)
    o = _attention(q.reshape(b, n, ATTN_W), k.reshape(b, n, ATTN_W), v.reshape(b, n, ATTN_W),
                   cache, lam_rows, lp["subln_g"], tq=256, heads_per_step=N_HEADS if cache is None else 2)
    x1, h2p, gw, cls, rank, counts = _post_mixer(o.reshape(t, ATTN_W), p, g, x.reshape(t, D_MODEL), mod, lp,
                                                 seq=n, n_sub=2)
    tables = _moe_tables(counts, t)
    cls, rank = cls.reshape(t), rank.reshape(t)
    o_sorted = _moe(h2p, gw, cls, rank, tables, lp)
    y = _combine(o_sorted, x1, mod, cls, rank, tables[0], seq=n, tm=256)
    return y.reshape(b, n, D_MODEL), k, v


def kernel(x_prompt, x_sample, c, cache_k, cache_v, c_ctx, w_ada, b_ada, norm1_g, w_in, b_gate, q_norm_g, k_norm_g, lambda_q1, lambda_k1, lambda_q2, lambda_k2, subln_g, pool_w, pool_scale, w_br_attn, w_br_pool, w_out, norm2_g, router_group_w, router_group_b, router_expert_w, router_expert_b, expert_w_gate, expert_w_up, expert_w_down):
    b_ctx, n_ctx, _ = x_prompt.shape
    b_lat, n_lat, _ = x_sample.shape

    def router_layout(we, wg):
        pad = jnp.zeros(we.shape[:-1] + (LANES - N_EXPERTS,), F32)
        return jnp.concatenate([we, pad, jnp.repeat(wg, EXPERTS_PER_GROUP, axis=-1), pad], axis=-1)

    w_router = router_layout(router_expert_w[0], router_group_w[0])
    w_router_hi, w_router_lo = _split_bf16(w_router)
    seg = (jnp.arange(256)[:, None] // HEAD_DIM == jnp.arange(256)[None, :] // HEAD_DIM)
    lp = dict(
        norm1_g=norm1_g[0].reshape(1, D_MODEL),
        w_in=w_in[0].astype(BF16),
        b_gate=b_gate[0].reshape(1, 2 * D_MODEL),
        q_gain=jnp.tile(q_norm_g[0], ATTN_W // HEAD_DIM).reshape(1, ATTN_W),
        k_gain=jnp.tile(k_norm_g[0], ATTN_W // HEAD_DIM).reshape(1, ATTN_W),
        seg=(seg.astype(F32) / HEAD_DIM).astype(BF16),
        subln_g=subln_g[0].reshape(1, HEAD_W),
        w_br_attn=w_br_attn[0].astype(BF16),
        pool_w=pool_w[0].astype(BF16),
        pool_scale=pool_scale[0].reshape(1, POOL_W),
        w_br_pool=w_br_pool[0].astype(BF16),
        w_out=w_out[0].astype(BF16),
        norm2_g=norm2_g[0].reshape(1, D_MODEL),
        w_router_hi=w_router_hi, w_router_lo=w_router_lo,
        b_router=router_layout(router_expert_b[0], router_group_b[0]).reshape(1, ROUTER_W),
        expert_w_gate=expert_w_gate[0].astype(BF16),
        expert_w_up=expert_w_up[0].astype(BF16),
        expert_w_down=expert_w_down[0].astype(BF16),
    )
    lam_rows = jnp.stack([lambda_q1[0], lambda_k1[0], lambda_q2[0], lambda_k2[0]], axis=0)

    n_cond = 1 + b_lat
    cond = jnp.concatenate([c_ctx[None, :], c, jnp.zeros((16 - n_cond, D_MODEL), F32)], axis=0)
    mod = _adaln(cond, w_ada[0], b_ada[0])[:n_cond].reshape(n_cond, ADA_CHUNKS, D_MODEL)

    y_prompt, k_ctx, v_ctx = _layer(x_prompt, mod[:1], lp, None, None, lam_rows, kv_dtype=F32)
    cache = (cache_k, cache_v)
    y_sample, _, _ = _layer(x_sample, mod[1:], lp, _rope_tables(n_lat), cache, lam_rows, kv_dtype=BF16)

    new_cache_k = k_ctx.reshape(b_ctx, 1, n_ctx, N_HEADS, HEAD_W)
    new_cache_v = v_ctx.reshape(b_ctx, 1, n_ctx, N_HEADS, HEAD_W)
    return (y_prompt, y_sample, new_cache_k, new_cache_v)
```

```python
import functools
import math

import jax
import jax.numpy as jnp
from jax import lax
from jax.experimental import pallas as pl
from jax.experimental.pallas import tpu as pltpu

D_MODEL = 1024
GRID_W = 64
N_HEADS = 8
HEAD_DIM = 64
HEAD_W = 2 * HEAD_DIM
ATTN_W = N_HEADS * HEAD_W
POOL_GROUPS = 4
POOL_WINDOWS = (2, 4, 8, 16)
POOL_W = 512
POOL_GROUP_W = 128
IN_W = 3 * ATTN_W + POOL_W + 2 * D_MODEL
ROPE_THETA = 10000.0
ROPE_PAIRS = 16
N_GROUPS = 4
EXPERTS_PER_GROUP = 4
N_EXPERTS = 16
PAIRS_PER_GROUP = 6
N_CLASSES = N_GROUPS * PAIRS_PER_GROUP
PAIR_LO = (0, 0, 0, 1, 1, 2)
PAIR_HI = (1, 2, 3, 2, 3, 3)
D_EXPERT = 512
MOE_TILE = 128
POST_ROWS = 256
POOL_HALO = 16
ADA_CHUNKS = 6
EPS = 1e-6
LAMBDA_INIT = 0.8 - 0.6 * math.exp(-0.0)

LANES = 128
ROW_SUB = D_MODEL // LANES
ROUTER_W = 2 * LANES
NEG_BIG = -1e30
Q_SCALE = math.log2(math.e) * HEAD_DIM ** -0.5

F32 = jnp.float32
BF16 = jnp.bfloat16

VMEM_LIMIT = 56 * 1024 * 1024


def _cparams(*sem):
    return pltpu.CompilerParams(dimension_semantics=sem, vmem_limit_bytes=VMEM_LIMIT)


def _split_bf16(x):
    hi = x.astype(BF16)
    lo = (x - hi.astype(F32)).astype(BF16)
    return hi, lo


def _dot(a, b):
    return jnp.dot(a, b, preferred_element_type=F32)


def _adaln_kernel(cond_ref, w_ref, b_ref, o_ref):
    c = cond_ref[...]
    s = c * jax.nn.sigmoid(c)
    s_hi, s_lo = _split_bf16(s)
    w_hi, w_lo = _split_bf16(w_ref[...])
    rows = s.shape[0]
    both = _dot(jnp.concatenate([s_hi, s_lo], axis=0), w_hi)
    o_ref[...] = both[:rows] + both[rows:] + _dot(s_hi, w_lo) + b_ref[...]


def _adaln(cond, w_ada, b_ada):
    rows = cond.shape[0]
    n = w_ada.shape[1]
    tn = 1536
    return pl.pallas_call(
        _adaln_kernel,
        grid=(n // tn,),
        in_specs=[pl.BlockSpec((rows, D_MODEL), lambda j: (0, 0)),
                  pl.BlockSpec((D_MODEL, tn), lambda j: (0, j)),
                  pl.BlockSpec((1, tn), lambda j: (0, j))],
        out_specs=pl.BlockSpec((rows, tn), lambda j: (0, j)),
        out_shape=jax.ShapeDtypeStruct((rows, n), F32),
        compiler_params=_cparams("arbitrary"),
        name="adaln",
    )(cond, w_ada, b_ada.reshape(1, n))


def _chunk_rms(z, seg):
    zz = (z * z).astype(BF16)
    parts = [_dot(zz[:, c * 256:(c + 1) * 256], seg) for c in range(z.shape[1] // 256)]
    return jnp.concatenate(parts, axis=1)


def _rope(x, cos, sin):
    lane = lax.broadcasted_iota(jnp.int32, (x.shape[0], LANES), 1)
    first = (lane % (2 * ROPE_PAIRS)) < ROPE_PAIRS
    parts = []
    for c in range(x.shape[1] // LANES):
        xc = x[:, c * LANES:(c + 1) * LANES]
        up = pltpu.roll(xc, LANES - ROPE_PAIRS, 1)
        dn = pltpu.roll(xc, ROPE_PAIRS, 1)
        parts.append(xc * cos + jnp.where(first, -up, dn) * sin)
    return jnp.concatenate(parts, axis=1)


def _pre_kernel(*refs, rope, kv_dtype):
    if rope:
        (x_ref, mod_ref, g1_ref, w_ref, bg_ref, qg_ref, kg_ref, seg_ref, cos_ref, sin_ref,
         q_out, k_out, v_out, p_out, g_out) = refs
    else:
        (x_ref, mod_ref, g1_ref, w_ref, bg_ref, qg_ref, kg_ref, seg_ref,
         q_out, k_out, v_out, p_out, g_out) = refs
    x = x_ref[...]
    shift = mod_ref[0:1, :]
    scale = mod_ref[1:2, :]
    xn = x * lax.rsqrt(jnp.mean(x * x, axis=-1, keepdims=True) + EPS) * g1_ref[...]
    h = (xn * (1.0 + scale) + shift).astype(BF16)
    seg = seg_ref[...]

    zq = _dot(h, w_ref[:, 0:ATTN_W])
    qn = zq * lax.rsqrt(_chunk_rms(zq, seg) + EPS) * qg_ref[...]
    if rope:
        qn = _rope(qn, cos_ref[...], sin_ref[...])
    q_out[...] = (qn * Q_SCALE).astype(BF16)

    zk = _dot(h, w_ref[:, ATTN_W:2 * ATTN_W])
    kn = zk * lax.rsqrt(_chunk_rms(zk, seg) + EPS) * kg_ref[...]
    if rope:
        kn = _rope(kn, cos_ref[...], sin_ref[...])
    k_out[...] = kn.astype(kv_dtype)

    v_out[...] = _dot(h, w_ref[:, 2 * ATTN_W:3 * ATTN_W]).astype(kv_dtype)
    p_out[...] = _dot(h, w_ref[:, 3 * ATTN_W:3 * ATTN_W + POOL_W]).astype(BF16)
    gl = _dot(h, w_ref[:, 3 * ATTN_W + POOL_W:IN_W]) + bg_ref[...]
    g_out[...] = jax.nn.sigmoid(gl).astype(BF16)


def _pre_mixer(x, mod, lp, rope_tabs, *, seq, tm, kv_dtype):
    t = x.shape[0]
    tiles_per_seq = seq // tm
    single_mod = mod.shape[0] == 1
    mod_idx = (lambda i: (0, 0, 0)) if single_mod else (lambda i: (i // tiles_per_seq, 0, 0))
    const = lambda i: (0, 0)
    row = lambda i: (i, 0)
    in_specs = [pl.BlockSpec((tm, D_MODEL), row),
                pl.BlockSpec((None, ADA_CHUNKS, D_MODEL), mod_idx),
                pl.BlockSpec((1, D_MODEL), const),
                pl.BlockSpec((D_MODEL, IN_W), const),
                pl.BlockSpec((1, 2 * D_MODEL), const),
                pl.BlockSpec((1, ATTN_W), const),
                pl.BlockSpec((1, ATTN_W), const),
                pl.BlockSpec((256, 256), const)]
    args = [x, mod, lp["norm1_g"], lp["w_in"], lp["b_gate"], lp["q_gain"], lp["k_gain"], lp["seg"]]
    rope = rope_tabs is not None
    if rope:
        in_specs += [pl.BlockSpec((tm, LANES), lambda i: (i % tiles_per_seq, 0))] * 2
        args += list(rope_tabs)
    out_shape = [jax.ShapeDtypeStruct((t, ATTN_W), BF16),
                 jax.ShapeDtypeStruct((t, ATTN_W), kv_dtype),
                 jax.ShapeDtypeStruct((t, ATTN_W), kv_dtype),
                 jax.ShapeDtypeStruct((t, POOL_W), BF16),
                 jax.ShapeDtypeStruct((t, 2 * D_MODEL), BF16)]
    out_specs = [pl.BlockSpec((tm, ATTN_W), row), pl.BlockSpec((tm, ATTN_W), row),
                 pl.BlockSpec((tm, ATTN_W), row), pl.BlockSpec((tm, POOL_W), row),
                 pl.BlockSpec((tm, 2 * D_MODEL), row)]
    return pl.pallas_call(
        functools.partial(_pre_kernel, rope=rope, kv_dtype=kv_dtype),
        grid=(t // tm,),
        in_specs=in_specs, out_specs=out_specs, out_shape=out_shape,
        compiler_params=_cparams("arbitrary"),
        name="pre_mixer_rope" if rope else "pre_mixer",
    )(*args)


def _attn_kernel(*refs, n_q_blocks, tq, has_cache, heads_per_step):
    if has_cache:
        q_ref, k_ref, v_ref, ck_ref, cv_ref, lam_ref, sg_ref, o_ref, k1_s, k2_s, v_s = refs
    else:
        q_ref, k_ref, v_ref, lam_ref, sg_ref, o_ref, k1_s, k2_s, v_s = refs
    lv = lam_ref[...]
    lam = (jnp.exp(jnp.sum(lv[0:1] * lv[1:2], axis=-1, keepdims=True))
           - jnp.exp(jnp.sum(lv[2:3] * lv[3:4], axis=-1, keepdims=True)) + LAMBDA_INIT)
    nt = (((1,), (1,)), ((), ()))
    sg = sg_ref[...] * (1.0 - LAMBDA_INIT)
    n = k_ref.shape[0]

    def stage(hh, rows, kf, vf):
        kf = kf.astype(F32)
        lane = lax.broadcasted_iota(jnp.int32, kf.shape, 1)
        k1_s[hh, rows, :] = jnp.where(lane < HEAD_DIM, kf, 0.0).astype(BF16)
        k2_s[hh, rows, :] = jnp.where(lane >= HEAD_DIM, kf, 0.0).astype(BF16)
        v_s[hh, rows, :HEAD_W] = vf.astype(BF16)

    v_s[:, :, HEAD_W:] = jnp.ones(v_s.shape[:2] + (HEAD_W,), BF16)

    for hh in range(heads_per_step):
        cols = slice(hh * HEAD_W, (hh + 1) * HEAD_W)
        stage(hh, slice(0, n), k_ref[:, cols], v_ref[:, cols])
        if has_cache:
            head = pl.program_id(1) * heads_per_step + hh
            stage(hh, slice(n, k1_s.shape[1]), ck_ref[:, head, :], cv_ref[:, head, :])

        def softmax_av(q, k_s):
            s = lax.dot_general(q, k_s[hh], nt, preferred_element_type=F32)
            e = jnp.exp2(s - jnp.max(s, axis=-1, keepdims=True)).astype(BF16)
            ov = _dot(e, v_s[hh])
            return ov[:, :HEAD_W] / ov[:, HEAD_W:]

        def block(i, carry):
            qs = pl.multiple_of(i * tq, tq)
            q = q_ref[pl.ds(qs, tq), cols]
            o = softmax_av(q, k1_s) - lam * softmax_av(q, k2_s)
            on = o * lax.rsqrt(jnp.mean(o * o, axis=-1, keepdims=True) + EPS) * sg
            o_ref[pl.ds(qs, tq), cols] = on.astype(o_ref.dtype)
            return carry

        if n_q_blocks == 1:
            block(0, 0)
        else:
            lax.fori_loop(0, n_q_blocks, block, 0, unroll=True)


def _attention(q, k, v, cache, lam_rows, subln_g, *, tq, heads_per_step):
    b, n, _ = q.shape
    has_cache = cache is not None
    heads = lambda bi, hi: (bi, 0, hi)
    const = lambda bi, hi: (0, 0)
    in_specs = [pl.BlockSpec((None, n, heads_per_step * HEAD_W), heads)] * 3
    args = [q, k, v]
    if has_cache:
        p_len = cache[0].shape[2]
        in_specs += [pl.BlockSpec((None, None, p_len, N_HEADS, HEAD_W), lambda bi, hi: (bi, 0, 0, 0, 0))] * 2
        args += list(cache)
    in_specs += [pl.BlockSpec((4, HEAD_DIM), const), pl.BlockSpec((1, HEAD_W), const)]
    args += [lam_rows, subln_g]
    n_keys = n + (cache[0].shape[2] if has_cache else 0)
    return pl.pallas_call(
        functools.partial(_attn_kernel, n_q_blocks=n // tq, tq=tq, has_cache=has_cache,
                          heads_per_step=heads_per_step),
        grid=(b, N_HEADS // heads_per_step),
        in_specs=in_specs,
        out_specs=pl.BlockSpec((None, n, heads_per_step * HEAD_W), heads),
        out_shape=jax.ShapeDtypeStruct((b, n, ATTN_W), BF16),
        scratch_shapes=[pltpu.VMEM((heads_per_step, n_keys, HEAD_W), BF16)] * 2
        + [pltpu.VMEM((heads_per_step, n_keys, 2 * HEAD_W), BF16)],
        compiler_params=_cparams("arbitrary", "arbitrary"),
        name="diff_attn_cache" if has_cache else "diff_attn",
    )(*args)


def _route(logits):
    rows = logits.shape[0]
    lane_i = lax.broadcasted_iota(jnp.int32, (rows, LANES), 1)
    valid = lane_i < N_EXPERTS
    lane = lane_i.astype(F32)
    grp = (lane_i // EXPERTS_PER_GROUP).astype(F32)
    e_log = logits[:, :LANES]
    g_log = jnp.where(valid, logits[:, LANES:], NEG_BIG)
    g_max = jnp.max(g_log, axis=-1, keepdims=True)
    g_den = jnp.sum(jnp.exp(g_log - g_max), axis=-1, keepdims=True) * (1.0 / EXPERTS_PER_GROUP)
    g_w = 1.0 / g_den
    g_idx = jnp.min(jnp.where(g_log == g_max, grp, float(N_GROUPS)), axis=-1, keepdims=True)
    e_sel = jnp.where(grp == g_idx, jnp.where(valid, e_log, NEG_BIG), NEG_BIG)
    v1 = jnp.max(e_sel, axis=-1, keepdims=True)
    i1 = jnp.min(jnp.where(e_sel == v1, lane, float(LANES)), axis=-1, keepdims=True)
    e_rest = jnp.where(lane == i1, NEG_BIG, e_sel)
    v2 = jnp.max(e_rest, axis=-1, keepdims=True)
    i2 = jnp.min(jnp.where(e_rest == v2, lane, float(LANES)), axis=-1, keepdims=True)
    t = jnp.exp(v2 - v1)
    w1 = g_w / (1.0 + t)
    w2 = w1 * t
    first_low = i1 < i2
    a = jnp.minimum(i1, i2) - EXPERTS_PER_GROUP * g_idx
    b = jnp.maximum(i1, i2) - EXPERTS_PER_GROUP * g_idx
    pair = a * (7.0 - a) * 0.5 + (b - a - 1.0)
    cls = g_idx * PAIRS_PER_GROUP + pair
    return cls, jnp.where(first_low, w1, w2), jnp.where(first_low, w2, w1)


def _post_kernel(o_ref, p_ref, g_ref, x_ref, mod_ref, wa_ref, pw_ref, ps_ref, wp_ref, wo_ref,
                 g2_ref, wrh_ref, wrl_ref, br_ref, x1_out, h2p_out, gw_out, cls_out, rank_out, counts_out,
                 carry_ref, *, seq, n_sub):
    step = pl.program_id(0)

    @pl.when(step == 0)
    def _():
        carry_ref[...] = jnp.zeros_like(carry_ref)

    tm = POST_ROWS
    tiles_per_seq = max(seq // (n_sub * tm), 1)
    ext = tm + 2 * POOL_HALO
    chains = range(n_sub)
    rows = [slice(s * tm, (s + 1) * tm) for s in chains]
    if seq == tm:
        blk0, t0 = [s * tm for s in chains], [0] * n_sub
    else:
        blk0 = t0 = [pl.multiple_of(((step % tiles_per_seq) * n_sub + s) * tm, tm) for s in chains]

    attn_out = [_dot(o_ref[rows[s], :], wa_ref[...]) for s in chains]

    def pooled(s):
        p_mid = p_ref[pl.ds(blk0[s], tm), :].astype(F32)
        if seq == tm:
            halo_top = halo_bot = jnp.zeros((POOL_HALO, POOL_W), F32)
        else:
            top0 = pl.multiple_of(jnp.maximum(blk0[s] - POOL_HALO, 0), POOL_HALO)
            bot0 = pl.multiple_of(jnp.minimum(blk0[s] + tm, seq - POOL_HALO), POOL_HALO)
            halo_top = p_ref[pl.ds(top0, POOL_HALO), :].astype(F32) * jnp.where(t0[s] > 0, 1.0, 0.0)
            halo_bot = p_ref[pl.ds(bot0, POOL_HALO), :].astype(F32) * jnp.where(t0[s] + tm < seq, 1.0, 0.0)
        p_ext = jnp.concatenate([halo_top, p_mid, halo_bot], axis=0)
        tok1 = t0[s] + lax.broadcasted_iota(jnp.int32, (tm, 1), 0)
        out = []
        for gi, w in enumerate(POOL_WINDOWS):
            half = w // 2
            sl = slice(gi * POOL_GROUP_W, (gi + 1) * POOL_GROUP_W)
            run = p_ext[:, sl]
            k = 1
            while k < w:
                run = run + pltpu.roll(run, ext - k, 0)
                k *= 2
            win = pltpu.roll(run, ext - (POOL_HALO - half), 0)[:tm]
            cnt = (jnp.minimum(tok1 + half, seq) - jnp.maximum(tok1 - half, 0)).astype(F32)
            out.append((win / cnt - p_mid[:, sl]).astype(BF16))
        return out

    pool_in = [pooled(s) for s in chains]
    mixed = [jnp.concatenate([_dot(pool_in[s][gi], pw_ref[gi]) for gi in range(POOL_GROUPS)], axis=1)
             * ps_ref[...] for s in chains]
    pool_out = [_dot(mixed[s].astype(BF16), wp_ref[...]) for s in chains]

    def merge(s):
        g = g_ref[rows[s], :]
        return (g[:, :D_MODEL].astype(F32) * attn_out[s] + g[:, D_MODEL:].astype(F32) * pool_out[s]).astype(BF16)

    merged = [merge(s) for s in chains]
    gate1 = mod_ref[2:3, :]
    x1 = [x_ref[rows[s], :] + gate1 * _dot(merged[s], wo_ref[...]) for s in chains]
    for s in chains:
        x1_out[rows[s], :] = x1[s]

    shift2 = mod_ref[3:4, :]
    scale2 = mod_ref[4:5, :]
    h2 = [x1[s] * lax.rsqrt(jnp.mean(x1[s] * x1[s], axis=-1, keepdims=True) + EPS) * g2_ref[...]
          * (1.0 + scale2) + shift2 for s in chains]
    h2_parts = [_split_bf16(h2[s]) for s in chains]
    for s in chains:
        h2p_out[rows[s]] = h2_parts[s][0].reshape(tm, ROW_SUB, LANES)

    logits = [_dot(h2_parts[s][0], wrh_ref[...]) + _dot(h2_parts[s][1], wrh_ref[...])
              + _dot(h2_parts[s][0], wrl_ref[...]) + br_ref[...] for s in chains]
    routes = [_route(logits[s]) for s in chains]
    lane = lax.broadcasted_iota(jnp.int32, (tm, LANES), 1)
    for s in chains:
        _, w_lo, w_hi = routes[s]
        gw_out[rows[s]] = jnp.where(lane == 0, w_lo, jnp.where(lane == 1, w_hi, 0.0)).reshape(tm, 1, LANES)

    row = lax.broadcasted_iota(jnp.int32, (tm, tm), 0)
    col = lax.broadcasted_iota(jnp.int32, (tm, tm), 1)
    before = jnp.where(col < row, 1.0, 0.0).astype(BF16)
    onehot = [jnp.where(lane.astype(F32) == routes[s][0], 1.0, 0.0) for s in chains]
    within = [_dot(before, onehot[s].astype(BF16)) for s in chains]
    for s in chains:
        rank = within[s] + carry_ref[...]
        cls_out[rows[s], :] = routes[s][0].astype(jnp.int32)
        rank_out[rows[s], :] = jnp.sum(onehot[s] * rank, axis=-1, keepdims=True).astype(jnp.int32)
        carry_ref[...] += jnp.sum(onehot[s], axis=0, keepdims=True)
    counts_out[...] = carry_ref[...]


def _post_mixer(o, p, g, x, mod, lp, *, seq, n_sub):
    t = x.shape[0]
    tm = n_sub * POST_ROWS
    tiles_per_seq = max(seq // tm, 1)
    single_mod = mod.shape[0] == 1
    mod_idx = (lambda i: (0, 0, 0)) if single_mod else (lambda i: (i // tiles_per_seq, 0, 0))
    tile = lambda i: (i, 0)
    const2 = lambda i: (0, 0)
    const3 = lambda i: (0, 0, 0)
    p_spec = (pl.BlockSpec((tm, POOL_W), tile) if seq == POST_ROWS
              else pl.BlockSpec((seq, POOL_W), lambda i: (i // tiles_per_seq, 0)))
    in_specs = [pl.BlockSpec((tm, ATTN_W), tile),
                p_spec,
                pl.BlockSpec((tm, 2 * D_MODEL), tile),
                pl.BlockSpec((tm, D_MODEL), tile),
                pl.BlockSpec((None, ADA_CHUNKS, D_MODEL), mod_idx),
                pl.BlockSpec((ATTN_W, D_MODEL), const2),
                pl.BlockSpec((POOL_GROUPS, POOL_GROUP_W, POOL_GROUP_W), const3),
                pl.BlockSpec((1, POOL_W), const2),
                pl.BlockSpec((POOL_W, D_MODEL), const2),
                pl.BlockSpec((D_MODEL, D_MODEL), const2),
                pl.BlockSpec((1, D_MODEL), const2),
                pl.BlockSpec((D_MODEL, ROUTER_W), const2),
                pl.BlockSpec((D_MODEL, ROUTER_W), const2),
                pl.BlockSpec((1, ROUTER_W), const2)]
    tile3 = lambda i: (i, 0, 0)
    out_shape = [jax.ShapeDtypeStruct((t, D_MODEL), F32),
                 jax.ShapeDtypeStruct((t, ROW_SUB, LANES), BF16),
                 jax.ShapeDtypeStruct((t, 1, LANES), F32),
                 jax.ShapeDtypeStruct((t, 1), jnp.int32),
                 jax.ShapeDtypeStruct((t, 1), jnp.int32),
                 jax.ShapeDtypeStruct((1, LANES), F32)]
    out_specs = [pl.BlockSpec((tm, D_MODEL), tile),
                 pl.BlockSpec((tm, ROW_SUB, LANES), tile3),
                 pl.BlockSpec((tm, 1, LANES), tile3),
                 pl.BlockSpec((tm, 1), tile),
                 pl.BlockSpec((tm, 1), tile),
                 pl.BlockSpec((1, LANES), const2)]
    return pl.pallas_call(
        functools.partial(_post_kernel, seq=seq, n_sub=n_sub),
        grid=(t // tm,),
        in_specs=in_specs, out_specs=out_specs, out_shape=out_shape,
        scratch_shapes=[pltpu.VMEM((1, LANES), F32)],
        compiler_params=_cparams("arbitrary"),
        name="post_mixer",
    )(o, p, g, x, mod, lp["w_br_attn"], lp["pool_w"], lp["pool_scale"], lp["w_br_pool"], lp["w_out"],
      lp["norm2_g"], lp["w_router_hi"], lp["w_router_lo"], lp["b_router"])


def _moe_tables(counts, n_tokens):
    cnt = counts[0, :N_CLASSES].astype(jnp.int32)
    tiles = (cnt + MOE_TILE - 1) // MOE_TILE
    upto = jnp.arange(N_CLASSES)[:, None] <= jnp.arange(N_CLASSES)[None, :]
    ends = jnp.sum(jnp.where(upto, tiles[:, None], 0), axis=0)
    off = jnp.concatenate([jnp.zeros((1,), jnp.int32), ends]) * MOE_TILE
    n_tiles = ends[-1:]
    max_tiles = n_tokens // MOE_TILE + N_CLASSES
    tile_id = jnp.minimum(jnp.arange(max_tiles), n_tiles - 1)
    tile_cls = jnp.minimum(jnp.sum(ends[None, :] <= tile_id[:, None], axis=1), N_CLASSES - 1)
    group, pair = tile_cls // PAIRS_PER_GROUP, tile_cls % PAIRS_PER_GROUP
    e_lo = group * EXPERTS_PER_GROUP + jnp.asarray(PAIR_LO, jnp.int32)[pair]
    e_hi = group * EXPERTS_PER_GROUP + jnp.asarray(PAIR_HI, jnp.int32)[pair]
    return off.astype(jnp.int32), e_lo.astype(jnp.int32), e_hi.astype(jnp.int32), n_tiles.astype(jnp.int32)


def _moe_kernel(cls_s, rank_s, off_s, elo_s, ehi_s, nt_s, h_ref, gw_ref, wg1_ref, wu1_ref, wd1_ref,
                wg2_ref, wu2_ref, wd2_ref, o_ref, src_s, xg_ref, gwg_ref, *, n_tokens):
    j = pl.program_id(0)
    n_tiles = nt_s[0]

    def gather_tile(tile, slot):
        base = tile * MOE_TILE
        for r in range(MOE_TILE):
            t = src_s[base + r]
            xg_ref[slot, r] = h_ref[t]
            gwg_ref[slot, r] = gw_ref[t]

    @pl.when(j == 0)
    def _():
        def clear_tail(c, carry):
            start = jnp.maximum(off_s[c + 1] - MOE_TILE, 0)
            for i in range(MOE_TILE):
                src_s[start + i] = 0
            return carry

        lax.fori_loop(0, N_CLASSES, clear_tail, 0)

        def place(t, c):
            src_s[off_s[cls_s[t]] + rank_s[t]] = t
            return c

        lax.fori_loop(0, n_tokens, place, 0, unroll=16)
        gather_tile(0, 0)

    @pl.when(j < n_tiles)
    def _():
        slot = j % 2
        gather_tile(jnp.minimum(j + 1, n_tiles - 1), 1 - slot)
        x = xg_ref[slot].reshape(MOE_TILE, D_MODEL)
        gw = gwg_ref[slot].reshape(MOE_TILE, LANES)

        def hidden(wg_ref, wu_ref, gate):
            a = _dot(x, wg_ref[...])
            u = _dot(x, wu_ref[...])
            return (a * jax.nn.sigmoid(a) * u * gate).astype(BF16)

        o = (_dot(hidden(wg1_ref, wu1_ref, gw[:, 0:1]), wd1_ref[...])
             + _dot(hidden(wg2_ref, wu2_ref, gw[:, 1:2]), wd2_ref[...]))
        o_ref[...] = o.astype(BF16).reshape(o_ref.shape)

    @pl.when(j >= n_tiles)
    def _():
        o_ref[...] = jnp.zeros_like(o_ref)


def _moe(h2p, gw, cls, rank, tables, lp):
    t, sub, _ = h2p.shape
    off, e_lo, e_hi, n_tiles = tables
    max_tiles = e_lo.shape[0]
    whole = lambda j, *_: (0, 0, 0)
    lo = lambda j, cls_s, rank_s, off_s, elo_s, ehi_s, nt_s: (elo_s[j], 0, 0)
    hi = lambda j, cls_s, rank_s, off_s, elo_s, ehi_s, nt_s: (ehi_s[j], 0, 0)
    w_in = lambda idx: pl.BlockSpec((None, D_MODEL, D_EXPERT), idx)
    w_out = lambda idx: pl.BlockSpec((None, D_EXPERT, D_MODEL), idx)
    grid_spec = pltpu.PrefetchScalarGridSpec(
        num_scalar_prefetch=6,
        grid=(max_tiles,),
        in_specs=[pl.BlockSpec((t, sub, LANES), whole, pipeline_mode=pl.Buffered(1)),
                  pl.BlockSpec((t, 1, LANES), whole, pipeline_mode=pl.Buffered(1)),
                  w_in(lo), w_in(lo), w_out(lo), w_in(hi), w_in(hi), w_out(hi)],
        out_specs=pl.BlockSpec((MOE_TILE, sub, LANES), lambda j, *_: (j, 0, 0)),
        scratch_shapes=[pltpu.SMEM((max_tiles * MOE_TILE,), jnp.int32),
                        pltpu.VMEM((2, MOE_TILE, sub, LANES), BF16),
                        pltpu.VMEM((2, MOE_TILE, 1, LANES), F32)])
    wg, wu, wd = lp["expert_w_gate"], lp["expert_w_up"], lp["expert_w_down"]
    return pl.pallas_call(
        functools.partial(_moe_kernel, n_tokens=t),
        grid_spec=grid_spec,
        out_shape=jax.ShapeDtypeStruct((max_tiles * MOE_TILE, sub, LANES), BF16),
        compiler_params=_cparams("arbitrary"),
        name="moe",
    )(cls, rank, off, e_lo, e_hi, n_tiles, h2p, gw, wg, wu, wd, wg, wu, wd)


def _combine_kernel(cls_s, rank_s, off_s, o_ref, x1_ref, mod_ref, y_ref, og_ref, *, tm):
    base = pl.program_id(0) * tm

    def gather(r, c):
        t = base + r
        og_ref[r] = o_ref[off_s[cls_s[t]] + rank_s[t]]
        return c

    lax.fori_loop(0, tm, gather, 0, unroll=8)
    moe = og_ref[...].reshape(tm, D_MODEL).astype(F32)
    y_ref[...] = x1_ref[...] + mod_ref[5:6, :] * moe


def _combine(o_sorted, x1, mod, cls, rank, off, *, seq, tm):
    t = x1.shape[0]
    rows, sub, _ = o_sorted.shape
    tiles_per_seq = seq // tm
    single_mod = mod.shape[0] == 1
    mod_idx = (lambda i, *_: (0, 0, 0)) if single_mod else (lambda i, *_: (i // tiles_per_seq, 0, 0))
    row = lambda i, *_: (i, 0)
    grid_spec = pltpu.PrefetchScalarGridSpec(
        num_scalar_prefetch=3,
        grid=(t // tm,),
        in_specs=[pl.BlockSpec((rows, sub, LANES), lambda i, *_: (0, 0, 0), pipeline_mode=pl.Buffered(1)),
                  pl.BlockSpec((tm, D_MODEL), row),
                  pl.BlockSpec((None, ADA_CHUNKS, D_MODEL), mod_idx)],
        out_specs=pl.BlockSpec((tm, D_MODEL), row),
        scratch_shapes=[pltpu.VMEM((tm, sub, LANES), BF16)])
    return pl.pallas_call(
        functools.partial(_combine_kernel, tm=tm),
        grid_spec=grid_spec,
        out_shape=jax.ShapeDtypeStruct((t, D_MODEL), F32),
        compiler_params=_cparams("arbitrary"),
        name="moe_combine",
    )(cls, rank, off, o_sorted, x1, mod)


def _rope_tables(n_tokens):
    rows = n_tokens // GRID_W
    row_ids = jnp.repeat(jnp.arange(rows, dtype=F32), GRID_W)
    col_ids = jnp.tile(jnp.arange(GRID_W, dtype=F32), rows)
    inv_freq = jnp.power(ROPE_THETA, -jnp.arange(ROPE_PAIRS, dtype=F32) / ROPE_PAIRS)
    ang_r = row_ids[:, None] * inv_freq[None, :]
    ang_c = col_ids[:, None] * inv_freq[None, :]
    ang = jnp.concatenate([ang_r, ang_r, ang_c, ang_c] * 2, axis=-1)
    return jnp.cos(ang), jnp.sin(ang)


def _layer(x, mod, lp, rope_tabs, cache, lam_rows, *, kv_dtype):
    b, n, _ = x.shape
    t = b * n
    q, k, v, p, g = _pre_mixer(x.reshape(t, D_MODEL), mod, lp, rope_tabs, seq=n, tm=256, kv_dtype=kv_dtype)
    o = _attention(q.reshape(b, n, ATTN_W), k.reshape(b, n, ATTN_W), v.reshape(b, n, ATTN_W),
                   cache, lam_rows, lp["subln_g"], tq=256, heads_per_step=N_HEADS if cache is None else 2)
    x1, h2p, gw, cls, rank, counts = _post_mixer(o.reshape(t, ATTN_W), p, g, x.reshape(t, D_MODEL), mod, lp,
                                                 seq=n, n_sub=2)
    tables = _moe_tables(counts, t)
    cls, rank = cls.reshape(t), rank.reshape(t)
    o_sorted = _moe(h2p, gw, cls, rank, tables, lp)
    y = _combine(o_sorted, x1, mod, cls, rank, tables[0], seq=n, tm=256)
    return y.reshape(b, n, D_MODEL), k, v


def kernel(x_prompt, x_sample, c, cache_k, cache_v, c_ctx, w_ada, b_ada, norm1_g, w_in, b_gate, q_norm_g, k_norm_g, lambda_q1, lambda_k1, lambda_q2, lambda_k2, subln_g, pool_w, pool_scale, w_br_attn, w_br_pool, w_out, norm2_g, router_group_w, router_group_b, router_expert_w, router_expert_b, expert_w_gate, expert_w_up, expert_w_down):
    b_ctx, n_ctx, _ = x_prompt.shape
    b_lat, n_lat, _ = x_sample.shape

    def router_layout(we, wg):
        pad = jnp.zeros(we.shape[:-1] + (LANES - N_EXPERTS,), F32)
        return jnp.concatenate([we, pad, jnp.repeat(wg, EXPERTS_PER_GROUP, axis=-1), pad], axis=-1)

    w_router = router_layout(router_expert_w[0], router_group_w[0])
    w_router_hi, w_router_lo = _split_bf16(w_router)
    seg = (jnp.arange(256)[:, None] // HEAD_DIM == jnp.arange(256)[None, :] // HEAD_DIM)
    lp = dict(
        norm1_g=norm1_g[0].reshape(1, D_MODEL),
        w_in=w_in[0].astype(BF16),
        b_gate=b_gate[0].reshape(1, 2 * D_MODEL),
        q_gain=jnp.tile(q_norm_g[0], ATTN_W // HEAD_DIM).reshape(1, ATTN_W),
        k_gain=jnp.tile(k_norm_g[0], ATTN_W // HEAD_DIM).reshape(1, ATTN_W),
        seg=(seg.astype(F32) / HEAD_DIM).astype(BF16),
        subln_g=subln_g[0].reshape(1, HEAD_W),
        w_br_attn=w_br_attn[0].astype(BF16),
        pool_w=pool_w[0].astype(BF16),
        pool_scale=pool_scale[0].reshape(1, POOL_W),
        w_br_pool=w_br_pool[0].astype(BF16),
        w_out=w_out[0].astype(BF16),
        norm2_g=norm2_g[0].reshape(1, D_MODEL),
        w_router_hi=w_router_hi, w_router_lo=w_router_lo,
        b_router=router_layout(router_expert_b[0], router_group_b[0]).reshape(1, ROUTER_W),
        expert_w_gate=expert_w_gate[0].astype(BF16),
        expert_w_up=expert_w_up[0].astype(BF16),
        expert_w_down=expert_w_down[0].astype(BF16),
    )
    lam_rows = jnp.stack([lambda_q1[0], lambda_k1[0], lambda_q2[0], lambda_k2[0]], axis=0)

    n_cond = 1 + b_lat
    cond = jnp.concatenate([c_ctx[None, :], c, jnp.zeros((16 - n_cond, D_MODEL), F32)], axis=0)
    mod = _adaln(cond, w_ada[0], b_ada[0])[:n_cond].reshape(n_cond, ADA_CHUNKS, D_MODEL)

    y_prompt, k_ctx, v_ctx = _layer(x_prompt, mod[:1], lp, None, None, lam_rows, kv_dtype=F32)
    cache = (cache_k, cache_v)
    y_sample, _, _ = _layer(x_sample, mod[1:], lp, _rope_tables(n_lat), cache, lam_rows, kv_dtype=BF16)

    new_cache_k = k_ctx.reshape(b_ctx, 1, n_ctx, N_HEADS, HEAD_W)
    new_cache_v = v_ctx.reshape(b_ctx, 1, n_ctx, N_HEADS, HEAD_W)
    return (y_prompt, y_sample, new_cache_k, new_cache_v)
```

```python
import functools
import math

import jax
import jax.numpy as jnp
from jax import lax
from jax.experimental import pallas as pl
from jax.experimental.pallas import tpu as pltpu

D_MODEL = 1024
GRID_W = 64
N_HEADS = 8
HEAD_DIM = 64
HEAD_W = 2 * HEAD_DIM
ATTN_W = N_HEADS * HEAD_W
POOL_GROUPS = 4
POOL_WINDOWS = (2, 4, 8, 16)
POOL_W = 512
POOL_GROUP_W = 128
IN_W = 3 * ATTN_W + POOL_W + 2 * D_MODEL
ROPE_THETA = 10000.0
ROPE_PAIRS = 16
N_GROUPS = 4
EXPERTS_PER_GROUP = 4
N_EXPERTS = 16
PAIRS_PER_GROUP = 6
N_CLASSES = N_GROUPS * PAIRS_PER_GROUP
PAIR_LO = (0, 0, 0, 1, 1, 2)
PAIR_HI = (1, 2, 3, 2, 3, 3)
D_EXPERT = 512
MOE_TILE = 128
POST_ROWS = 256
POOL_HALO = 16
ADA_CHUNKS = 6
EPS = 1e-6
LAMBDA_INIT = 0.8 - 0.6 * math.exp(-0.0)

LANES = 128
ROW_SUB = D_MODEL // LANES
ROUTER_W = 2 * LANES
NEG_BIG = -1e30
Q_SCALE = math.log2(math.e) * HEAD_DIM ** -0.5

F32 = jnp.float32
BF16 = jnp.bfloat16

VMEM_LIMIT = 56 * 1024 * 1024


def _cparams(*sem):
    return pltpu.CompilerParams(dimension_semantics=sem, vmem_limit_bytes=VMEM_LIMIT)


def _split_bf16(x):
    hi = x.astype(BF16)
    lo = (x - hi.astype(F32)).astype(BF16)
    return hi, lo


def _dot(a, b):
    return jnp.dot(a, b, preferred_element_type=F32)


def _adaln_kernel(cond_ref, w_ref, b_ref, o_ref):
    c = cond_ref[...]
    s = c * jax.nn.sigmoid(c)
    s_hi, s_lo = _split_bf16(s)
    w_hi, w_lo = _split_bf16(w_ref[...])
    rows = s.shape[0]
    both = _dot(jnp.concatenate([s_hi, s_lo], axis=0), w_hi)
    o_ref[...] = both[:rows] + both[rows:] + _dot(s_hi, w_lo) + b_ref[...]


def _adaln(cond, w_ada, b_ada):
    rows = cond.shape[0]
    n = w_ada.shape[1]
    tn = 1536
    return pl.pallas_call(
        _adaln_kernel,
        grid=(n // tn,),
        in_specs=[pl.BlockSpec((rows, D_MODEL), lambda j: (0, 0)),
                  pl.BlockSpec((D_MODEL, tn), lambda j: (0, j)),
                  pl.BlockSpec((1, tn), lambda j: (0, j))],
        out_specs=pl.BlockSpec((rows, tn), lambda j: (0, j)),
        out_shape=jax.ShapeDtypeStruct((rows, n), F32),
        compiler_params=_cparams("arbitrary"),
        name="adaln",
    )(cond, w_ada, b_ada.reshape(1, n))


def _chunk_rms(z, seg):
    zz = (z * z).astype(BF16)
    parts = [_dot(zz[:, c * 256:(c + 1) * 256], seg) for c in range(z.shape[1] // 256)]
    return jnp.concatenate(parts, axis=1)


def _rope(x, cos, sin):
    lane = lax.broadcasted_iota(jnp.int32, (x.shape[0], LANES), 1)
    first = (lane % (2 * ROPE_PAIRS)) < ROPE_PAIRS
    parts = []
    for c in range(x.shape[1] // LANES):
        xc = x[:, c * LANES:(c + 1) * LANES]
        up = pltpu.roll(xc, LANES - ROPE_PAIRS, 1)
        dn = pltpu.roll(xc, ROPE_PAIRS, 1)
        parts.append(xc * cos + jnp.where(first, -up, dn) * sin)
    return jnp.concatenate(parts, axis=1)


def _pre_kernel(*refs, rope, kv_dtype):
    if rope:
        (x_ref, mod_ref, g1_ref, w_ref, bg_ref, qg_ref, kg_ref, seg_ref, cos_ref, sin_ref,
         q_out, k_out, v_out, p_out, g_out) = refs
    else:
        (x_ref, mod_ref, g1_ref, w_ref, bg_ref, qg_ref, kg_ref, seg_ref,
         q_out, k_out, v_out, p_out, g_out) = refs
    x = x_ref[...]
    shift = mod_ref[0:1, :]
    scale = mod_ref[1:2, :]
    xn = x * lax.rsqrt(jnp.mean(x * x, axis=-1, keepdims=True) + EPS) * g1_ref[...]
    h = (xn * (1.0 + scale) + shift).astype(BF16)
    seg = seg_ref[...]

    zq = _dot(h, w_ref[:, 0:ATTN_W])
    qn = zq * lax.rsqrt(_chunk_rms(zq, seg) + EPS) * qg_ref[...]
    if rope:
        qn = _rope(qn, cos_ref[...], sin_ref[...])
    q_out[...] = (qn * Q_SCALE).astype(BF16)

    zk = _dot(h, w_ref[:, ATTN_W:2 * ATTN_W])
    kn = zk * lax.rsqrt(_chunk_rms(zk, seg) + EPS) * kg_ref[...]
    if rope:
        kn = _rope(kn, cos_ref[...], sin_ref[...])
    k_out[...] = kn.astype(kv_dtype)

    v_out[...] = _dot(h, w_ref[:, 2 * ATTN_W:3 * ATTN_W]).astype(kv_dtype)
    p_out[...] = _dot(h, w_ref[:, 3 * ATTN_W:3 * ATTN_W + POOL_W]).astype(BF16)
    gl = _dot(h, w_ref[:, 3 * ATTN_W + POOL_W:IN_W]) + bg_ref[...]
    g_out[...] = jax.nn.sigmoid(gl).astype(BF16)


def _pre_mixer(x, mod, lp, rope_tabs, *, seq, tm, kv_dtype):
    t = x.shape[0]
    tiles_per_seq = seq // tm
    single_mod = mod.shape[0] == 1
    mod_idx = (lambda i: (0, 0, 0)) if single_mod else (lambda i: (i // tiles_per_seq, 0, 0))
    const = lambda i: (0, 0)
    row = lambda i: (i, 0)
    in_specs = [pl.BlockSpec((tm, D_MODEL), row),
                pl.BlockSpec((None, ADA_CHUNKS, D_MODEL), mod_idx),
                pl.BlockSpec((1, D_MODEL), const),
                pl.BlockSpec((D_MODEL, IN_W), const, pipeline_mode=pl.Buffered(1)),
                pl.BlockSpec((1, 2 * D_MODEL), const),
                pl.BlockSpec((1, ATTN_W), const),
                pl.BlockSpec((1, ATTN_W), const),
                pl.BlockSpec((256, 256), const)]
    args = [x, mod, lp["norm1_g"], lp["w_in"], lp["b_gate"], lp["q_gain"], lp["k_gain"], lp["seg"]]
    rope = rope_tabs is not None
    if rope:
        in_specs += [pl.BlockSpec((tm, LANES), lambda i: (i % tiles_per_seq, 0))] * 2
        args += list(rope_tabs)
    out_shape = [jax.ShapeDtypeStruct((t, ATTN_W), BF16),
                 jax.ShapeDtypeStruct((t, ATTN_W), kv_dtype),
                 jax.ShapeDtypeStruct((t, ATTN_W), kv_dtype),
                 jax.ShapeDtypeStruct((t, POOL_W), BF16),
                 jax.ShapeDtypeStruct((t, 2 * D_MODEL), BF16)]
    out_specs = [pl.BlockSpec((tm, ATTN_W), row), pl.BlockSpec((tm, ATTN_W), row),
                 pl.BlockSpec((tm, ATTN_W), row), pl.BlockSpec((tm, POOL_W), row),
                 pl.BlockSpec((tm, 2 * D_MODEL), row)]
    return pl.pallas_call(
        functools.partial(_pre_kernel, rope=rope, kv_dtype=kv_dtype),
        grid=(t // tm,),
        in_specs=in_specs, out_specs=out_specs, out_shape=out_shape,
        compiler_params=_cparams("arbitrary"),
        name="pre_mixer_rope" if rope else "pre_mixer",
    )(*args)


def _attn_kernel(*refs, n_q_blocks, tq, has_cache, heads_per_step):
    if has_cache:
        q_ref, k_ref, v_ref, ck_ref, cv_ref, lam_ref, sg_ref, o_ref, k1_s, k2_s, v_s = refs
    else:
        q_ref, k_ref, v_ref, lam_ref, sg_ref, o_ref, k1_s, k2_s, v_s = refs
    lv = lam_ref[...]
    lam = (jnp.exp(jnp.sum(lv[0:1] * lv[1:2], axis=-1, keepdims=True))
           - jnp.exp(jnp.sum(lv[2:3] * lv[3:4], axis=-1, keepdims=True)) + LAMBDA_INIT)
    nt = (((1,), (1,)), ((), ()))
    sg = sg_ref[...] * (1.0 - LAMBDA_INIT)
    n = k_ref.shape[0]

    def stage(hh, rows, kf, vf):
        kf = kf.astype(F32)
        lane = lax.broadcasted_iota(jnp.int32, kf.shape, 1)
        k1_s[hh, rows, :] = jnp.where(lane < HEAD_DIM, kf, 0.0).astype(BF16)
        k2_s[hh, rows, :] = jnp.where(lane >= HEAD_DIM, kf, 0.0).astype(BF16)
        v_s[hh, rows, :HEAD_W] = vf.astype(BF16)

    v_s[:, :, HEAD_W:] = jnp.ones(v_s.shape[:2] + (HEAD_W,), BF16)

    for hh in range(heads_per_step):
        cols = slice(hh * HEAD_W, (hh + 1) * HEAD_W)
        stage(hh, slice(0, n), k_ref[:, cols], v_ref[:, cols])
        if has_cache:
            head = pl.program_id(1) * heads_per_step + hh
            stage(hh, slice(n, k1_s.shape[1]), ck_ref[:, head, :], cv_ref[:, head, :])

        def softmax_av(q, k_s):
            s = lax.dot_general(q, k_s[hh], nt, preferred_element_type=F32)
            e = jnp.exp2(s - jnp.max(s, axis=-1, keepdims=True)).astype(BF16)
            ov = _dot(e, v_s[hh])
            return ov[:, :HEAD_W] / ov[:, HEAD_W:]

        def block(i, carry):
            qs = pl.multiple_of(i * tq, tq)
            q = q_ref[pl.ds(qs, tq), cols]
            o = softmax_av(q, k1_s) - lam * softmax_av(q, k2_s)
            on = o * lax.rsqrt(jnp.mean(o * o, axis=-1, keepdims=True) + EPS) * sg
            o_ref[pl.ds(qs, tq), cols] = on.astype(o_ref.dtype)
            return carry

        if n_q_blocks == 1:
            block(0, 0)
        else:
            lax.fori_loop(0, n_q_blocks, block, 0, unroll=True)


def _attention(q, k, v, cache, lam_rows, subln_g, *, tq, heads_per_step):
    b, n, _ = q.shape
    has_cache = cache is not None
    heads = lambda bi, hi: (bi, 0, hi)
    const = lambda bi, hi: (0, 0)
    in_specs = [pl.BlockSpec((None, n, heads_per_step * HEAD_W), heads)] * 3
    args = [q, k, v]
    if has_cache:
        p_len = cache[0].shape[2]
        in_specs += [pl.BlockSpec((None, None, p_len, N_HEADS, HEAD_W), lambda bi, hi: (bi, 0, 0, 0, 0))] * 2
        args += list(cache)
    in_specs += [pl.BlockSpec((4, HEAD_DIM), const), pl.BlockSpec((1, HEAD_W), const)]
    args += [lam_rows, subln_g]
    n_keys = n + (cache[0].shape[2] if has_cache else 0)
    return pl.pallas_call(
        functools.partial(_attn_kernel, n_q_blocks=n // tq, tq=tq, has_cache=has_cache,
                          heads_per_step=heads_per_step),
        grid=(b, N_HEADS // heads_per_step),
        in_specs=in_specs,
        out_specs=pl.BlockSpec((None, n, heads_per_step * HEAD_W), heads),
        out_shape=jax.ShapeDtypeStruct((b, n, ATTN_W), BF16),
        scratch_shapes=[pltpu.VMEM((heads_per_step, n_keys, HEAD_W), BF16)] * 2
        + [pltpu.VMEM((heads_per_step, n_keys, 2 * HEAD_W), BF16)],
        compiler_params=_cparams("arbitrary", "arbitrary"),
        name="diff_attn_cache" if has_cache else "diff_attn",
    )(*args)


def _route(logits):
    rows = logits.shape[0]
    lane_i = lax.broadcasted_iota(jnp.int32, (rows, LANES), 1)
    valid = lane_i < N_EXPERTS
    lane = lane_i.astype(F32)
    grp = (lane_i // EXPERTS_PER_GROUP).astype(F32)
    e_log = logits[:, :LANES]
    g_log = jnp.where(valid, logits[:, LANES:], NEG_BIG)
    g_max = jnp.max(g_log, axis=-1, keepdims=True)
    g_den = jnp.sum(jnp.exp(g_log - g_max), axis=-1, keepdims=True) * (1.0 / EXPERTS_PER_GROUP)
    g_w = 1.0 / g_den
    g_idx = jnp.min(jnp.where(g_log == g_max, grp, float(N_GROUPS)), axis=-1, keepdims=True)
    e_sel = jnp.where(grp == g_idx, jnp.where(valid, e_log, NEG_BIG), NEG_BIG)
    v1 = jnp.max(e_sel, axis=-1, keepdims=True)
    i1 = jnp.min(jnp.where(e_sel == v1, lane, float(LANES)), axis=-1, keepdims=True)
    e_rest = jnp.where(lane == i1, NEG_BIG, e_sel)
    v2 = jnp.max(e_rest, axis=-1, keepdims=True)
    i2 = jnp.min(jnp.where(e_rest == v2, lane, float(LANES)), axis=-1, keepdims=True)
    t = jnp.exp(v2 - v1)
    w1 = g_w / (1.0 + t)
    w2 = w1 * t
    first_low = i1 < i2
    a = jnp.minimum(i1, i2) - EXPERTS_PER_GROUP * g_idx
    b = jnp.maximum(i1, i2) - EXPERTS_PER_GROUP * g_idx
    pair = a * (7.0 - a) * 0.5 + (b - a - 1.0)
    cls = g_idx * PAIRS_PER_GROUP + pair
    return cls, jnp.where(first_low, w1, w2), jnp.where(first_low, w2, w1)


def _post_kernel(o_ref, p_ref, g_ref, x_ref, mod_ref, wa_ref, pw_ref, ps_ref, wp_ref, wo_ref,
                 g2_ref, wrh_ref, wrl_ref, br_ref, x1_out, h2p_out, gw_out, route_out, counts_out,
                 carry_ref, *, seq, n_sub):
    step = pl.program_id(0)

    @pl.when(step == 0)
    def _():
        carry_ref[...] = jnp.zeros_like(carry_ref)

    tm = POST_ROWS
    tiles_per_seq = max(seq // (n_sub * tm), 1)
    ext = tm + 2 * POOL_HALO
    chains = range(n_sub)
    rows = [slice(s * tm, (s + 1) * tm) for s in chains]
    if seq == tm:
        blk0, t0 = [s * tm for s in chains], [0] * n_sub
    else:
        blk0 = t0 = [pl.multiple_of(((step % tiles_per_seq) * n_sub + s) * tm, tm) for s in chains]

    attn_out = [_dot(o_ref[rows[s], :], wa_ref[...]) for s in chains]

    def pooled(s):
        p_mid = p_ref[pl.ds(blk0[s], tm), :].astype(F32)
        if seq == tm:
            halo_top = halo_bot = jnp.zeros((POOL_HALO, POOL_W), F32)
        else:
            top0 = pl.multiple_of(jnp.maximum(blk0[s] - POOL_HALO, 0), POOL_HALO)
            bot0 = pl.multiple_of(jnp.minimum(blk0[s] + tm, seq - POOL_HALO), POOL_HALO)
            halo_top = p_ref[pl.ds(top0, POOL_HALO), :].astype(F32) * jnp.where(t0[s] > 0, 1.0, 0.0)
            halo_bot = p_ref[pl.ds(bot0, POOL_HALO), :].astype(F32) * jnp.where(t0[s] + tm < seq, 1.0, 0.0)
        p_ext = jnp.concatenate([halo_top, p_mid, halo_bot], axis=0)
        tok1 = t0[s] + lax.broadcasted_iota(jnp.int32, (tm, 1), 0)
        out = []
        for gi, w in enumerate(POOL_WINDOWS):
            half = w // 2
            sl = slice(gi * POOL_GROUP_W, (gi + 1) * POOL_GROUP_W)
            run = p_ext[:, sl]
            k = 1
            while k < w:
                run = run + pltpu.roll(run, ext - k, 0)
                k *= 2
            win = pltpu.roll(run, ext - (POOL_HALO - half), 0)[:tm]
            cnt = (jnp.minimum(tok1 + half, seq) - jnp.maximum(tok1 - half, 0)).astype(F32)
            out.append((win / cnt - p_mid[:, sl]).astype(BF16))
        return out

    pool_in = [pooled(s) for s in chains]
    mixed = [jnp.concatenate([_dot(pool_in[s][gi], pw_ref[gi]) for gi in range(POOL_GROUPS)], axis=1)
             * ps_ref[...] for s in chains]
    pool_out = [_dot(mixed[s].astype(BF16), wp_ref[...]) for s in chains]

    def merge(s):
        g = g_ref[rows[s], :]
        return (g[:, :D_MODEL].astype(F32) * attn_out[s] + g[:, D_MODEL:].astype(F32) * pool_out[s]).astype(BF16)

    merged = [merge(s) for s in chains]
    gate1 = mod_ref[2:3, :]
    x1 = [x_ref[rows[s], :] + gate1 * _dot(merged[s], wo_ref[...]) for s in chains]
    for s in chains:
        x1_out[rows[s], :] = x1[s]

    shift2 = mod_ref[3:4, :]
    scale2 = mod_ref[4:5, :]
    h2 = [x1[s] * lax.rsqrt(jnp.mean(x1[s] * x1[s], axis=-1, keepdims=True) + EPS) * g2_ref[...]
          * (1.0 + scale2) + shift2 for s in chains]
    h2_parts = [_split_bf16(h2[s]) for s in chains]
    for s in chains:
        h2p_out[rows[s]] = h2_parts[s][0].reshape(tm, ROW_SUB, LANES)

    logits = [_dot(h2_parts[s][0], wrh_ref[...]) + _dot(h2_parts[s][1], wrh_ref[...])
              + _dot(h2_parts[s][0], wrl_ref[...]) + br_ref[...] for s in chains]
    routes = [_route(logits[s]) for s in chains]
    lane = lax.broadcasted_iota(jnp.int32, (tm, LANES), 1)
    for s in chains:
        _, w_lo, w_hi = routes[s]
        gw_out[rows[s]] = jnp.where(lane == 0, w_lo, jnp.where(lane == 1, w_hi, 0.0)).reshape(tm, 1, LANES)

    row = lax.broadcasted_iota(jnp.int32, (tm, tm), 0)
    col = lax.broadcasted_iota(jnp.int32, (tm, tm), 1)
    before = jnp.where(col < row, 1.0, 0.0).astype(BF16)
    onehot = [jnp.where(lane.astype(F32) == routes[s][0], 1.0, 0.0) for s in chains]
    within = [_dot(before, onehot[s].astype(BF16)) for s in chains]
    for s in chains:
        rank = jnp.sum(onehot[s] * (within[s] + carry_ref[...]), axis=-1, keepdims=True)
        info = jnp.where(lane == 0, routes[s][0], jnp.where(lane == 1, rank, 0.0))
        route_out[s] = jnp.transpose(info)[:8, :].astype(jnp.int32)
        carry_ref[...] += jnp.sum(onehot[s], axis=0, keepdims=True)
    counts_out[...] = carry_ref[...]


def _post_mixer(o, p, g, x, mod, lp, *, seq, n_sub):
    t = x.shape[0]
    tm = n_sub * POST_ROWS
    tiles_per_seq = max(seq // tm, 1)
    single_mod = mod.shape[0] == 1
    mod_idx = (lambda i: (0, 0, 0)) if single_mod else (lambda i: (i // tiles_per_seq, 0, 0))
    tile = lambda i: (i, 0)
    const2 = lambda i: (0, 0)
    const3 = lambda i: (0, 0, 0)
    p_spec = (pl.BlockSpec((tm, POOL_W), tile) if seq == POST_ROWS
              else pl.BlockSpec((seq, POOL_W), lambda i: (i // tiles_per_seq, 0)))
    in_specs = [pl.BlockSpec((tm, ATTN_W), tile),
                p_spec,
                pl.BlockSpec((tm, 2 * D_MODEL), tile),
                pl.BlockSpec((tm, D_MODEL), tile),
                pl.BlockSpec((None, ADA_CHUNKS, D_MODEL), mod_idx),
                pl.BlockSpec((ATTN_W, D_MODEL), const2),
                pl.BlockSpec((POOL_GROUPS, POOL_GROUP_W, POOL_GROUP_W), const3),
                pl.BlockSpec((1, POOL_W), const2),
                pl.BlockSpec((POOL_W, D_MODEL), const2),
                pl.BlockSpec((D_MODEL, D_MODEL), const2),
                pl.BlockSpec((1, D_MODEL), const2),
                pl.BlockSpec((D_MODEL, ROUTER_W), const2),
                pl.BlockSpec((D_MODEL, ROUTER_W), const2),
                pl.BlockSpec((1, ROUTER_W), const2)]
    tile3 = lambda i: (i, 0, 0)
    out_shape = [jax.ShapeDtypeStruct((t, D_MODEL), F32),
                 jax.ShapeDtypeStruct((t, ROW_SUB, LANES), BF16),
                 jax.ShapeDtypeStruct((t, 1, LANES), F32),
                 jax.ShapeDtypeStruct((t // POST_ROWS, 8, POST_ROWS), jnp.int32),
                 jax.ShapeDtypeStruct((1, LANES), F32)]
    out_specs = [pl.BlockSpec((tm, D_MODEL), tile),
                 pl.BlockSpec((tm, ROW_SUB, LANES), tile3),
                 pl.BlockSpec((tm, 1, LANES), tile3),
                 pl.BlockSpec((n_sub, 8, POST_ROWS), tile3),
                 pl.BlockSpec((1, LANES), const2)]
    return pl.pallas_call(
        functools.partial(_post_kernel, seq=seq, n_sub=n_sub),
        grid=(t // tm,),
        in_specs=in_specs, out_specs=out_specs, out_shape=out_shape,
        scratch_shapes=[pltpu.VMEM((1, LANES), F32)],
        compiler_params=_cparams("arbitrary"),
        name="post_mixer",
    )(o, p, g, x, mod, lp["w_br_attn"], lp["pool_w"], lp["pool_scale"], lp["w_br_pool"], lp["w_out"],
      lp["norm2_g"], lp["w_router_hi"], lp["w_router_lo"], lp["b_router"])


def _moe_tables(counts, n_tokens):
    cnt = counts[0, :N_CLASSES].astype(jnp.int32)
    tiles = (cnt + MOE_TILE - 1) // MOE_TILE
    upto = jnp.arange(N_CLASSES)[:, None] <= jnp.arange(N_CLASSES)[None, :]
    ends = jnp.sum(jnp.where(upto, tiles[:, None], 0), axis=0)
    off = jnp.concatenate([jnp.zeros((1,), jnp.int32), ends]) * MOE_TILE
    n_tiles = ends[-1:]
    max_tiles = n_tokens // MOE_TILE + N_CLASSES
    tile_id = jnp.minimum(jnp.arange(max_tiles), n_tiles - 1)
    tile_cls = jnp.minimum(jnp.sum(ends[None, :] <= tile_id[:, None], axis=1), N_CLASSES - 1)
    group, pair = tile_cls // PAIRS_PER_GROUP, tile_cls % PAIRS_PER_GROUP
    e_lo = group * EXPERTS_PER_GROUP + jnp.asarray(PAIR_LO, jnp.int32)[pair]
    e_hi = group * EXPERTS_PER_GROUP + jnp.asarray(PAIR_HI, jnp.int32)[pair]

    def runs(e):
        steps = jnp.arange(max_tiles)
        change = jnp.concatenate([jnp.zeros((1,), jnp.int32), (e[1:] != e[:-1]).astype(jnp.int32)])
        run = jnp.sum(jnp.where(steps[:, None] <= steps[None, :], change[:, None], 0), axis=0)
        run_end = jnp.sum(run[None, :] <= run[:, None], axis=1)
        nxt = jnp.where(run_end < max_tiles, e[jnp.minimum(run_end, max_tiles - 1)], -1)
        return e.astype(jnp.int32), (run % 2).astype(jnp.int32), nxt.astype(jnp.int32)

    return (off.astype(jnp.int32),) + runs(e_lo) + runs(e_hi) + (n_tiles.astype(jnp.int32),)


def _moe_kernel(cls_s, rank_s, off_s, elo_s, plo_s, nlo_s, ehi_s, phi_s, nhi_s, nt_s, h_ref, gw_ref,
                wg_hbm, wu_hbm, wd_hbm, o_ref, src_s, xg_ref, gwg_ref, wg_buf, wu_buf, wd_buf, sems,
                *, n_tokens):
    j = pl.program_id(0)
    n_tiles = nt_s[0]

    def weight_copies(e, role, slot):
        return [pltpu.make_async_copy(hbm.at[e], buf.at[role, slot], sems.at[role, slot, i])
                for i, (hbm, buf) in enumerate(((wg_hbm, wg_buf), (wu_hbm, wu_buf), (wd_hbm, wd_buf)))]

    roles = ((0, elo_s, plo_s, nlo_s), (1, ehi_s, phi_s, nhi_s))

    def gather_tile(tile, slot):
        base = tile * MOE_TILE
        for r in range(MOE_TILE):
            t = src_s[base + r]
            xg_ref[slot, r] = h_ref[t]
            gwg_ref[slot, r] = gw_ref[t]

    @pl.when(j == 0)
    def _():
        for role, e_s, _, _ in roles:
            for cp in weight_copies(e_s[0], role, 0):
                cp.start()

        def clear_tail(c, carry):
            start = jnp.maximum(off_s[c + 1] - MOE_TILE, 0)
            for i in range(MOE_TILE):
                src_s[start + i] = 0
            return carry

        lax.fori_loop(0, N_CLASSES, clear_tail, 0)

        def place(t, c):
            src_s[off_s[cls_s[t]] + rank_s[t]] = t
            return c

        lax.fori_loop(0, n_tokens, place, 0, unroll=16)
        gather_tile(0, 0)

    @pl.when(j < n_tiles)
    def _():
        for role, e_s, par_s, nxt_s in roles:
            run_starts = (j == 0) | (e_s[j] != e_s[jnp.maximum(j - 1, 0)])

            @pl.when(run_starts)
            def _():
                for cp in weight_copies(e_s[j], role, par_s[j]):
                    cp.wait()

                @pl.when(nxt_s[j] >= 0)
                def _():
                    for cp in weight_copies(nxt_s[j], role, 1 - par_s[j]):
                        cp.start()

        slot = j % 2
        gather_tile(jnp.minimum(j + 1, n_tiles - 1), 1 - slot)
        x = xg_ref[slot].reshape(MOE_TILE, D_MODEL)
        gw = gwg_ref[slot].reshape(MOE_TILE, LANES)
        lo, hi = plo_s[j], phi_s[j]

        a1, a2 = _dot(x, wg_buf[0, lo]), _dot(x, wg_buf[1, hi])
        u1, u2 = _dot(x, wu_buf[0, lo]), _dot(x, wu_buf[1, hi])
        hid1 = (a1 * jax.nn.sigmoid(a1) * u1 * gw[:, 0:1]).astype(BF16)
        hid2 = (a2 * jax.nn.sigmoid(a2) * u2 * gw[:, 1:2]).astype(BF16)
        o = _dot(hid1, wd_buf[0, lo]) + _dot(hid2, wd_buf[1, hi])
        o_ref[...] = o.astype(BF16).reshape(o_ref.shape)

    @pl.when(j >= n_tiles)
    def _():
        o_ref[...] = jnp.zeros_like(o_ref)


def _moe(h2p, gw, cls, rank, tables, lp):
    t, sub, _ = h2p.shape
    max_tiles = tables[1].shape[0]
    whole = lambda j, *_: (0, 0, 0)
    in_hbm = pl.BlockSpec(memory_space=pl.ANY)
    grid_spec = pltpu.PrefetchScalarGridSpec(
        num_scalar_prefetch=2 + len(tables),
        grid=(max_tiles,),
        in_specs=[pl.BlockSpec((t, sub, LANES), whole, pipeline_mode=pl.Buffered(1)),
                  pl.BlockSpec((t, 1, LANES), whole, pipeline_mode=pl.Buffered(1)),
                  in_hbm, in_hbm, in_hbm],
        out_specs=pl.BlockSpec((MOE_TILE, sub, LANES), lambda j, *_: (j, 0, 0)),
        scratch_shapes=[pltpu.SMEM((max_tiles * MOE_TILE,), jnp.int32),
                        pltpu.VMEM((2, MOE_TILE, sub, LANES), BF16),
                        pltpu.VMEM((2, MOE_TILE, 1, LANES), F32),
                        pltpu.VMEM((2, 2, D_MODEL, D_EXPERT), BF16),
                        pltpu.VMEM((2, 2, D_MODEL, D_EXPERT), BF16),
                        pltpu.VMEM((2, 2, D_EXPERT, D_MODEL), BF16),
                        pltpu.SemaphoreType.DMA((2, 2, 3))])
    return pl.pallas_call(
        functools.partial(_moe_kernel, n_tokens=t),
        grid_spec=grid_spec,
        out_shape=jax.ShapeDtypeStruct((max_tiles * MOE_TILE, sub, LANES), BF16),
        compiler_params=_cparams("arbitrary"),
        name="moe",
    )(cls, rank, *tables, h2p, gw, lp["expert_w_gate"], lp["expert_w_up"], lp["expert_w_down"])


def _combine_kernel(cls_s, rank_s, off_s, o_ref, x1_ref, mod_ref, y_ref, og_ref, *, tm):
    base = pl.program_id(0) * tm

    def gather(r, c):
        t = base + r
        og_ref[r] = o_ref[off_s[cls_s[t]] + rank_s[t]]
        return c

    lax.fori_loop(0, tm, gather, 0, unroll=8)
    moe = og_ref[...].reshape(tm, D_MODEL).astype(F32)
    y_ref[...] = x1_ref[...] + mod_ref[5:6, :] * moe


def _combine(o_sorted, x1, mod, cls, rank, off, *, seq, tm):
    t = x1.shape[0]
    rows, sub, _ = o_sorted.shape
    tiles_per_seq = seq // tm
    single_mod = mod.shape[0] == 1
    mod_idx = (lambda i, *_: (0, 0, 0)) if single_mod else (lambda i, *_: (i // tiles_per_seq, 0, 0))
    row = lambda i, *_: (i, 0)
    grid_spec = pltpu.PrefetchScalarGridSpec(
        num_scalar_prefetch=3,
        grid=(t // tm,),
        in_specs=[pl.BlockSpec((rows, sub, LANES), lambda i, *_: (0, 0, 0), pipeline_mode=pl.Buffered(1)),
                  pl.BlockSpec((tm, D_MODEL), row),
                  pl.BlockSpec((None, ADA_CHUNKS, D_MODEL), mod_idx)],
        out_specs=pl.BlockSpec((tm, D_MODEL), row),
        scratch_shapes=[pltpu.VMEM((tm, sub, LANES), BF16)])
    return pl.pallas_call(
        functools.partial(_combine_kernel, tm=tm),
        grid_spec=grid_spec,
        out_shape=jax.ShapeDtypeStruct((t, D_MODEL), F32),
        compiler_params=_cparams("arbitrary"),
        name="moe_combine",
    )(cls, rank, off, o_sorted, x1, mod)


def _rope_tables(n_tokens):
    rows = n_tokens // GRID_W
    row_ids = jnp.repeat(jnp.arange(rows, dtype=F32), GRID_W)
    col_ids = jnp.tile(jnp.arange(GRID_W, dtype=F32), rows)
    inv_freq = jnp.power(ROPE_THETA, -jnp.arange(ROPE_PAIRS, dtype=F32) / ROPE_PAIRS)
    ang_r = row_ids[:, None] * inv_freq[None, :]
    ang_c = col_ids[:, None] * inv_freq[None, :]
    ang = jnp.concatenate([ang_r, ang_r, ang_c, ang_c] * 2, axis=-1)
    return jnp.cos(ang), jnp.sin(ang)


def _layer(x, mod, lp, rope_tabs, cache, lam_rows, *, kv_dtype):
    b, n, _ = x.shape
    t = b * n
    q, k, v, p, g = _pre_mixer(x.reshape(t, D_MODEL), mod, lp, rope_tabs, seq=n, tm=256, kv_dtype=kv_dtype)
    o = _attention(q.reshape(b, n, ATTN_W), k.reshape(b, n, ATTN_W), v.reshape(b, n, ATTN_W),
                   cache, lam_rows, lp["subln_g"], tq=256, heads_per_step=N_HEADS if cache is None else 2)
    x1, h2p, gw, route, counts = _post_mixer(o.reshape(t, ATTN_W), p, g, x.reshape(t, D_MODEL), mod, lp,
                                             seq=n, n_sub=2)
    tables = _moe_tables(counts, t)
    cls, rank = route[:, 0, :].reshape(t), route[:, 1, :].reshape(t)
    o_sorted = _moe(h2p, gw, cls, rank, tables, lp)
    y = _combine(o_sorted, x1, mod, cls, rank, tables[0], seq=n, tm=256)
    return y.reshape(b, n, D_MODEL), k, v


def kernel(x_prompt, x_sample, c, cache_k, cache_v, c_ctx, w_ada, b_ada, norm1_g, w_in, b_gate, q_norm_g, k_norm_g, lambda_q1, lambda_k1, lambda_q2, lambda_k2, subln_g, pool_w, pool_scale, w_br_attn, w_br_pool, w_out, norm2_g, router_group_w, router_group_b, router_expert_w, router_expert_b, expert_w_gate, expert_w_up, expert_w_down):
    b_ctx, n_ctx, _ = x_prompt.shape
    b_lat, n_lat, _ = x_sample.shape

    def router_layout(we, wg):
        pad = jnp.zeros(we.shape[:-1] + (LANES - N_EXPERTS,), F32)
        return jnp.concatenate([we, pad, jnp.repeat(wg, EXPERTS_PER_GROUP, axis=-1), pad], axis=-1)

    w_router = router_layout(router_expert_w[0], router_group_w[0])
    w_router_hi, w_router_lo = _split_bf16(w_router)
    seg = (jnp.arange(256)[:, None] // HEAD_DIM == jnp.arange(256)[None, :] // HEAD_DIM)
    lp = dict(
        norm1_g=norm1_g[0].reshape(1, D_MODEL),
        w_in=w_in[0].astype(BF16),
        b_gate=b_gate[0].reshape(1, 2 * D_MODEL),
        q_gain=jnp.tile(q_norm_g[0], ATTN_W // HEAD_DIM).reshape(1, ATTN_W),
        k_gain=jnp.tile(k_norm_g[0], ATTN_W // HEAD_DIM).reshape(1, ATTN_W),
        seg=(seg.astype(F32) / HEAD_DIM).astype(BF16),
        subln_g=subln_g[0].reshape(1, HEAD_W),
        w_br_attn=w_br_attn[0].astype(BF16),
        pool_w=pool_w[0].astype(BF16),
        pool_scale=pool_scale[0].reshape(1, POOL_W),
        w_br_pool=w_br_pool[0].astype(BF16),
        w_out=w_out[0].astype(BF16),
        norm2_g=norm2_g[0].reshape(1, D_MODEL),
        w_router_hi=w_router_hi, w_router_lo=w_router_lo,
        b_router=router_layout(router_expert_b[0], router_group_b[0]).reshape(1, ROUTER_W),
        expert_w_gate=expert_w_gate[0].astype(BF16),
        expert_w_up=expert_w_up[0].astype(BF16),
        expert_w_down=expert_w_down[0].astype(BF16),
    )
    lam_rows = jnp.stack([lambda_q1[0], lambda_k1[0], lambda_q2[0], lambda_k2[0]], axis=0)

    n_cond = 1 + b_lat
    cond = jnp.concatenate([c_ctx[None, :], c, jnp.zeros((16 - n_cond, D_MODEL), F32)], axis=0)
    mod = _adaln(cond, w_ada[0], b_ada[0])[:n_cond].reshape(n_cond, ADA_CHUNKS, D_MODEL)

    y_prompt, k_ctx, v_ctx = _layer(x_prompt, mod[:1], lp, None, None, lam_rows, kv_dtype=F32)
    cache = (cache_k, cache_v)
    y_sample, _, _ = _layer(x_sample, mod[1:], lp, _rope_tables(n_lat), cache, lam_rows, kv_dtype=BF16)

    new_cache_k = k_ctx.reshape(b_ctx, 1, n_ctx, N_HEADS, HEAD_W)
    new_cache_v = v_ctx.reshape(b_ctx, 1, n_ctx, N_HEADS, HEAD_W)
    return (y_prompt, y_sample, new_cache_k, new_cache_v)
```

```python
import functools
import math

import jax
import jax.numpy as jnp
from jax import lax
from jax.experimental import pallas as pl
from jax.experimental.pallas import tpu as pltpu

D_MODEL = 1024
GRID_W = 64
N_HEADS = 8
HEAD_DIM = 64
HEAD_W = 2 * HEAD_DIM
ATTN_W = N_HEADS * HEAD_W
POOL_GROUPS = 4
POOL_WINDOWS = (2, 4, 8, 16)
POOL_W = 512
POOL_GROUP_W = 128
IN_W = 3 * ATTN_W + POOL_W + 2 * D_MODEL
ROPE_THETA = 10000.0
ROPE_PAIRS = 16
N_GROUPS = 4
EXPERTS_PER_GROUP = 4
N_EXPERTS = 16
PAIRS_PER_GROUP = 6
N_CLASSES = N_GROUPS * PAIRS_PER_GROUP
PAIR_LO = (0, 0, 0, 1, 1, 2)
PAIR_HI = (1, 2, 3, 2, 3, 3)
D_EXPERT = 512
MOE_TILE = 256
MOE_ROWS = 128
POST_ROWS = 256
POOL_HALO = 16
ADA_CHUNKS = 6
EPS = 1e-6
LAMBDA_INIT = 0.8 - 0.6 * math.exp(-0.0)

LANES = 128
ROW_SUB = D_MODEL // LANES
ROUTER_W = 2 * LANES
NEG_BIG = -1e30
Q_SCALE = math.log2(math.e) * HEAD_DIM ** -0.5

F32 = jnp.float32
BF16 = jnp.bfloat16

VMEM_LIMIT = 56 * 1024 * 1024


def _cparams(*sem):
    return pltpu.CompilerParams(dimension_semantics=sem, vmem_limit_bytes=VMEM_LIMIT)


def _split_bf16(x):
    hi = x.astype(BF16)
    lo = (x - hi.astype(F32)).astype(BF16)
    return hi, lo


def _dot(a, b):
    return jnp.dot(a, b, preferred_element_type=F32)


def _adaln_kernel(cond_ref, w_ref, b_ref, o_ref):
    c = cond_ref[...]
    s = c * jax.nn.sigmoid(c)
    s_hi, s_lo = _split_bf16(s)
    w_hi, w_lo = _split_bf16(w_ref[...])
    rows = s.shape[0]
    both = _dot(jnp.concatenate([s_hi, s_lo], axis=0), w_hi)
    o_ref[...] = both[:rows] + both[rows:] + _dot(s_hi, w_lo) + b_ref[...]


def _adaln(cond, w_ada, b_ada):
    rows = cond.shape[0]
    n = w_ada.shape[1]
    tn = 1536
    return pl.pallas_call(
        _adaln_kernel,
        grid=(n // tn,),
        in_specs=[pl.BlockSpec((rows, D_MODEL), lambda j: (0, 0)),
                  pl.BlockSpec((D_MODEL, tn), lambda j: (0, j)),
                  pl.BlockSpec((1, tn), lambda j: (0, j))],
        out_specs=pl.BlockSpec((rows, tn), lambda j: (0, j)),
        out_shape=jax.ShapeDtypeStruct((rows, n), F32),
        compiler_params=_cparams("arbitrary"),
        name="adaln",
    )(cond, w_ada, b_ada.reshape(1, n))


def _chunk_rms(z, seg):
    zz = (z * z).astype(BF16)
    parts = [_dot(zz[:, c * 256:(c + 1) * 256], seg) for c in range(z.shape[1] // 256)]
    return jnp.concatenate(parts, axis=1)


def _rope(x, cos, sin):
    lane = lax.broadcasted_iota(jnp.int32, (x.shape[0], LANES), 1)
    first = (lane % (2 * ROPE_PAIRS)) < ROPE_PAIRS
    parts = []
    for c in range(x.shape[1] // LANES):
        xc = x[:, c * LANES:(c + 1) * LANES]
        up = pltpu.roll(xc, LANES - ROPE_PAIRS, 1)
        dn = pltpu.roll(xc, ROPE_PAIRS, 1)
        parts.append(xc * cos + jnp.where(first, -up, dn) * sin)
    return jnp.concatenate(parts, axis=1)


def _pre_kernel(*refs, rope, kv_dtype):
    if rope:
        (x_ref, mod_ref, g1_ref, w_ref, bg_ref, qg_ref, kg_ref, seg_ref, cos_ref, sin_ref,
         q_out, k_out, v_out, p_out, g_out) = refs
    else:
        (x_ref, mod_ref, g1_ref, w_ref, bg_ref, qg_ref, kg_ref, seg_ref,
         q_out, k_out, v_out, p_out, g_out) = refs
    x = x_ref[...]
    shift = mod_ref[0:1, :]
    scale = mod_ref[1:2, :]
    xn = x * lax.rsqrt(jnp.mean(x * x, axis=-1, keepdims=True) + EPS) * g1_ref[...]
    h = (xn * (1.0 + scale) + shift).astype(BF16)
    seg = seg_ref[...]

    zq = _dot(h, w_ref[:, 0:ATTN_W])
    qn = zq * lax.rsqrt(_chunk_rms(zq, seg) + EPS) * qg_ref[...]
    if rope:
        qn = _rope(qn, cos_ref[...], sin_ref[...])
    q_out[...] = (qn * Q_SCALE).astype(BF16)

    zk = _dot(h, w_ref[:, ATTN_W:2 * ATTN_W])
    kn = zk * lax.rsqrt(_chunk_rms(zk, seg) + EPS) * kg_ref[...]
    if rope:
        kn = _rope(kn, cos_ref[...], sin_ref[...])
    k_out[...] = kn.astype(kv_dtype)

    v_out[...] = _dot(h, w_ref[:, 2 * ATTN_W:3 * ATTN_W]).astype(kv_dtype)
    p_out[...] = _dot(h, w_ref[:, 3 * ATTN_W:3 * ATTN_W + POOL_W]).astype(BF16)
    gl = _dot(h, w_ref[:, 3 * ATTN_W + POOL_W:IN_W]) + bg_ref[...]
    g_out[...] = jax.nn.sigmoid(gl).astype(BF16)


def _pre_mixer(x, mod, lp, rope_tabs, *, seq, tm, kv_dtype):
    t = x.shape[0]
    tiles_per_seq = seq // tm
    single_mod = mod.shape[0] == 1
    mod_idx = (lambda i: (0, 0, 0)) if single_mod else (lambda i: (i // tiles_per_seq, 0, 0))
    const = lambda i: (0, 0)
    row = lambda i: (i, 0)
    in_specs = [pl.BlockSpec((tm, D_MODEL), row),
                pl.BlockSpec((None, ADA_CHUNKS, D_MODEL), mod_idx),
                pl.BlockSpec((1, D_MODEL), const),
                pl.BlockSpec((D_MODEL, IN_W), const, pipeline_mode=pl.Buffered(1)),
                pl.BlockSpec((1, 2 * D_MODEL), const),
                pl.BlockSpec((1, ATTN_W), const),
                pl.BlockSpec((1, ATTN_W), const),
                pl.BlockSpec((256, 256), const)]
    args = [x, mod, lp["norm1_g"], lp["w_in"], lp["b_gate"], lp["q_gain"], lp["k_gain"], lp["seg"]]
    rope = rope_tabs is not None
    if rope:
        in_specs += [pl.BlockSpec((tm, LANES), lambda i: (i % tiles_per_seq, 0))] * 2
        args += list(rope_tabs)
    out_shape = [jax.ShapeDtypeStruct((t, ATTN_W), BF16),
                 jax.ShapeDtypeStruct((t, ATTN_W), kv_dtype),
                 jax.ShapeDtypeStruct((t, ATTN_W), kv_dtype),
                 jax.ShapeDtypeStruct((t, POOL_W), BF16),
                 jax.ShapeDtypeStruct((t, 2 * D_MODEL), BF16)]
    out_specs = [pl.BlockSpec((tm, ATTN_W), row), pl.BlockSpec((tm, ATTN_W), row),
                 pl.BlockSpec((tm, ATTN_W), row), pl.BlockSpec((tm, POOL_W), row),
                 pl.BlockSpec((tm, 2 * D_MODEL), row)]
    return pl.pallas_call(
        functools.partial(_pre_kernel, rope=rope, kv_dtype=kv_dtype),
        grid=(t // tm,),
        in_specs=in_specs, out_specs=out_specs, out_shape=out_shape,
        compiler_params=_cparams("arbitrary"),
        name="pre_mixer_rope" if rope else "pre_mixer",
    )(*args)


def _attn_kernel(*refs, n_q_blocks, tq, has_cache, heads_per_step):
    if has_cache:
        q_ref, k_ref, v_ref, ck_ref, cv_ref, lam_ref, sg_ref, o_ref, k1_s, k2_s, v_s = refs
    else:
        q_ref, k_ref, v_ref, lam_ref, sg_ref, o_ref, k1_s, k2_s, v_s = refs
    lv = lam_ref[...]
    lam = (jnp.exp(jnp.sum(lv[0:1] * lv[1:2], axis=-1, keepdims=True))
           - jnp.exp(jnp.sum(lv[2:3] * lv[3:4], axis=-1, keepdims=True)) + LAMBDA_INIT)
    nt = (((1,), (1,)), ((), ()))
    sg = sg_ref[...] * (1.0 - LAMBDA_INIT)
    n = k_ref.shape[0]

    def stage(hh, rows, kf, vf):
        kf = kf.astype(F32)
        lane = lax.broadcasted_iota(jnp.int32, kf.shape, 1)
        k1_s[hh, rows, :] = jnp.where(lane < HEAD_DIM, kf, 0.0).astype(BF16)
        k2_s[hh, rows, :] = jnp.where(lane >= HEAD_DIM, kf, 0.0).astype(BF16)
        v_s[hh, rows, :HEAD_W] = vf.astype(BF16)

    v_s[:, :, HEAD_W:] = jnp.ones(v_s.shape[:2] + (HEAD_W,), BF16)

    for hh in range(heads_per_step):
        cols = slice(hh * HEAD_W, (hh + 1) * HEAD_W)
        stage(hh, slice(0, n), k_ref[:, cols], v_ref[:, cols])
        if has_cache:
            head = pl.program_id(1) * heads_per_step + hh
            stage(hh, slice(n, k1_s.shape[1]), ck_ref[:, head, :], cv_ref[:, head, :])

        def softmax_av(q, k_s):
            s = lax.dot_general(q, k_s[hh], nt, preferred_element_type=F32)
            e = jnp.exp2(s - jnp.max(s, axis=-1, keepdims=True)).astype(BF16)
            ov = _dot(e, v_s[hh])
            return ov[:, :HEAD_W] / ov[:, HEAD_W:]

        def block(i, carry):
            qs = pl.multiple_of(i * tq, tq)
            q = q_ref[pl.ds(qs, tq), cols]
            o = softmax_av(q, k1_s) - lam * softmax_av(q, k2_s)
            on = o * lax.rsqrt(jnp.mean(o * o, axis=-1, keepdims=True) + EPS) * sg
            o_ref[pl.ds(qs, tq), cols] = on.astype(o_ref.dtype)
            return carry

        if n_q_blocks == 1:
            block(0, 0)
        else:
            lax.fori_loop(0, n_q_blocks, block, 0, unroll=True)


def _attention(q, k, v, cache, lam_rows, subln_g, *, tq, heads_per_step):
    b, n, _ = q.shape
    has_cache = cache is not None
    heads = lambda bi, hi: (bi, 0, hi)
    const = lambda bi, hi: (0, 0)
    in_specs = [pl.BlockSpec((None, n, heads_per_step * HEAD_W), heads)] * 3
    args = [q, k, v]
    if has_cache:
        p_len = cache[0].shape[2]
        in_specs += [pl.BlockSpec((None, None, p_len, N_HEADS, HEAD_W), lambda bi, hi: (bi, 0, 0, 0, 0))] * 2
        args += list(cache)
    in_specs += [pl.BlockSpec((4, HEAD_DIM), const), pl.BlockSpec((1, HEAD_W), const)]
    args += [lam_rows, subln_g]
    n_keys = n + (cache[0].shape[2] if has_cache else 0)
    return pl.pallas_call(
        functools.partial(_attn_kernel, n_q_blocks=n // tq, tq=tq, has_cache=has_cache,
                          heads_per_step=heads_per_step),
        grid=(b, N_HEADS // heads_per_step),
        in_specs=in_specs,
        out_specs=pl.BlockSpec((None, n, heads_per_step * HEAD_W), heads),
        out_shape=jax.ShapeDtypeStruct((b, n, ATTN_W), BF16),
        scratch_shapes=[pltpu.VMEM((heads_per_step, n_keys, HEAD_W), BF16)] * 2
        + [pltpu.VMEM((heads_per_step, n_keys, 2 * HEAD_W), BF16)],
        compiler_params=_cparams("arbitrary", "arbitrary"),
        name="diff_attn_cache" if has_cache else "diff_attn",
    )(*args)


def _route(logits):
    rows = logits.shape[0]
    lane_i = lax.broadcasted_iota(jnp.int32, (rows, LANES), 1)
    valid = lane_i < N_EXPERTS
    lane = lane_i.astype(F32)
    grp = (lane_i // EXPERTS_PER_GROUP).astype(F32)
    e_log = logits[:, :LANES]
    g_log = jnp.where(valid, logits[:, LANES:], NEG_BIG)
    g_max = jnp.max(g_log, axis=-1, keepdims=True)
    g_den = jnp.sum(jnp.exp(g_log - g_max), axis=-1, keepdims=True) * (1.0 / EXPERTS_PER_GROUP)
    g_w = 1.0 / g_den
    g_idx = jnp.min(jnp.where(g_log == g_max, grp, float(N_GROUPS)), axis=-1, keepdims=True)
    e_sel = jnp.where(grp == g_idx, jnp.where(valid, e_log, NEG_BIG), NEG_BIG)
    v1 = jnp.max(e_sel, axis=-1, keepdims=True)
    i1 = jnp.min(jnp.where(e_sel == v1, lane, float(LANES)), axis=-1, keepdims=True)
    e_rest = jnp.where(lane == i1, NEG_BIG, e_sel)
    v2 = jnp.max(e_rest, axis=-1, keepdims=True)
    i2 = jnp.min(jnp.where(e_rest == v2, lane, float(LANES)), axis=-1, keepdims=True)
    t = jnp.exp(v2 - v1)
    w1 = g_w / (1.0 + t)
    w2 = w1 * t
    first_low = i1 < i2
    a = jnp.minimum(i1, i2) - EXPERTS_PER_GROUP * g_idx
    b = jnp.maximum(i1, i2) - EXPERTS_PER_GROUP * g_idx
    pair = a * (7.0 - a) * 0.5 + (b - a - 1.0)
    cls = g_idx * PAIRS_PER_GROUP + pair
    return cls, jnp.where(first_low, w1, w2), jnp.where(first_low, w2, w1)


def _post_kernel(o_ref, p_ref, g_ref, x_ref, mod_ref, wa_ref, pw_ref, ps_ref, wp_ref, wo_ref,
                 g2_ref, wrh_ref, wrl_ref, br_ref, x1_out, h2p_out, gw_out, route_out, counts_out,
                 carry_ref, *, seq, n_sub):
    step = pl.program_id(0)

    @pl.when(step == 0)
    def _():
        carry_ref[...] = jnp.zeros_like(carry_ref)

    tm = POST_ROWS
    tiles_per_seq = max(seq // (n_sub * tm), 1)
    ext = tm + 2 * POOL_HALO
    chains = range(n_sub)
    rows = [slice(s * tm, (s + 1) * tm) for s in chains]
    if seq == tm:
        blk0, t0 = [s * tm for s in chains], [0] * n_sub
    else:
        blk0 = t0 = [pl.multiple_of(((step % tiles_per_seq) * n_sub + s) * tm, tm) for s in chains]

    attn_out = [_dot(o_ref[rows[s], :], wa_ref[...]) for s in chains]

    def pooled(s):
        p_mid = p_ref[pl.ds(blk0[s], tm), :].astype(F32)
        if seq == tm:
            halo_top = halo_bot = jnp.zeros((POOL_HALO, POOL_W), F32)
        else:
            top0 = pl.multiple_of(jnp.maximum(blk0[s] - POOL_HALO, 0), POOL_HALO)
            bot0 = pl.multiple_of(jnp.minimum(blk0[s] + tm, seq - POOL_HALO), POOL_HALO)
            halo_top = p_ref[pl.ds(top0, POOL_HALO), :].astype(F32) * jnp.where(t0[s] > 0, 1.0, 0.0)
            halo_bot = p_ref[pl.ds(bot0, POOL_HALO), :].astype(F32) * jnp.where(t0[s] + tm < seq, 1.0, 0.0)
        p_ext = jnp.concatenate([halo_top, p_mid, halo_bot], axis=0)
        tok1 = t0[s] + lax.broadcasted_iota(jnp.int32, (tm, 1), 0)
        out = []
        for gi, w in enumerate(POOL_WINDOWS):
            half = w // 2
            sl = slice(gi * POOL_GROUP_W, (gi + 1) * POOL_GROUP_W)
            run = p_ext[:, sl]
            k = 1
            while k < w:
                run = run + pltpu.roll(run, ext - k, 0)
                k *= 2
            win = pltpu.roll(run, ext - (POOL_HALO - half), 0)[:tm]
            cnt = (jnp.minimum(tok1 + half, seq) - jnp.maximum(tok1 - half, 0)).astype(F32)
            out.append((win / cnt - p_mid[:, sl]).astype(BF16))
        return out

    pool_in = [pooled(s) for s in chains]
    mixed = [jnp.concatenate([_dot(pool_in[s][gi], pw_ref[gi]) for gi in range(POOL_GROUPS)], axis=1)
             * ps_ref[...] for s in chains]
    pool_out = [_dot(mixed[s].astype(BF16), wp_ref[...]) for s in chains]

    def merge(s):
        g = g_ref[rows[s], :]
        return (g[:, :D_MODEL].astype(F32) * attn_out[s] + g[:, D_MODEL:].astype(F32) * pool_out[s]).astype(BF16)

    merged = [merge(s) for s in chains]
    gate1 = mod_ref[2:3, :]
    x1 = [x_ref[rows[s], :] + gate1 * _dot(merged[s], wo_ref[...]) for s in chains]
    for s in chains:
        x1_out[rows[s], :] = x1[s]

    shift2 = mod_ref[3:4, :]
    scale2 = mod_ref[4:5, :]
    h2 = [x1[s] * lax.rsqrt(jnp.mean(x1[s] * x1[s], axis=-1, keepdims=True) + EPS) * g2_ref[...]
          * (1.0 + scale2) + shift2 for s in chains]
    h2_parts = [_split_bf16(h2[s]) for s in chains]
    for s in chains:
        h2p_out[rows[s]] = h2_parts[s][0].reshape(tm, ROW_SUB, LANES)

    logits = [_dot(h2_parts[s][0], wrh_ref[...]) + _dot(h2_parts[s][1], wrh_ref[...])
              + _dot(h2_parts[s][0], wrl_ref[...]) + br_ref[...] for s in chains]
    routes = [_route(logits[s]) for s in chains]
    lane = lax.broadcasted_iota(jnp.int32, (tm, LANES), 1)
    for s in chains:
        _, w_lo, w_hi = routes[s]
        gw_out[rows[s]] = jnp.where(lane == 0, w_lo, jnp.where(lane == 1, w_hi, 0.0)).reshape(tm, 1, LANES)

    row = lax.broadcasted_iota(jnp.int32, (tm, tm), 0)
    col = lax.broadcasted_iota(jnp.int32, (tm, tm), 1)
    before = jnp.where(col < row, 1.0, 0.0).astype(BF16)
    onehot = [jnp.where(lane.astype(F32) == routes[s][0], 1.0, 0.0) for s in chains]
    within = [_dot(before, onehot[s].astype(BF16)) for s in chains]
    for s in chains:
        rank = jnp.sum(onehot[s] * (within[s] + carry_ref[...]), axis=-1, keepdims=True)
        info = jnp.where(lane == 0, routes[s][0], jnp.where(lane == 1, rank, 0.0))
        route_out[s] = jnp.transpose(info)[:8, :].astype(jnp.int32)
        carry_ref[...] += jnp.sum(onehot[s], axis=0, keepdims=True)
    counts_out[...] = carry_ref[...]


def _post_mixer(o, p, g, x, mod, lp, *, seq, n_sub):
    t = x.shape[0]
    tm = n_sub * POST_ROWS
    tiles_per_seq = max(seq // tm, 1)
    single_mod = mod.shape[0] == 1
    mod_idx = (lambda i: (0, 0, 0)) if single_mod else (lambda i: (i // tiles_per_seq, 0, 0))
    tile = lambda i: (i, 0)
    const2 = lambda i: (0, 0)
    const3 = lambda i: (0, 0, 0)
    p_spec = (pl.BlockSpec((tm, POOL_W), tile) if seq == POST_ROWS
              else pl.BlockSpec((seq, POOL_W), lambda i: (i // tiles_per_seq, 0)))
    in_specs = [pl.BlockSpec((tm, ATTN_W), tile),
                p_spec,
                pl.BlockSpec((tm, 2 * D_MODEL), tile),
                pl.BlockSpec((tm, D_MODEL), tile),
                pl.BlockSpec((None, ADA_CHUNKS, D_MODEL), mod_idx),
                pl.BlockSpec((ATTN_W, D_MODEL), const2),
                pl.BlockSpec((POOL_GROUPS, POOL_GROUP_W, POOL_GROUP_W), const3),
                pl.BlockSpec((1, POOL_W), const2),
                pl.BlockSpec((POOL_W, D_MODEL), const2),
                pl.BlockSpec((D_MODEL, D_MODEL), const2),
                pl.BlockSpec((1, D_MODEL), const2),
                pl.BlockSpec((D_MODEL, ROUTER_W), const2),
                pl.BlockSpec((D_MODEL, ROUTER_W), const2),
                pl.BlockSpec((1, ROUTER_W), const2)]
    tile3 = lambda i: (i, 0, 0)
    out_shape = [jax.ShapeDtypeStruct((t, D_MODEL), F32),
                 jax.ShapeDtypeStruct((t, ROW_SUB, LANES), BF16),
                 jax.ShapeDtypeStruct((t, 1, LANES), F32),
                 jax.ShapeDtypeStruct((t // POST_ROWS, 8, POST_ROWS), jnp.int32),
                 jax.ShapeDtypeStruct((1, LANES), F32)]
    out_specs = [pl.BlockSpec((tm, D_MODEL), tile),
                 pl.BlockSpec((tm, ROW_SUB, LANES), tile3),
                 pl.BlockSpec((tm, 1, LANES), tile3),
                 pl.BlockSpec((n_sub, 8, POST_ROWS), tile3),
                 pl.BlockSpec((1, LANES), const2)]
    return pl.pallas_call(
        functools.partial(_post_kernel, seq=seq, n_sub=n_sub),
        grid=(t // tm,),
        in_specs=in_specs, out_specs=out_specs, out_shape=out_shape,
        scratch_shapes=[pltpu.VMEM((1, LANES), F32)],
        compiler_params=_cparams("arbitrary"),
        name="post_mixer",
    )(o, p, g, x, mod, lp["w_br_attn"], lp["pool_w"], lp["pool_scale"], lp["w_br_pool"], lp["w_out"],
      lp["norm2_g"], lp["w_router_hi"], lp["w_router_lo"], lp["b_router"])


def _moe_tables(counts, n_tokens):
    cnt = counts[0, :N_CLASSES].astype(jnp.int32)
    tiles = (cnt + MOE_TILE - 1) // MOE_TILE
    upto = jnp.arange(N_CLASSES)[:, None] <= jnp.arange(N_CLASSES)[None, :]
    ends = jnp.sum(jnp.where(upto, tiles[:, None], 0), axis=0)
    off = jnp.concatenate([jnp.zeros((1,), jnp.int32), ends]) * MOE_TILE
    n_tiles = ends[-1:]
    max_tiles = n_tokens // MOE_TILE + N_CLASSES
    tile_id = jnp.minimum(jnp.arange(max_tiles), n_tiles - 1)
    tile_cls = jnp.minimum(jnp.sum(ends[None, :] <= tile_id[:, None], axis=1), N_CLASSES - 1)
    group, pair = tile_cls // PAIRS_PER_GROUP, tile_cls % PAIRS_PER_GROUP
    e_lo = group * EXPERTS_PER_GROUP + jnp.asarray(PAIR_LO, jnp.int32)[pair]
    e_hi = group * EXPERTS_PER_GROUP + jnp.asarray(PAIR_HI, jnp.int32)[pair]

    def runs(e):
        steps = jnp.arange(max_tiles)
        change = jnp.concatenate([jnp.zeros((1,), jnp.int32), (e[1:] != e[:-1]).astype(jnp.int32)])
        run = jnp.sum(jnp.where(steps[:, None] <= steps[None, :], change[:, None], 0), axis=0)
        run_end = jnp.sum(run[None, :] <= run[:, None], axis=1)
        nxt = jnp.where(run_end < max_tiles, e[jnp.minimum(run_end, max_tiles - 1)], -1)
        return e.astype(jnp.int32), (run % 2).astype(jnp.int32), nxt.astype(jnp.int32)

    valid = jnp.clip(cnt[tile_cls] - MOE_TILE * (tile_id - (ends - tiles)[tile_cls]), 0, MOE_TILE)
    return ((off.astype(jnp.int32),) + runs(e_lo) + runs(e_hi)
            + (valid.astype(jnp.int32), n_tiles.astype(jnp.int32)))


def _moe_kernel(cls_s, rank_s, off_s, elo_s, plo_s, nlo_s, ehi_s, phi_s, nhi_s, valid_s, nt_s, h_ref, gw_ref,
                wg_hbm, wu_hbm, wd_hbm, o_ref, src_s, xg_ref, gwg_ref, wg_buf, wu_buf, wd_buf, sems,
                *, n_tokens):
    j = pl.program_id(0)
    n_tiles = nt_s[0]

    def weight_copies(e, role, slot):
        return [pltpu.make_async_copy(hbm.at[e], buf.at[role, slot], sems.at[role, slot, i])
                for i, (hbm, buf) in enumerate(((wg_hbm, wg_buf), (wu_hbm, wu_buf), (wd_hbm, wd_buf)))]

    roles = ((0, elo_s, plo_s, nlo_s), (1, ehi_s, phi_s, nhi_s))

    def gather_tile(tile, slot):
        base = tile * MOE_TILE
        for r in range(MOE_TILE):
            t = src_s[base + r]
            xg_ref[slot, r] = h_ref[t]
            gwg_ref[slot, r] = gw_ref[t]

    @pl.when(j == 0)
    def _():
        for role, e_s, _, _ in roles:
            for cp in weight_copies(e_s[0], role, 0):
                cp.start()

        def clear_tail(c, carry):
            start = jnp.maximum(off_s[c + 1] - MOE_TILE, 0)
            for i in range(MOE_TILE):
                src_s[start + i] = 0
            return carry

        lax.fori_loop(0, N_CLASSES, clear_tail, 0)

        def place(t, c):
            src_s[off_s[cls_s[t]] + rank_s[t]] = t
            return c

        lax.fori_loop(0, n_tokens, place, 0, unroll=16)
        gather_tile(0, 0)

    @pl.when(j < n_tiles)
    def _():
        for role, e_s, par_s, nxt_s in roles:
            run_starts = (j == 0) | (e_s[j] != e_s[jnp.maximum(j - 1, 0)])

            @pl.when(run_starts)
            def _():
                for cp in weight_copies(e_s[j], role, par_s[j]):
                    cp.wait()

                @pl.when(nxt_s[j] >= 0)
                def _():
                    for cp in weight_copies(nxt_s[j], role, 1 - par_s[j]):
                        cp.start()

        slot = j % 2
        gather_tile(jnp.minimum(j + 1, n_tiles - 1), 1 - slot)
        lo, hi = plo_s[j], phi_s[j]

        def experts(rows):
            x = xg_ref[slot, :rows].reshape(rows, D_MODEL)
            gw = gwg_ref[slot, :rows].reshape(rows, LANES)
            a1, a2 = _dot(x, wg_buf[0, lo]), _dot(x, wg_buf[1, hi])
            u1, u2 = _dot(x, wu_buf[0, lo]), _dot(x, wu_buf[1, hi])
            hid1 = (a1 * jax.nn.sigmoid(a1) * u1 * gw[:, 0:1]).astype(BF16)
            hid2 = (a2 * jax.nn.sigmoid(a2) * u2 * gw[:, 1:2]).astype(BF16)
            o = _dot(hid1, wd_buf[0, lo]) + _dot(hid2, wd_buf[1, hi])
            o_ref[:rows] = o.astype(BF16).reshape(rows, ROW_SUB, LANES)
            if rows < MOE_TILE:
                o_ref[rows:] = jnp.zeros((MOE_TILE - rows, ROW_SUB, LANES), BF16)

        pieces = (valid_s[j] + MOE_ROWS - 1) // MOE_ROWS
        for n in range(1, MOE_TILE // MOE_ROWS + 1):
            @pl.when(pieces == n)
            def _():
                experts(n * MOE_ROWS)

    @pl.when(j >= n_tiles)
    def _():
        o_ref[...] = jnp.zeros_like(o_ref)


def _moe(h2p, gw, cls, rank, tables, lp):
    t, sub, _ = h2p.shape
    max_tiles = tables[1].shape[0]
    whole = lambda j, *_: (0, 0, 0)
    in_hbm = pl.BlockSpec(memory_space=pl.ANY)
    grid_spec = pltpu.PrefetchScalarGridSpec(
        num_scalar_prefetch=2 + len(tables),
        grid=(max_tiles,),
        in_specs=[pl.BlockSpec((t, sub, LANES), whole, pipeline_mode=pl.Buffered(1)),
                  pl.BlockSpec((t, 1, LANES), whole, pipeline_mode=pl.Buffered(1)),
                  in_hbm, in_hbm, in_hbm],
        out_specs=pl.BlockSpec((MOE_TILE, sub, LANES), lambda j, *_: (j, 0, 0)),
        scratch_shapes=[pltpu.SMEM((max_tiles * MOE_TILE,), jnp.int32),
                        pltpu.VMEM((2, MOE_TILE, sub, LANES), BF16),
                        pltpu.VMEM((2, MOE_TILE, 1, LANES), F32),
                        pltpu.VMEM((2, 2, D_MODEL, D_EXPERT), BF16),
                        pltpu.VMEM((2, 2, D_MODEL, D_EXPERT), BF16),
                        pltpu.VMEM((2, 2, D_EXPERT, D_MODEL), BF16),
                        pltpu.SemaphoreType.DMA((2, 2, 3))])
    return pl.pallas_call(
        functools.partial(_moe_kernel, n_tokens=t),
        grid_spec=grid_spec,
        out_shape=jax.ShapeDtypeStruct((max_tiles * MOE_TILE, sub, LANES), BF16),
        compiler_params=_cparams("arbitrary"),
        name="moe",
    )(cls, rank, *tables, h2p, gw, lp["expert_w_gate"], lp["expert_w_up"], lp["expert_w_down"])


def _combine_kernel(cls_s, rank_s, off_s, o_ref, x1_ref, mod_ref, y_ref, og_ref, *, tm):
    base = pl.program_id(0) * tm

    def gather(r, c):
        t = base + r
        og_ref[r] = o_ref[off_s[cls_s[t]] + rank_s[t]]
        return c

    lax.fori_loop(0, tm, gather, 0, unroll=8)
    moe = og_ref[...].reshape(tm, D_MODEL).astype(F32)
    y_ref[...] = x1_ref[...] + mod_ref[5:6, :] * moe


def _combine(o_sorted, x1, mod, cls, rank, off, *, seq, tm):
    t = x1.shape[0]
    rows, sub, _ = o_sorted.shape
    tiles_per_seq = seq // tm
    single_mod = mod.shape[0] == 1
    mod_idx = (lambda i, *_: (0, 0, 0)) if single_mod else (lambda i, *_: (i // tiles_per_seq, 0, 0))
    row = lambda i, *_: (i, 0)
    grid_spec = pltpu.PrefetchScalarGridSpec(
        num_scalar_prefetch=3,
        grid=(t // tm,),
        in_specs=[pl.BlockSpec((rows, sub, LANES), lambda i, *_: (0, 0, 0), pipeline_mode=pl.Buffered(1)),
                  pl.BlockSpec((tm, D_MODEL), row),
                  pl.BlockSpec((None, ADA_CHUNKS, D_MODEL), mod_idx)],
        out_specs=pl.BlockSpec((tm, D_MODEL), row),
        scratch_shapes=[pltpu.VMEM((tm, sub, LANES), BF16)])
    return pl.pallas_call(
        functools.partial(_combine_kernel, tm=tm),
        grid_spec=grid_spec,
        out_shape=jax.ShapeDtypeStruct((t, D_MODEL), F32),
        compiler_params=_cparams("arbitrary"),
        name="moe_combine",
    )(cls, rank, off, o_sorted, x1, mod)


def _rope_tables(n_tokens):
    rows = n_tokens // GRID_W
    row_ids = jnp.repeat(jnp.arange(rows, dtype=F32), GRID_W)
    col_ids = jnp.tile(jnp.arange(GRID_W, dtype=F32), rows)
    inv_freq = jnp.power(ROPE_THETA, -jnp.arange(ROPE_PAIRS, dtype=F32) / ROPE_PAIRS)
    ang_r = row_ids[:, None] * inv_freq[None, :]
    ang_c = col_ids[:, None] * inv_freq[None, :]
    ang = jnp.concatenate([ang_r, ang_r, ang_c, ang_c] * 2, axis=-1)
    return jnp.cos(ang), jnp.sin(ang)


def _layer(x, mod, lp, rope_tabs, cache, lam_rows, *, kv_dtype):
    b, n, _ = x.shape
    t = b * n
    q, k, v, p, g = _pre_mixer(x.reshape(t, D_MODEL), mod, lp, rope_tabs, seq=n, tm=256, kv_dtype=kv_dtype)
    o = _attention(q.reshape(b, n, ATTN_W), k.reshape(b, n, ATTN_W), v.reshape(b, n, ATTN_W),
                   cache, lam_rows, lp["subln_g"], tq=256, heads_per_step=N_HEADS if cache is None else 2)
    x1, h2p, gw, route, counts = _post_mixer(o.reshape(t, ATTN_W), p, g, x.reshape(t, D_MODEL), mod, lp,
                                             seq=n, n_sub=2)
    tables = _moe_tables(counts, t)
    cls, rank = route[:, 0, :].reshape(t), route[:, 1, :].reshape(t)
    o_sorted = _moe(h2p, gw, cls, rank, tables, lp)
    y = _combine(o_sorted, x1, mod, cls, rank, tables[0], seq=n, tm=256)
    return y.reshape(b, n, D_MODEL), k, v


def kernel(x_prompt, x_sample, c, cache_k, cache_v, c_ctx, w_ada, b_ada, norm1_g, w_in, b_gate, q_norm_g, k_norm_g, lambda_q1, lambda_k1, lambda_q2, lambda_k2, subln_g, pool_w, pool_scale, w_br_attn, w_br_pool, w_out, norm2_g, router_group_w, router_group_b, router_expert_w, router_expert_b, expert_w_gate, expert_w_up, expert_w_down):
    b_ctx, n_ctx, _ = x_prompt.shape
    b_lat, n_lat, _ = x_sample.shape

    def router_layout(we, wg):
        pad = jnp.zeros(we.shape[:-1] + (LANES - N_EXPERTS,), F32)
        return jnp.concatenate([we, pad, jnp.repeat(wg, EXPERTS_PER_GROUP, axis=-1), pad], axis=-1)

    w_router = router_layout(router_expert_w[0], router_group_w[0])
    w_router_hi, w_router_lo = _split_bf16(w_router)
    seg = (jnp.arange(256)[:, None] // HEAD_DIM == jnp.arange(256)[None, :] // HEAD_DIM)
    lp = dict(
        norm1_g=norm1_g[0].reshape(1, D_MODEL),
        w_in=w_in[0].astype(BF16),
        b_gate=b_gate[0].reshape(1, 2 * D_MODEL),
        q_gain=jnp.tile(q_norm_g[0], ATTN_W // HEAD_DIM).reshape(1, ATTN_W),
        k_gain=jnp.tile(k_norm_g[0], ATTN_W // HEAD_DIM).reshape(1, ATTN_W),
        seg=(seg.astype(F32) / HEAD_DIM).astype(BF16),
        subln_g=subln_g[0].reshape(1, HEAD_W),
        w_br_attn=w_br_attn[0].astype(BF16),
        pool_w=pool_w[0].astype(BF16),
        pool_scale=pool_scale[0].reshape(1, POOL_W),
        w_br_pool=w_br_pool[0].astype(BF16),
        w_out=w_out[0].astype(BF16),
        norm2_g=norm2_g[0].reshape(1, D_MODEL),
        w_router_hi=w_router_hi, w_router_lo=w_router_lo,
        b_router=router_layout(router_expert_b[0], router_group_b[0]).reshape(1, ROUTER_W),
        expert_w_gate=expert_w_gate[0].astype(BF16),
        expert_w_up=expert_w_up[0].astype(BF16),
        expert_w_down=expert_w_down[0].astype(BF16),
    )
    lam_rows = jnp.stack([lambda_q1[0], lambda_k1[0], lambda_q2[0], lambda_k2[0]], axis=0)

    n_cond = 1 + b_lat
    cond = jnp.concatenate([c_ctx[None, :], c, jnp.zeros((16 - n_cond, D_MODEL), F32)], axis=0)
    mod = _adaln(cond, w_ada[0], b_ada[0])[:n_cond].reshape(n_cond, ADA_CHUNKS, D_MODEL)

    y_prompt, k_ctx, v_ctx = _layer(x_prompt, mod[:1], lp, None, None, lam_rows, kv_dtype=F32)
    cache = (cache_k, cache_v)
    y_sample, _, _ = _layer(x_sample, mod[1:], lp, _rope_tables(n_lat), cache, lam_rows, kv_dtype=BF16)

    new_cache_k = k_ctx.reshape(b_ctx, 1, n_ctx, N_HEADS, HEAD_W)
    new_cache_v = v_ctx.reshape(b_ctx, 1, n_ctx, N_HEADS, HEAD_W)
    return (y_prompt, y_sample, new_cache_k, new_cache_v)
```

```python
import functools
import math

import jax
import jax.numpy as jnp
from jax import lax
from jax.experimental import pallas as pl
from jax.experimental.pallas import tpu as pltpu

D_MODEL = 1024
GRID_W = 64
N_HEADS = 8
HEAD_DIM = 64
HEAD_W = 2 * HEAD_DIM
ATTN_W = N_HEADS * HEAD_W
POOL_GROUPS = 4
POOL_WINDOWS = (2, 4, 8, 16)
POOL_W = 512
POOL_GROUP_W = 128
IN_W = 3 * ATTN_W + POOL_W + 2 * D_MODEL
ROPE_THETA = 10000.0
ROPE_PAIRS = 16
N_GROUPS = 4
EXPERTS_PER_GROUP = 4
N_EXPERTS = 16
PAIRS_PER_GROUP = 6
N_CLASSES = N_GROUPS * PAIRS_PER_GROUP
PAIR_LO = (0, 0, 0, 1, 1, 2)
PAIR_HI = (1, 2, 3, 2, 3, 3)
D_EXPERT = 512
MOE_TILE = 256
MOE_ROWS = 128
POST_ROWS = 256
POOL_HALO = 16
ADA_CHUNKS = 6
EPS = 1e-6
LAMBDA_INIT = 0.8 - 0.6 * math.exp(-0.0)

LANES = 128
ROW_SUB = D_MODEL // LANES
ROUTER_W = 2 * LANES
NEG_BIG = -1e30
Q_SCALE = math.log2(math.e) * HEAD_DIM ** -0.5

F32 = jnp.float32
BF16 = jnp.bfloat16

VMEM_LIMIT = 56 * 1024 * 1024


def _cparams(*sem):
    return pltpu.CompilerParams(dimension_semantics=sem, vmem_limit_bytes=VMEM_LIMIT)


def _split_bf16(x):
    hi = x.astype(BF16)
    lo = (x - hi.astype(F32)).astype(BF16)
    return hi, lo


def _dot(a, b):
    return jnp.dot(a, b, preferred_element_type=F32)


def _adaln_kernel(cond_ref, w_ref, b_ref, o_ref):
    c = cond_ref[...]
    s = c * jax.nn.sigmoid(c)
    s_hi, s_lo = _split_bf16(s)
    w_hi, w_lo = _split_bf16(w_ref[...])
    rows = s.shape[0]
    both = _dot(jnp.concatenate([s_hi, s_lo], axis=0), w_hi)
    o_ref[...] = both[:rows] + both[rows:] + _dot(s_hi, w_lo) + b_ref[...]


def _adaln(cond, w_ada, b_ada):
    rows = cond.shape[0]
    n = w_ada.shape[1]
    tn = 1536
    return pl.pallas_call(
        _adaln_kernel,
        grid=(n // tn,),
        in_specs=[pl.BlockSpec((rows, D_MODEL), lambda j: (0, 0)),
                  pl.BlockSpec((D_MODEL, tn), lambda j: (0, j)),
                  pl.BlockSpec((1, tn), lambda j: (0, j))],
        out_specs=pl.BlockSpec((rows, tn), lambda j: (0, j)),
        out_shape=jax.ShapeDtypeStruct((rows, n), F32),
        compiler_params=_cparams("arbitrary"),
        name="adaln",
    )(cond, w_ada, b_ada.reshape(1, n))


def _chunk_rms(z, seg):
    zz = (z * z).astype(BF16)
    parts = [_dot(zz[:, c * 256:(c + 1) * 256], seg) for c in range(z.shape[1] // 256)]
    return jnp.concatenate(parts, axis=1)


def _rope(x, cos, sin):
    lane = lax.broadcasted_iota(jnp.int32, (x.shape[0], LANES), 1)
    first = (lane % (2 * ROPE_PAIRS)) < ROPE_PAIRS
    parts = []
    for c in range(x.shape[1] // LANES):
        xc = x[:, c * LANES:(c + 1) * LANES]
        up = pltpu.roll(xc, LANES - ROPE_PAIRS, 1)
        dn = pltpu.roll(xc, ROPE_PAIRS, 1)
        parts.append(xc * cos + jnp.where(first, -up, dn) * sin)
    return jnp.concatenate(parts, axis=1)


def _pre_kernel(*refs, rope, kv_dtype):
    if rope:
        (x_ref, mod_ref, g1_ref, w_ref, bg_ref, qg_ref, kg_ref, seg_ref, cos_ref, sin_ref,
         q_out, k_out, v_out, p_out, g_out) = refs
    else:
        (x_ref, mod_ref, g1_ref, w_ref, bg_ref, qg_ref, kg_ref, seg_ref,
         q_out, k_out, v_out, p_out, g_out) = refs
    x = x_ref[...]
    shift = mod_ref[0:1, :]
    scale = mod_ref[1:2, :]
    xn = x * lax.rsqrt(jnp.mean(x * x, axis=-1, keepdims=True) + EPS) * g1_ref[...]
    h = (xn * (1.0 + scale) + shift).astype(BF16)
    seg = seg_ref[...]

    zq = _dot(h, w_ref[:, 0:ATTN_W])
    qn = zq * lax.rsqrt(_chunk_rms(zq, seg) + EPS) * qg_ref[...]
    if rope:
        qn = _rope(qn, cos_ref[...], sin_ref[...])
    q_out[...] = (qn * Q_SCALE).astype(BF16)

    zk = _dot(h, w_ref[:, ATTN_W:2 * ATTN_W])
    kn = zk * lax.rsqrt(_chunk_rms(zk, seg) + EPS) * kg_ref[...]
    if rope:
        kn = _rope(kn, cos_ref[...], sin_ref[...])
    k_out[...] = kn.astype(kv_dtype)

    v_out[...] = _dot(h, w_ref[:, 2 * ATTN_W:3 * ATTN_W]).astype(kv_dtype)
    p_out[...] = _dot(h, w_ref[:, 3 * ATTN_W:3 * ATTN_W + POOL_W]).astype(BF16)
    gl = _dot(h, w_ref[:, 3 * ATTN_W + POOL_W:IN_W]) + bg_ref[...]
    g_out[...] = jax.nn.sigmoid(gl).astype(BF16)


def _pre_mixer(x, mod, lp, rope_tabs, *, seq, tm, kv_dtype):
    t = x.shape[0]
    tiles_per_seq = seq // tm
    single_mod = mod.shape[0] == 1
    mod_idx = (lambda i: (0, 0, 0)) if single_mod else (lambda i: (i // tiles_per_seq, 0, 0))
    const = lambda i: (0, 0)
    row = lambda i: (i, 0)
    in_specs = [pl.BlockSpec((tm, D_MODEL), row),
                pl.BlockSpec((None, ADA_CHUNKS, D_MODEL), mod_idx),
                pl.BlockSpec((1, D_MODEL), const),
                pl.BlockSpec((D_MODEL, IN_W), const, pipeline_mode=pl.Buffered(1)),
                pl.BlockSpec((1, 2 * D_MODEL), const),
                pl.BlockSpec((1, ATTN_W), const),
                pl.BlockSpec((1, ATTN_W), const),
                pl.BlockSpec((256, 256), const)]
    args = [x, mod, lp["norm1_g"], lp["w_in"], lp["b_gate"], lp["q_gain"], lp["k_gain"], lp["seg"]]
    rope = rope_tabs is not None
    if rope:
        in_specs += [pl.BlockSpec((tm, LANES), lambda i: (i % tiles_per_seq, 0))] * 2
        args += list(rope_tabs)
    out_shape = [jax.ShapeDtypeStruct((t, ATTN_W), BF16),
                 jax.ShapeDtypeStruct((t, ATTN_W), kv_dtype),
                 jax.ShapeDtypeStruct((t, ATTN_W), kv_dtype),
                 jax.ShapeDtypeStruct((t, POOL_W), BF16),
                 jax.ShapeDtypeStruct((t, 2 * D_MODEL), BF16)]
    out_specs = [pl.BlockSpec((tm, ATTN_W), row), pl.BlockSpec((tm, ATTN_W), row),
                 pl.BlockSpec((tm, ATTN_W), row), pl.BlockSpec((tm, POOL_W), row),
                 pl.BlockSpec((tm, 2 * D_MODEL), row)]
    return pl.pallas_call(
        functools.partial(_pre_kernel, rope=rope, kv_dtype=kv_dtype),
        grid=(t // tm,),
        in_specs=in_specs, out_specs=out_specs, out_shape=out_shape,
        compiler_params=_cparams("arbitrary"),
        name="pre_mixer_rope" if rope else "pre_mixer",
    )(*args)


def _attn_kernel(*refs, n_q_blocks, tq, has_cache, heads_per_step):
    if has_cache:
        q_ref, k_ref, v_ref, ck_ref, cv_ref, lam_ref, sg_ref, o_ref, k1_s, k2_s, v_s = refs
    else:
        q_ref, k_ref, v_ref, lam_ref, sg_ref, o_ref, k1_s, k2_s, v_s = refs
    lv = lam_ref[...]
    lam = (jnp.exp(jnp.sum(lv[0:1] * lv[1:2], axis=-1, keepdims=True))
           - jnp.exp(jnp.sum(lv[2:3] * lv[3:4], axis=-1, keepdims=True)) + LAMBDA_INIT)
    nt = (((1,), (1,)), ((), ()))
    sg = sg_ref[...] * (1.0 - LAMBDA_INIT)
    n = k_ref.shape[0]

    def stage(hh, rows, kf, vf):
        kf = kf.astype(F32)
        lane = lax.broadcasted_iota(jnp.int32, kf.shape, 1)
        k1_s[hh, rows, :] = jnp.where(lane < HEAD_DIM, kf, 0.0).astype(BF16)
        k2_s[hh, rows, :] = jnp.where(lane >= HEAD_DIM, kf, 0.0).astype(BF16)
        v_s[hh, rows, :HEAD_W] = vf.astype(BF16)

    v_s[:, :, HEAD_W:] = jnp.ones(v_s.shape[:2] + (HEAD_W,), BF16)

    for hh in range(heads_per_step):
        cols = slice(hh * HEAD_W, (hh + 1) * HEAD_W)
        stage(hh, slice(0, n), k_ref[:, cols], v_ref[:, cols])
        if has_cache:
            head = pl.program_id(1) * heads_per_step + hh
            stage(hh, slice(n, k1_s.shape[1]), ck_ref[:, head, :], cv_ref[:, head, :])

        def softmax_av(q, k_s):
            s = lax.dot_general(q, k_s[hh], nt, preferred_element_type=F32)
            e = jnp.exp2(s - jnp.max(s, axis=-1, keepdims=True)).astype(BF16)
            ov = _dot(e, v_s[hh])
            return ov[:, :HEAD_W] / ov[:, HEAD_W:]

        def block(i, carry):
            qs = pl.multiple_of(i * tq, tq)
            q = q_ref[pl.ds(qs, tq), cols]
            o = softmax_av(q, k1_s) - lam * softmax_av(q, k2_s)
            on = o * lax.rsqrt(jnp.mean(o * o, axis=-1, keepdims=True) + EPS) * sg
            o_ref[pl.ds(qs, tq), cols] = on.astype(o_ref.dtype)
            return carry

        if n_q_blocks == 1:
            block(0, 0)
        else:
            lax.fori_loop(0, n_q_blocks, block, 0, unroll=True)


def _attention(q, k, v, cache, lam_rows, subln_g, *, tq, heads_per_step):
    b, n, _ = q.shape
    has_cache = cache is not None
    heads = lambda bi, hi: (bi, 0, hi)
    const = lambda bi, hi: (0, 0)
    in_specs = [pl.BlockSpec((None, n, heads_per_step * HEAD_W), heads)] * 3
    args = [q, k, v]
    if has_cache:
        p_len = cache[0].shape[2]
        in_specs += [pl.BlockSpec((None, None, p_len, N_HEADS, HEAD_W), lambda bi, hi: (bi, 0, 0, 0, 0))] * 2
        args += list(cache)
    in_specs += [pl.BlockSpec((4, HEAD_DIM), const), pl.BlockSpec((1, HEAD_W), const)]
    args += [lam_rows, subln_g]
    n_keys = n + (cache[0].shape[2] if has_cache else 0)
    return pl.pallas_call(
        functools.partial(_attn_kernel, n_q_blocks=n // tq, tq=tq, has_cache=has_cache,
                          heads_per_step=heads_per_step),
        grid=(b, N_HEADS // heads_per_step),
        in_specs=in_specs,
        out_specs=pl.BlockSpec((None, n, heads_per_step * HEAD_W), heads),
        out_shape=jax.ShapeDtypeStruct((b, n, ATTN_W), BF16),
        scratch_shapes=[pltpu.VMEM((heads_per_step, n_keys, HEAD_W), BF16)] * 2
        + [pltpu.VMEM((heads_per_step, n_keys, 2 * HEAD_W), BF16)],
        compiler_params=_cparams("arbitrary", "arbitrary"),
        name="diff_attn_cache" if has_cache else "diff_attn",
    )(*args)


def _route(logits):
    rows = logits.shape[0]
    lane_i = lax.broadcasted_iota(jnp.int32, (rows, LANES), 1)
    valid = lane_i < N_EXPERTS
    lane = lane_i.astype(F32)
    grp = (lane_i // EXPERTS_PER_GROUP).astype(F32)
    e_log = logits[:, :LANES]
    g_log = jnp.where(valid, logits[:, LANES:], NEG_BIG)
    g_max = jnp.max(g_log, axis=-1, keepdims=True)
    g_den = jnp.sum(jnp.exp(g_log - g_max), axis=-1, keepdims=True) * (1.0 / EXPERTS_PER_GROUP)
    g_w = 1.0 / g_den
    g_idx = jnp.min(jnp.where(g_log == g_max, grp, float(N_GROUPS)), axis=-1, keepdims=True)
    e_sel = jnp.where(grp == g_idx, jnp.where(valid, e_log, NEG_BIG), NEG_BIG)
    v1 = jnp.max(e_sel, axis=-1, keepdims=True)
    i1 = jnp.min(jnp.where(e_sel == v1, lane, float(LANES)), axis=-1, keepdims=True)
    e_rest = jnp.where(lane == i1, NEG_BIG, e_sel)
    v2 = jnp.max(e_rest, axis=-1, keepdims=True)
    i2 = jnp.min(jnp.where(e_rest == v2, lane, float(LANES)), axis=-1, keepdims=True)
    t = jnp.exp(v2 - v1)
    w1 = g_w / (1.0 + t)
    w2 = w1 * t
    first_low = i1 < i2
    a = jnp.minimum(i1, i2) - EXPERTS_PER_GROUP * g_idx
    b = jnp.maximum(i1, i2) - EXPERTS_PER_GROUP * g_idx
    pair = a * (7.0 - a) * 0.5 + (b - a - 1.0)
    cls = g_idx * PAIRS_PER_GROUP + pair
    return cls, jnp.where(first_low, w1, w2), jnp.where(first_low, w2, w1)


def _post_kernel(o_ref, p_ref, g_ref, x_ref, mod_ref, wa_ref, pw_ref, ps_ref, wp_ref, wo_ref,
                 g2_ref, wrh_ref, wrl_ref, br_ref, x1_out, h2p_out, gw_out, route_out, counts_out,
                 carry_ref, *, seq, n_sub):
    step = pl.program_id(0)

    @pl.when(step == 0)
    def _():
        carry_ref[...] = jnp.zeros_like(carry_ref)

    tm = POST_ROWS
    tiles_per_seq = max(seq // (n_sub * tm), 1)
    ext = tm + 2 * POOL_HALO
    chains = range(n_sub)
    rows = [slice(s * tm, (s + 1) * tm) for s in chains]
    if seq == tm:
        blk0, t0 = [s * tm for s in chains], [0] * n_sub
    else:
        blk0 = t0 = [pl.multiple_of(((step % tiles_per_seq) * n_sub + s) * tm, tm) for s in chains]

    attn_out = [_dot(o_ref[rows[s], :], wa_ref[...]) for s in chains]

    def pooled(s):
        p_mid = p_ref[pl.ds(blk0[s], tm), :].astype(F32)
        if seq == tm:
            halo_top = halo_bot = jnp.zeros((POOL_HALO, POOL_W), F32)
        else:
            top0 = pl.multiple_of(jnp.maximum(blk0[s] - POOL_HALO, 0), POOL_HALO)
            bot0 = pl.multiple_of(jnp.minimum(blk0[s] + tm, seq - POOL_HALO), POOL_HALO)
            halo_top = p_ref[pl.ds(top0, POOL_HALO), :].astype(F32) * jnp.where(t0[s] > 0, 1.0, 0.0)
            halo_bot = p_ref[pl.ds(bot0, POOL_HALO), :].astype(F32) * jnp.where(t0[s] + tm < seq, 1.0, 0.0)
        p_ext = jnp.concatenate([halo_top, p_mid, halo_bot], axis=0)
        tok1 = t0[s] + lax.broadcasted_iota(jnp.int32, (tm, 1), 0)
        out = []
        for gi, w in enumerate(POOL_WINDOWS):
            half = w // 2
            sl = slice(gi * POOL_GROUP_W, (gi + 1) * POOL_GROUP_W)
            run = p_ext[:, sl]
            k = 1
            while k < w:
                run = run + pltpu.roll(run, ext - k, 0)
                k *= 2
            win = pltpu.roll(run, ext - (POOL_HALO - half), 0)[:tm]
            cnt = (jnp.minimum(tok1 + half, seq) - jnp.maximum(tok1 - half, 0)).astype(F32)
            out.append((win / cnt - p_mid[:, sl]).astype(BF16))
        return out

    pool_in = [pooled(s) for s in chains]
    mixed = [jnp.concatenate([_dot(pool_in[s][gi], pw_ref[gi]) for gi in range(POOL_GROUPS)], axis=1)
             * ps_ref[...] for s in chains]
    pool_out = [_dot(mixed[s].astype(BF16), wp_ref[...]) for s in chains]

    def merge(s):
        g = g_ref[rows[s], :]
        return (g[:, :D_MODEL].astype(F32) * attn_out[s] + g[:, D_MODEL:].astype(F32) * pool_out[s]).astype(BF16)

    merged = [merge(s) for s in chains]
    gate1 = mod_ref[2:3, :]
    x1 = [x_ref[rows[s], :] + gate1 * _dot(merged[s], wo_ref[...]) for s in chains]
    for s in chains:
        x1_out[rows[s], :] = x1[s]

    shift2 = mod_ref[3:4, :]
    scale2 = mod_ref[4:5, :]
    h2 = [x1[s] * lax.rsqrt(jnp.mean(x1[s] * x1[s], axis=-1, keepdims=True) + EPS) * g2_ref[...]
          * (1.0 + scale2) + shift2 for s in chains]
    h2_parts = [_split_bf16(h2[s]) for s in chains]
    for s in chains:
        h2p_out[rows[s]] = h2_parts[s][0].reshape(tm, ROW_SUB, LANES)

    logits = [_dot(h2_parts[s][0], wrh_ref[...]) + _dot(h2_parts[s][1], wrh_ref[...])
              + _dot(h2_parts[s][0], wrl_ref[...]) + br_ref[...] for s in chains]
    routes = [_route(logits[s]) for s in chains]
    lane = lax.broadcasted_iota(jnp.int32, (tm, LANES), 1)
    for s in chains:
        _, w_lo, w_hi = routes[s]
        gw_out[rows[s]] = jnp.where(lane == 0, w_lo, jnp.where(lane == 1, w_hi, 0.0)).reshape(tm, 1, LANES)

    row = lax.broadcasted_iota(jnp.int32, (tm, tm), 0)
    col = lax.broadcasted_iota(jnp.int32, (tm, tm), 1)
    before = jnp.where(col < row, 1.0, 0.0).astype(BF16)
    onehot = [jnp.where(lane.astype(F32) == routes[s][0], 1.0, 0.0) for s in chains]
    within = [_dot(before, onehot[s].astype(BF16)) for s in chains]
    for s in chains:
        rank = jnp.sum(onehot[s] * (within[s] + carry_ref[...]), axis=-1, keepdims=True)
        info = jnp.where(lane == 0, routes[s][0], jnp.where(lane == 1, rank, 0.0))
        route_out[s] = jnp.transpose(info)[:8, :].astype(jnp.int32)
        carry_ref[...] += jnp.sum(onehot[s], axis=0, keepdims=True)
    counts_out[...] = carry_ref[...]


def _post_mixer(o, p, g, x, mod, lp, *, seq, n_sub):
    t = x.shape[0]
    tm = n_sub * POST_ROWS
    tiles_per_seq = max(seq // tm, 1)
    single_mod = mod.shape[0] == 1
    mod_idx = (lambda i: (0, 0, 0)) if single_mod else (lambda i: (i // tiles_per_seq, 0, 0))
    tile = lambda i: (i, 0)
    const2 = lambda i: (0, 0)
    const3 = lambda i: (0, 0, 0)
    p_spec = (pl.BlockSpec((tm, POOL_W), tile) if seq == POST_ROWS
              else pl.BlockSpec((seq, POOL_W), lambda i: (i // tiles_per_seq, 0)))
    in_specs = [pl.BlockSpec((tm, ATTN_W), tile),
                p_spec,
                pl.BlockSpec((tm, 2 * D_MODEL), tile),
                pl.BlockSpec((tm, D_MODEL), tile),
                pl.BlockSpec((None, ADA_CHUNKS, D_MODEL), mod_idx),
                pl.BlockSpec((ATTN_W, D_MODEL), const2),
                pl.BlockSpec((POOL_GROUPS, POOL_GROUP_W, POOL_GROUP_W), const3),
                pl.BlockSpec((1, POOL_W), const2),
                pl.BlockSpec((POOL_W, D_MODEL), const2),
                pl.BlockSpec((D_MODEL, D_MODEL), const2),
                pl.BlockSpec((1, D_MODEL), const2),
                pl.BlockSpec((D_MODEL, ROUTER_W), const2),
                pl.BlockSpec((D_MODEL, ROUTER_W), const2),
                pl.BlockSpec((1, ROUTER_W), const2)]
    tile3 = lambda i: (i, 0, 0)
    out_shape = [jax.ShapeDtypeStruct((t, D_MODEL), F32),
                 jax.ShapeDtypeStruct((t, ROW_SUB, LANES), BF16),
                 jax.ShapeDtypeStruct((t, 1, LANES), F32),
                 jax.ShapeDtypeStruct((t // POST_ROWS, 8, POST_ROWS), jnp.int32),
                 jax.ShapeDtypeStruct((1, LANES), F32)]
    out_specs = [pl.BlockSpec((tm, D_MODEL), tile),
                 pl.BlockSpec((tm, ROW_SUB, LANES), tile3),
                 pl.BlockSpec((tm, 1, LANES), tile3),
                 pl.BlockSpec((n_sub, 8, POST_ROWS), tile3),
                 pl.BlockSpec((1, LANES), const2)]
    return pl.pallas_call(
        functools.partial(_post_kernel, seq=seq, n_sub=n_sub),
        grid=(t // tm,),
        in_specs=in_specs, out_specs=out_specs, out_shape=out_shape,
        scratch_shapes=[pltpu.VMEM((1, LANES), F32)],
        compiler_params=_cparams("arbitrary"),
        name="post_mixer",
    )(o, p, g, x, mod, lp["w_br_attn"], lp["pool_w"], lp["pool_scale"], lp["w_br_pool"], lp["w_out"],
      lp["norm2_g"], lp["w_router_hi"], lp["w_router_lo"], lp["b_router"])


def _moe_kernel(cls_s, rank_s, cnt_s, h_hbm, gw_hbm, wg_hbm, wu_hbm, wd_hbm, o_ref, dest_s,
                src_s, off_s, elo_s, plo_s, nlo_s, ehi_s, phi_s, nhi_s, valid_s, nt_s,
                h_ref, gw_ref, xg_ref, gwg_ref, wg_buf, wu_buf, wd_buf, sems, in_sems, *, n_tokens):
    j = pl.program_id(0)

    def weight_copies(e, role, slot):
        return [pltpu.make_async_copy(hbm.at[e], buf.at[role, slot], sems.at[role, slot, i])
                for i, (hbm, buf) in enumerate(((wg_hbm, wg_buf), (wu_hbm, wu_buf), (wd_hbm, wd_buf)))]

    token_copies = [pltpu.make_async_copy(h_hbm, h_ref, in_sems.at[0]),
                    pltpu.make_async_copy(gw_hbm, gw_ref, in_sems.at[1])]
    roles = ((0, elo_s, plo_s, nlo_s), (1, ehi_s, phi_s, nhi_s))

    def gather_tile(tile, slot):
        base = tile * MOE_TILE
        for r in range(MOE_TILE):
            t = src_s[base + r]
            xg_ref[slot, r] = h_ref[t]
            gwg_ref[slot, r] = gw_ref[t]

    @pl.when(j == 0)
    def _():
        for cp in token_copies:
            cp.start()

        tile = 0
        for c in range(N_CLASSES):
            n_tok = cnt_s[c]
            n_cls_tiles = (n_tok + MOE_TILE - 1) // MOE_TILE
            off_s[c] = tile * MOE_TILE
            e_lo = (c // PAIRS_PER_GROUP) * EXPERTS_PER_GROUP + PAIR_LO[c % PAIRS_PER_GROUP]
            e_hi = (c // PAIRS_PER_GROUP) * EXPERTS_PER_GROUP + PAIR_HI[c % PAIRS_PER_GROUP]

            def fill(k, carry, tile=tile, n_tok=n_tok, e_lo=e_lo, e_hi=e_hi):
                elo_s[tile + k] = e_lo
                ehi_s[tile + k] = e_hi
                valid_s[tile + k] = jnp.minimum(n_tok - k * MOE_TILE, MOE_TILE)
                return carry

            lax.fori_loop(0, n_cls_tiles, fill, 0)
            tile = tile + n_cls_tiles
        off_s[N_CLASSES] = tile * MOE_TILE
        nt_s[0] = tile
        n_tiles = tile

        for role, e_s, par_s, nxt_s in roles:
            for cp in weight_copies(e_s[0], role, 0):
                cp.start()

            def forward(t, par, e_s=e_s, par_s=par_s):
                par = jnp.where((t > 0) & (e_s[t] != e_s[jnp.maximum(t - 1, 0)]), 1 - par, par)
                par_s[t] = par
                return par

            lax.fori_loop(0, n_tiles, forward, 0)

            def backward(i, nxt, e_s=e_s, nxt_s=nxt_s):
                t = n_tiles - 1 - i
                after = jnp.minimum(t + 1, n_tiles - 1)
                nxt = jnp.where((t + 1 < n_tiles) & (e_s[after] != e_s[t]), e_s[after], nxt)
                nxt_s[t] = nxt
                return nxt

            lax.fori_loop(0, n_tiles, backward, -1)

        def clear_tail(c, carry):
            start = jnp.maximum(off_s[c + 1] - MOE_TILE, 0)
            for i in range(MOE_TILE):
                src_s[start + i] = 0
            return carry

        lax.fori_loop(0, N_CLASSES, clear_tail, 0)

        def place(t, c):
            row = off_s[cls_s[t]] + rank_s[t]
            src_s[row] = t
            dest_s[t] = row
            return c

        lax.fori_loop(0, n_tokens, place, 0, unroll=16)
        for cp in token_copies:
            cp.wait()
        gather_tile(0, 0)

    n_tiles = nt_s[0]

    @pl.when(j < n_tiles)
    def _():
        for role, e_s, par_s, nxt_s in roles:
            run_starts = (j == 0) | (e_s[j] != e_s[jnp.maximum(j - 1, 0)])

            @pl.when(run_starts)
            def _():
                for cp in weight_copies(e_s[j], role, par_s[j]):
                    cp.wait()

                @pl.when(nxt_s[j] >= 0)
                def _():
                    for cp in weight_copies(nxt_s[j], role, 1 - par_s[j]):
                        cp.start()

        slot = j % 2
        gather_tile(jnp.minimum(j + 1, n_tiles - 1), 1 - slot)
        lo, hi = plo_s[j], phi_s[j]

        def experts(rows):
            x = xg_ref[slot, :rows].reshape(rows, D_MODEL)
            gw = gwg_ref[slot, :rows].reshape(rows, LANES)
            a1, a2 = _dot(x, wg_buf[0, lo]), _dot(x, wg_buf[1, hi])
            u1, u2 = _dot(x, wu_buf[0, lo]), _dot(x, wu_buf[1, hi])
            hid1 = (a1 * jax.nn.sigmoid(a1) * u1 * gw[:, 0:1]).astype(BF16)
            hid2 = (a2 * jax.nn.sigmoid(a2) * u2 * gw[:, 1:2]).astype(BF16)
            o = _dot(hid1, wd_buf[0, lo]) + _dot(hid2, wd_buf[1, hi])
            o_ref[:rows] = o.astype(BF16).reshape(rows, ROW_SUB, LANES)
            if rows < MOE_TILE:
                o_ref[rows:] = jnp.zeros((MOE_TILE - rows, ROW_SUB, LANES), BF16)

        pieces = (valid_s[j] + MOE_ROWS - 1) // MOE_ROWS
        for n in range(1, MOE_TILE // MOE_ROWS + 1):
            @pl.when(pieces == n)
            def _():
                experts(n * MOE_ROWS)

    @pl.when(j >= n_tiles)
    def _():
        o_ref[...] = jnp.zeros_like(o_ref)


def _moe(h2p, gw, cls, rank, cnt, lp):
    t, sub, _ = h2p.shape
    max_tiles = t // MOE_TILE + N_CLASSES
    in_hbm = pl.BlockSpec(memory_space=pl.ANY)
    tile_table = pltpu.SMEM((max_tiles,), jnp.int32)
    grid_spec = pltpu.PrefetchScalarGridSpec(
        num_scalar_prefetch=3,
        grid=(max_tiles,),
        in_specs=[in_hbm] * 5,
        out_specs=[pl.BlockSpec((MOE_TILE, sub, LANES), lambda j, *_: (j, 0, 0)),
                   pl.BlockSpec(memory_space=pltpu.SMEM)],
        scratch_shapes=[pltpu.SMEM((max_tiles * MOE_TILE,), jnp.int32),
                        pltpu.SMEM((N_CLASSES + 1,), jnp.int32)]
        + [tile_table] * 7
        + [pltpu.SMEM((1,), jnp.int32),
           pltpu.VMEM((t, sub, LANES), BF16),
           pltpu.VMEM((t, 1, LANES), F32),
           pltpu.VMEM((2, MOE_TILE, sub, LANES), BF16),
           pltpu.VMEM((2, MOE_TILE, 1, LANES), F32),
           pltpu.VMEM((2, 2, D_MODEL, D_EXPERT), BF16),
           pltpu.VMEM((2, 2, D_MODEL, D_EXPERT), BF16),
           pltpu.VMEM((2, 2, D_EXPERT, D_MODEL), BF16),
           pltpu.SemaphoreType.DMA((2, 2, 3)),
           pltpu.SemaphoreType.DMA((2,))])
    return pl.pallas_call(
        functools.partial(_moe_kernel, n_tokens=t),
        grid_spec=grid_spec,
        out_shape=[jax.ShapeDtypeStruct((max_tiles * MOE_TILE, sub, LANES), BF16),
                   jax.ShapeDtypeStruct((t,), jnp.int32)],
        compiler_params=_cparams("arbitrary"),
        name="moe",
    )(cls, rank, cnt, h2p, gw, lp["expert_w_gate"], lp["expert_w_up"], lp["expert_w_down"])


def _combine_kernel(dest_s, o_ref, x1_ref, mod_ref, y_ref, og_ref, *, tm):
    base = pl.program_id(0) * tm

    def gather(r, c):
        og_ref[r] = o_ref[dest_s[base + r]]
        return c

    lax.fori_loop(0, tm, gather, 0, unroll=8)
    moe = og_ref[...].reshape(tm, D_MODEL).astype(F32)
    y_ref[...] = x1_ref[...] + mod_ref[5:6, :] * moe


def _combine(o_sorted, x1, mod, dest, *, seq, tm):
    t = x1.shape[0]
    rows, sub, _ = o_sorted.shape
    tiles_per_seq = seq // tm
    single_mod = mod.shape[0] == 1
    mod_idx = (lambda i, *_: (0, 0, 0)) if single_mod else (lambda i, *_: (i // tiles_per_seq, 0, 0))
    row = lambda i, *_: (i, 0)
    grid_spec = pltpu.PrefetchScalarGridSpec(
        num_scalar_prefetch=1,
        grid=(t // tm,),
        in_specs=[pl.BlockSpec((rows, sub, LANES), lambda i, *_: (0, 0, 0), pipeline_mode=pl.Buffered(1)),
                  pl.BlockSpec((tm, D_MODEL), row),
                  pl.BlockSpec((None, ADA_CHUNKS, D_MODEL), mod_idx)],
        out_specs=pl.BlockSpec((tm, D_MODEL), row),
        scratch_shapes=[pltpu.VMEM((tm, sub, LANES), BF16)])
    return pl.pallas_call(
        functools.partial(_combine_kernel, tm=tm),
        grid_spec=grid_spec,
        out_shape=jax.ShapeDtypeStruct((t, D_MODEL), F32),
        compiler_params=_cparams("arbitrary"),
        name="moe_combine",
    )(dest, o_sorted, x1, mod)


def _rope_tables(n_tokens):
    rows = n_tokens // GRID_W
    row_ids = jnp.repeat(jnp.arange(rows, dtype=F32), GRID_W)
    col_ids = jnp.tile(jnp.arange(GRID_W, dtype=F32), rows)
    inv_freq = jnp.power(ROPE_THETA, -jnp.arange(ROPE_PAIRS, dtype=F32) / ROPE_PAIRS)
    ang_r = row_ids[:, None] * inv_freq[None, :]
    ang_c = col_ids[:, None] * inv_freq[None, :]
    ang = jnp.concatenate([ang_r, ang_r, ang_c, ang_c] * 2, axis=-1)
    return jnp.cos(ang), jnp.sin(ang)


def _layer(x, mod, lp, rope_tabs, cache, lam_rows, *, kv_dtype):
    b, n, _ = x.shape
    t = b * n
    q, k, v, p, g = _pre_mixer(x.reshape(t, D_MODEL), mod, lp, rope_tabs, seq=n, tm=256, kv_dtype=kv_dtype)
    o = _attention(q.reshape(b, n, ATTN_W), k.reshape(b, n, ATTN_W), v.reshape(b, n, ATTN_W),
                   cache, lam_rows, lp["subln_g"], tq=256, heads_per_step=N_HEADS if cache is None else 2)
    x1, h2p, gw, route, counts = _post_mixer(o.reshape(t, ATTN_W), p, g, x.reshape(t, D_MODEL), mod, lp,
                                             seq=n, n_sub=2)
    cls, rank = route[:, 0, :].reshape(t), route[:, 1, :].reshape(t)
    o_sorted, dest = _moe(h2p, gw, cls, rank, counts[0, :N_CLASSES].astype(jnp.int32), lp)
    y = _combine(o_sorted, x1, mod, dest, seq=n, tm=256)
    return y.reshape(b, n, D_MODEL), k, v


def kernel(x_prompt, x_sample, c, cache_k, cache_v, c_ctx, w_ada, b_ada, norm1_g, w_in, b_gate, q_norm_g, k_norm_g, lambda_q1, lambda_k1, lambda_q2, lambda_k2, subln_g, pool_w, pool_scale, w_br_attn, w_br_pool, w_out, norm2_g, router_group_w, router_group_b, router_expert_w, router_expert_b, expert_w_gate, expert_w_up, expert_w_down):
    b_ctx, n_ctx, _ = x_prompt.shape
    b_lat, n_lat, _ = x_sample.shape

    def router_layout(we, wg):
        pad = jnp.zeros(we.shape[:-1] + (LANES - N_EXPERTS,), F32)
        return jnp.concatenate([we, pad, jnp.repeat(wg, EXPERTS_PER_GROUP, axis=-1), pad], axis=-1)

    w_router = router_layout(router_expert_w[0], router_group_w[0])
    w_router_hi, w_router_lo = _split_bf16(w_router)
    seg = (jnp.arange(256)[:, None] // HEAD_DIM == jnp.arange(256)[None, :] // HEAD_DIM)
    lp = dict(
        norm1_g=norm1_g[0].reshape(1, D_MODEL),
        w_in=w_in[0].astype(BF16),
        b_gate=b_gate[0].reshape(1, 2 * D_MODEL),
        q_gain=jnp.tile(q_norm_g[0], ATTN_W // HEAD_DIM).reshape(1, ATTN_W),
        k_gain=jnp.tile(k_norm_g[0], ATTN_W // HEAD_DIM).reshape(1, ATTN_W),
        seg=(seg.astype(F32) / HEAD_DIM).astype(BF16),
        subln_g=subln_g[0].reshape(1, HEAD_W),
        w_br_attn=w_br_attn[0].astype(BF16),
        pool_w=pool_w[0].astype(BF16),
        pool_scale=pool_scale[0].reshape(1, POOL_W),
        w_br_pool=w_br_pool[0].astype(BF16),
        w_out=w_out[0].astype(BF16),
        norm2_g=norm2_g[0].reshape(1, D_MODEL),
        w_router_hi=w_router_hi, w_router_lo=w_router_lo,
        b_router=router_layout(router_expert_b[0], router_group_b[0]).reshape(1, ROUTER_W),
        expert_w_gate=expert_w_gate[0].astype(BF16),
        expert_w_up=expert_w_up[0].astype(BF16),
        expert_w_down=expert_w_down[0].astype(BF16),
    )
    lam_rows = jnp.stack([lambda_q1[0], lambda_k1[0], lambda_q2[0], lambda_k2[0]], axis=0)

    n_cond = 1 + b_lat
    cond = jnp.concatenate([c_ctx[None, :], c, jnp.zeros((16 - n_cond, D_MODEL), F32)], axis=0)
    mod = _adaln(cond, w_ada[0], b_ada[0])[:n_cond].reshape(n_cond, ADA_CHUNKS, D_MODEL)

    y_prompt, k_ctx, v_ctx = _layer(x_prompt, mod[:1], lp, None, None, lam_rows, kv_dtype=F32)
    cache = (cache_k, cache_v)
    y_sample, _, _ = _layer(x_sample, mod[1:], lp, _rope_tables(n_lat), cache, lam_rows, kv_dtype=BF16)

    new_cache_k = k_ctx.reshape(b_ctx, 1, n_ctx, N_HEADS, HEAD_W)
    new_cache_v = v_ctx.reshape(b_ctx, 1, n_ctx, N_HEADS, HEAD_W)
    return (y_prompt, y_sample, new_cache_k, new_cache_v)
```

```python
import functools
import math

import jax
import jax.numpy as jnp
from jax import lax
from jax.experimental import pallas as pl
from jax.experimental.pallas import tpu as pltpu

D_MODEL = 1024
GRID_W = 64
N_HEADS = 8
HEAD_DIM = 64
HEAD_W = 2 * HEAD_DIM
ATTN_W = N_HEADS * HEAD_W
POOL_GROUPS = 4
POOL_WINDOWS = (2, 4, 8, 16)
POOL_W = 512
POOL_GROUP_W = 128
IN_W = 3 * ATTN_W + POOL_W + 2 * D_MODEL
ROPE_THETA = 10000.0
ROPE_PAIRS = 16
N_GROUPS = 4
EXPERTS_PER_GROUP = 4
N_EXPERTS = 16
PAIRS_PER_GROUP = 6
N_CLASSES = N_GROUPS * PAIRS_PER_GROUP
PAIR_LO = (0, 0, 0, 1, 1, 2)
PAIR_HI = (1, 2, 3, 2, 3, 3)
D_EXPERT = 512
MOE_TILE = 256
MOE_ROWS = 128
POST_ROWS = 256
POOL_HALO = 16
ADA_CHUNKS = 6
EPS = 1e-6
LAMBDA_INIT = 0.8 - 0.6 * math.exp(-0.0)

LANES = 128
ROW_SUB = D_MODEL // LANES
ROUTER_W = 2 * LANES
NEG_BIG = -1e30
Q_SCALE = math.log2(math.e) * HEAD_DIM ** -0.5

F32 = jnp.float32
BF16 = jnp.bfloat16

VMEM_LIMIT = 56 * 1024 * 1024


def _cparams(*sem):
    return pltpu.CompilerParams(dimension_semantics=sem, vmem_limit_bytes=VMEM_LIMIT)


def _split_bf16(x):
    hi = x.astype(BF16)
    lo = (x - hi.astype(F32)).astype(BF16)
    return hi, lo


def _dot(a, b):
    return jnp.dot(a, b, preferred_element_type=F32)


def _cast_weight(w_ref, wb_ref):
    step = 512
    for c in range(0, w_ref.shape[1], step):
        wb_ref[:, c:c + step] = w_ref[:, c:c + step].astype(BF16)


def _adaln_kernel(cond_ref, w_ref, b_ref, o_ref):
    c = cond_ref[...]
    s = c * jax.nn.sigmoid(c)
    s_hi, s_lo = _split_bf16(s)
    w_hi, w_lo = _split_bf16(w_ref[...])
    rows = s.shape[0]
    both = _dot(jnp.concatenate([s_hi, s_lo], axis=0), w_hi)
    o_ref[...] = both[:rows] + both[rows:] + _dot(s_hi, w_lo) + b_ref[...]


def _adaln(cond, w_ada, b_ada):
    rows = cond.shape[0]
    n = w_ada.shape[1]
    tn = 1536
    return pl.pallas_call(
        _adaln_kernel,
        grid=(n // tn,),
        in_specs=[pl.BlockSpec((rows, D_MODEL), lambda j: (0, 0)),
                  pl.BlockSpec((D_MODEL, tn), lambda j: (0, j)),
                  pl.BlockSpec((1, tn), lambda j: (0, j))],
        out_specs=pl.BlockSpec((rows, tn), lambda j: (0, j)),
        out_shape=jax.ShapeDtypeStruct((rows, n), F32),
        compiler_params=_cparams("arbitrary"),
        name="adaln",
    )(cond, w_ada, b_ada.reshape(1, n))


def _chunk_rms(z, seg):
    zz = (z * z).astype(BF16)
    parts = [_dot(zz[:, c * 256:(c + 1) * 256], seg) for c in range(z.shape[1] // 256)]
    return jnp.concatenate(parts, axis=1)


def _rope(x, cos, sin):
    lane = lax.broadcasted_iota(jnp.int32, (x.shape[0], LANES), 1)
    first = (lane % (2 * ROPE_PAIRS)) < ROPE_PAIRS
    parts = []
    for c in range(x.shape[1] // LANES):
        xc = x[:, c * LANES:(c + 1) * LANES]
        up = pltpu.roll(xc, LANES - ROPE_PAIRS, 1)
        dn = pltpu.roll(xc, ROPE_PAIRS, 1)
        parts.append(xc * cos + jnp.where(first, -up, dn) * sin)
    return jnp.concatenate(parts, axis=1)


def _pre_kernel(*refs, rope, kv_dtype):
    if rope:
        (x_ref, mod_ref, g1_ref, wf_ref, bg_ref, qg_ref, kg_ref, seg_ref, cos_ref, sin_ref,
         q_out, k_out, v_out, p_out, g_out, w_ref) = refs
    else:
        (x_ref, mod_ref, g1_ref, wf_ref, bg_ref, qg_ref, kg_ref, seg_ref,
         q_out, k_out, v_out, p_out, g_out, w_ref) = refs

    @pl.when(pl.program_id(0) == 0)
    def _():
        _cast_weight(wf_ref, w_ref)

    x = x_ref[...]
    shift = mod_ref[0:1, :]
    scale = mod_ref[1:2, :]
    xn = x * lax.rsqrt(jnp.mean(x * x, axis=-1, keepdims=True) + EPS) * g1_ref[...]
    h = (xn * (1.0 + scale) + shift).astype(BF16)
    seg = seg_ref[...]

    zq = _dot(h, w_ref[:, 0:ATTN_W])
    qn = zq * lax.rsqrt(_chunk_rms(zq, seg) + EPS) * qg_ref[...]
    if rope:
        qn = _rope(qn, cos_ref[...], sin_ref[...])
    q_out[...] = (qn * Q_SCALE).astype(BF16)

    zk = _dot(h, w_ref[:, ATTN_W:2 * ATTN_W])
    kn = zk * lax.rsqrt(_chunk_rms(zk, seg) + EPS) * kg_ref[...]
    if rope:
        kn = _rope(kn, cos_ref[...], sin_ref[...])
    k_out[...] = kn.astype(kv_dtype)

    v_out[...] = _dot(h, w_ref[:, 2 * ATTN_W:3 * ATTN_W]).astype(kv_dtype)
    p_out[...] = _dot(h, w_ref[:, 3 * ATTN_W:3 * ATTN_W + POOL_W]).astype(BF16)
    gl = _dot(h, w_ref[:, 3 * ATTN_W + POOL_W:IN_W]) + bg_ref[...]
    g_out[...] = jax.nn.sigmoid(gl).astype(BF16)


def _pre_mixer(x, mod, lp, rope_tabs, *, seq, tm, kv_dtype):
    t = x.shape[0]
    tiles_per_seq = seq // tm
    single_mod = mod.shape[0] == 1
    mod_idx = (lambda i: (0, 0, 0)) if single_mod else (lambda i: (i // tiles_per_seq, 0, 0))
    const = lambda i: (0, 0)
    row = lambda i: (i, 0)
    in_specs = [pl.BlockSpec((tm, D_MODEL), row),
                pl.BlockSpec((None, ADA_CHUNKS, D_MODEL), mod_idx),
                pl.BlockSpec((1, D_MODEL), const),
                pl.BlockSpec((D_MODEL, IN_W), const, pipeline_mode=pl.Buffered(1)),
                pl.BlockSpec((1, 2 * D_MODEL), const),
                pl.BlockSpec((1, ATTN_W), const),
                pl.BlockSpec((1, ATTN_W), const),
                pl.BlockSpec((256, 256), const)]
    args = [x, mod, lp["norm1_g"], lp["w_in"], lp["b_gate"], lp["q_gain"], lp["k_gain"], lp["seg"]]
    rope = rope_tabs is not None
    if rope:
        in_specs += [pl.BlockSpec((tm, LANES), lambda i: (i % tiles_per_seq, 0))] * 2
        args += list(rope_tabs)
    out_shape = [jax.ShapeDtypeStruct((t, ATTN_W), BF16),
                 jax.ShapeDtypeStruct((t, ATTN_W), kv_dtype),
                 jax.ShapeDtypeStruct((t, ATTN_W), kv_dtype),
                 jax.ShapeDtypeStruct((t, POOL_W), BF16),
                 jax.ShapeDtypeStruct((t, 2 * D_MODEL), BF16)]
    out_specs = [pl.BlockSpec((tm, ATTN_W), row), pl.BlockSpec((tm, ATTN_W), row),
                 pl.BlockSpec((tm, ATTN_W), row), pl.BlockSpec((tm, POOL_W), row),
                 pl.BlockSpec((tm, 2 * D_MODEL), row)]
    return pl.pallas_call(
        functools.partial(_pre_kernel, rope=rope, kv_dtype=kv_dtype),
        grid=(t // tm,),
        in_specs=in_specs, out_specs=out_specs, out_shape=out_shape,
        scratch_shapes=[pltpu.VMEM((D_MODEL, IN_W), BF16)],
        compiler_params=_cparams("arbitrary"),
        name="pre_mixer_rope" if rope else "pre_mixer",
    )(*args)


def _attn_kernel(*refs, n_q_blocks, tq, has_cache, heads_per_step):
    if has_cache:
        q_ref, k_ref, v_ref, ck_ref, cv_ref, lam_ref, sg_ref, o_ref, k1_s, k2_s, v_s = refs
    else:
        q_ref, k_ref, v_ref, lam_ref, sg_ref, o_ref, k1_s, k2_s, v_s = refs
    lv = lam_ref[...]
    lam = (jnp.exp(jnp.sum(lv[0:1] * lv[1:2], axis=-1, keepdims=True))
           - jnp.exp(jnp.sum(lv[2:3] * lv[3:4], axis=-1, keepdims=True)) + LAMBDA_INIT)
    nt = (((1,), (1,)), ((), ()))
    sg = sg_ref[...] * (1.0 - LAMBDA_INIT)
    n = k_ref.shape[0]

    def stage(hh, rows, kf, vf):
        kf = kf.astype(F32)
        lane = lax.broadcasted_iota(jnp.int32, kf.shape, 1)
        k1_s[hh, rows, :] = jnp.where(lane < HEAD_DIM, kf, 0.0).astype(BF16)
        k2_s[hh, rows, :] = jnp.where(lane >= HEAD_DIM, kf, 0.0).astype(BF16)
        v_s[hh, rows, :HEAD_W] = vf.astype(BF16)

    v_s[:, :, HEAD_W:] = jnp.ones(v_s.shape[:2] + (HEAD_W,), BF16)

    for hh in range(heads_per_step):
        cols = slice(hh * HEAD_W, (hh + 1) * HEAD_W)
        stage(hh, slice(0, n), k_ref[:, cols], v_ref[:, cols])
        if has_cache:
            head = pl.program_id(1) * heads_per_step + hh
            stage(hh, slice(n, k1_s.shape[1]), ck_ref[:, head, :], cv_ref[:, head, :])

        def softmax_av(q, k_s):
            s = lax.dot_general(q, k_s[hh], nt, preferred_element_type=F32)
            e = jnp.exp2(s - jnp.max(s, axis=-1, keepdims=True)).astype(BF16)
            ov = _dot(e, v_s[hh])
            return ov[:, :HEAD_W] / ov[:, HEAD_W:]

        def block(i, carry):
            qs = pl.multiple_of(i * tq, tq)
            q = q_ref[pl.ds(qs, tq), cols]
            o = softmax_av(q, k1_s) - lam * softmax_av(q, k2_s)
            on = o * lax.rsqrt(jnp.mean(o * o, axis=-1, keepdims=True) + EPS) * sg
            o_ref[pl.ds(qs, tq), cols] = on.astype(o_ref.dtype)
            return carry

        if n_q_blocks == 1:
            block(0, 0)
        else:
            lax.fori_loop(0, n_q_blocks, block, 0, unroll=True)


def _attention(q, k, v, cache, lam_rows, subln_g, *, tq, heads_per_step):
    b, n, _ = q.shape
    has_cache = cache is not None
    heads = lambda bi, hi: (bi, 0, hi)
    const = lambda bi, hi: (0, 0)
    in_specs = [pl.BlockSpec((None, n, heads_per_step * HEAD_W), heads)] * 3
    args = [q, k, v]
    if has_cache:
        p_len = cache[0].shape[2]
        in_specs += [pl.BlockSpec((None, None, p_len, N_HEADS, HEAD_W), lambda bi, hi: (bi, 0, 0, 0, 0))] * 2
        args += list(cache)
    in_specs += [pl.BlockSpec((4, HEAD_DIM), const), pl.BlockSpec((1, HEAD_W), const)]
    args += [lam_rows, subln_g]
    n_keys = n + (cache[0].shape[2] if has_cache else 0)
    return pl.pallas_call(
        functools.partial(_attn_kernel, n_q_blocks=n // tq, tq=tq, has_cache=has_cache,
                          heads_per_step=heads_per_step),
        grid=(b, N_HEADS // heads_per_step),
        in_specs=in_specs,
        out_specs=pl.BlockSpec((None, n, heads_per_step * HEAD_W), heads),
        out_shape=jax.ShapeDtypeStruct((b, n, ATTN_W), BF16),
        scratch_shapes=[pltpu.VMEM((heads_per_step, n_keys, HEAD_W), BF16)] * 2
        + [pltpu.VMEM((heads_per_step, n_keys, 2 * HEAD_W), BF16)],
        compiler_params=_cparams("arbitrary", "arbitrary"),
        name="diff_attn_cache" if has_cache else "diff_attn",
    )(*args)


def _route(logits):
    rows = logits.shape[0]
    lane_i = lax.broadcasted_iota(jnp.int32, (rows, LANES), 1)
    valid = lane_i < N_EXPERTS
    lane = lane_i.astype(F32)
    grp = (lane_i // EXPERTS_PER_GROUP).astype(F32)
    e_log = logits[:, :LANES]
    g_log = jnp.where(valid, logits[:, LANES:], NEG_BIG)
    g_max = jnp.max(g_log, axis=-1, keepdims=True)
    g_den = jnp.sum(jnp.exp(g_log - g_max), axis=-1, keepdims=True) * (1.0 / EXPERTS_PER_GROUP)
    g_w = 1.0 / g_den
    g_idx = jnp.min(jnp.where(g_log == g_max, grp, float(N_GROUPS)), axis=-1, keepdims=True)
    e_sel = jnp.where(grp == g_idx, jnp.where(valid, e_log, NEG_BIG), NEG_BIG)
    v1 = jnp.max(e_sel, axis=-1, keepdims=True)
    i1 = jnp.min(jnp.where(e_sel == v1, lane, float(LANES)), axis=-1, keepdims=True)
    e_rest = jnp.where(lane == i1, NEG_BIG, e_sel)
    v2 = jnp.max(e_rest, axis=-1, keepdims=True)
    i2 = jnp.min(jnp.where(e_rest == v2, lane, float(LANES)), axis=-1, keepdims=True)
    t = jnp.exp(v2 - v1)
    w1 = g_w / (1.0 + t)
    w2 = w1 * t
    first_low = i1 < i2
    a = jnp.minimum(i1, i2) - EXPERTS_PER_GROUP * g_idx
    b = jnp.maximum(i1, i2) - EXPERTS_PER_GROUP * g_idx
    pair = a * (7.0 - a) * 0.5 + (b - a - 1.0)
    cls = g_idx * PAIRS_PER_GROUP + pair
    return cls, jnp.where(first_low, w1, w2), jnp.where(first_low, w2, w1)


def _post_kernel(o_ref, p_ref, g_ref, x_ref, mod_ref, waf_ref, pw_ref, ps_ref, wpf_ref, wof_ref,
                 g2_ref, wrh_ref, wrl_ref, br_ref, x1_out, h2p_out, gw_out, route_out, counts_out,
                 carry_ref, wa_ref, wp_ref, wo_ref, *, seq, n_sub):
    step = pl.program_id(0)

    @pl.when(step == 0)
    def _():
        carry_ref[...] = jnp.zeros_like(carry_ref)
        _cast_weight(waf_ref, wa_ref)
        _cast_weight(wpf_ref, wp_ref)
        _cast_weight(wof_ref, wo_ref)

    tm = POST_ROWS
    tiles_per_seq = max(seq // (n_sub * tm), 1)
    ext = tm + 2 * POOL_HALO
    chains = range(n_sub)
    rows = [slice(s * tm, (s + 1) * tm) for s in chains]
    if seq == tm:
        blk0, t0 = [s * tm for s in chains], [0] * n_sub
    else:
        blk0 = t0 = [pl.multiple_of(((step % tiles_per_seq) * n_sub + s) * tm, tm) for s in chains]

    attn_out = [_dot(o_ref[rows[s], :], wa_ref[...]) for s in chains]

    def pooled(s):
        p_mid = p_ref[pl.ds(blk0[s], tm), :].astype(F32)
        if seq == tm:
            halo_top = halo_bot = jnp.zeros((POOL_HALO, POOL_W), F32)
        else:
            top0 = pl.multiple_of(jnp.maximum(blk0[s] - POOL_HALO, 0), POOL_HALO)
            bot0 = pl.multiple_of(jnp.minimum(blk0[s] + tm, seq - POOL_HALO), POOL_HALO)
            halo_top = p_ref[pl.ds(top0, POOL_HALO), :].astype(F32) * jnp.where(t0[s] > 0, 1.0, 0.0)
            halo_bot = p_ref[pl.ds(bot0, POOL_HALO), :].astype(F32) * jnp.where(t0[s] + tm < seq, 1.0, 0.0)
        p_ext = jnp.concatenate([halo_top, p_mid, halo_bot], axis=0)
        tok1 = t0[s] + lax.broadcasted_iota(jnp.int32, (tm, 1), 0)
        out = []
        for gi, w in enumerate(POOL_WINDOWS):
            half = w // 2
            sl = slice(gi * POOL_GROUP_W, (gi + 1) * POOL_GROUP_W)
            run = p_ext[:, sl]
            k = 1
            while k < w:
                run = run + pltpu.roll(run, ext - k, 0)
                k *= 2
            win = pltpu.roll(run, ext - (POOL_HALO - half), 0)[:tm]
            cnt = (jnp.minimum(tok1 + half, seq) - jnp.maximum(tok1 - half, 0)).astype(F32)
            out.append((win / cnt - p_mid[:, sl]).astype(BF16))
        return out

    pool_in = [pooled(s) for s in chains]
    mixed = [jnp.concatenate([_dot(pool_in[s][gi], pw_ref[gi]) for gi in range(POOL_GROUPS)], axis=1)
             * ps_ref[...] for s in chains]
    pool_out = [_dot(mixed[s].astype(BF16), wp_ref[...]) for s in chains]

    def merge(s):
        g = g_ref[rows[s], :]
        return (g[:, :D_MODEL].astype(F32) * attn_out[s] + g[:, D_MODEL:].astype(F32) * pool_out[s]).astype(BF16)

    merged = [merge(s) for s in chains]
    gate1 = mod_ref[2:3, :]
    x1 = [x_ref[rows[s], :] + gate1 * _dot(merged[s], wo_ref[...]) for s in chains]
    for s in chains:
        x1_out[rows[s], :] = x1[s]

    shift2 = mod_ref[3:4, :]
    scale2 = mod_ref[4:5, :]
    h2 = [x1[s] * lax.rsqrt(jnp.mean(x1[s] * x1[s], axis=-1, keepdims=True) + EPS) * g2_ref[...]
          * (1.0 + scale2) + shift2 for s in chains]
    h2_parts = [_split_bf16(h2[s]) for s in chains]
    for s in chains:
        h2p_out[rows[s]] = h2_parts[s][0].reshape(tm, ROW_SUB, LANES)

    logits = [_dot(h2_parts[s][0], wrh_ref[...]) + _dot(h2_parts[s][1], wrh_ref[...])
              + _dot(h2_parts[s][0], wrl_ref[...]) + br_ref[...] for s in chains]
    routes = [_route(logits[s]) for s in chains]
    lane = lax.broadcasted_iota(jnp.int32, (tm, LANES), 1)
    for s in chains:
        _, w_lo, w_hi = routes[s]
        gw_out[rows[s]] = jnp.where(lane == 0, w_lo, jnp.where(lane == 1, w_hi, 0.0)).reshape(tm, 1, LANES)

    row = lax.broadcasted_iota(jnp.int32, (tm, tm), 0)
    col = lax.broadcasted_iota(jnp.int32, (tm, tm), 1)
    before = jnp.where(col < row, 1.0, 0.0).astype(BF16)
    onehot = [jnp.where(lane.astype(F32) == routes[s][0], 1.0, 0.0) for s in chains]
    within = [_dot(before, onehot[s].astype(BF16)) for s in chains]
    for s in chains:
        rank = jnp.sum(onehot[s] * (within[s] + carry_ref[...]), axis=-1, keepdims=True)
        info = jnp.where(lane == 0, routes[s][0], jnp.where(lane == 1, rank, 0.0))
        route_out[s] = jnp.transpose(info)[:8, :].astype(jnp.int32)
        carry_ref[...] += jnp.sum(onehot[s], axis=0, keepdims=True)
    counts_out[...] = carry_ref[...]


def _post_mixer(o, p, g, x, mod, lp, *, seq, n_sub):
    t = x.shape[0]
    tm = n_sub * POST_ROWS
    tiles_per_seq = max(seq // tm, 1)
    single_mod = mod.shape[0] == 1
    mod_idx = (lambda i: (0, 0, 0)) if single_mod else (lambda i: (i // tiles_per_seq, 0, 0))
    tile = lambda i: (i, 0)
    const2 = lambda i: (0, 0)
    const3 = lambda i: (0, 0, 0)
    p_spec = (pl.BlockSpec((tm, POOL_W), tile) if seq == POST_ROWS
              else pl.BlockSpec((seq, POOL_W), lambda i: (i // tiles_per_seq, 0)))
    in_specs = [pl.BlockSpec((tm, ATTN_W), tile),
                p_spec,
                pl.BlockSpec((tm, 2 * D_MODEL), tile),
                pl.BlockSpec((tm, D_MODEL), tile),
                pl.BlockSpec((None, ADA_CHUNKS, D_MODEL), mod_idx),
                pl.BlockSpec((ATTN_W, D_MODEL), const2, pipeline_mode=pl.Buffered(1)),
                pl.BlockSpec((POOL_GROUPS, POOL_GROUP_W, POOL_GROUP_W), const3),
                pl.BlockSpec((1, POOL_W), const2),
                pl.BlockSpec((POOL_W, D_MODEL), const2, pipeline_mode=pl.Buffered(1)),
                pl.BlockSpec((D_MODEL, D_MODEL), const2, pipeline_mode=pl.Buffered(1)),
                pl.BlockSpec((1, D_MODEL), const2),
                pl.BlockSpec((D_MODEL, ROUTER_W), const2),
                pl.BlockSpec((D_MODEL, ROUTER_W), const2),
                pl.BlockSpec((1, ROUTER_W), const2)]
    tile3 = lambda i: (i, 0, 0)
    out_shape = [jax.ShapeDtypeStruct((t, D_MODEL), F32),
                 jax.ShapeDtypeStruct((t, ROW_SUB, LANES), BF16),
                 jax.ShapeDtypeStruct((t, 1, LANES), F32),
                 jax.ShapeDtypeStruct((t // POST_ROWS, 8, POST_ROWS), jnp.int32),
                 jax.ShapeDtypeStruct((1, LANES), F32)]
    out_specs = [pl.BlockSpec((tm, D_MODEL), tile),
                 pl.BlockSpec((tm, ROW_SUB, LANES), tile3),
                 pl.BlockSpec((tm, 1, LANES), tile3),
                 pl.BlockSpec((n_sub, 8, POST_ROWS), tile3),
                 pl.BlockSpec((1, LANES), const2)]
    return pl.pallas_call(
        functools.partial(_post_kernel, seq=seq, n_sub=n_sub),
        grid=(t // tm,),
        in_specs=in_specs, out_specs=out_specs, out_shape=out_shape,
        scratch_shapes=[pltpu.VMEM((1, LANES), F32),
                        pltpu.VMEM((ATTN_W, D_MODEL), BF16),
                        pltpu.VMEM((POOL_W, D_MODEL), BF16),
                        pltpu.VMEM((D_MODEL, D_MODEL), BF16)],
        compiler_params=_cparams("arbitrary"),
        name="post_mixer",
    )(o, p, g, x, mod, lp["w_br_attn"], lp["pool_w"], lp["pool_scale"], lp["w_br_pool"], lp["w_out"],
      lp["norm2_g"], lp["w_router_hi"], lp["w_router_lo"], lp["b_router"])


def _moe_kernel(cls_s, rank_s, cnt_s, h_hbm, gw_hbm, wg_hbm, wu_hbm, wd_hbm, o_ref, dest_s,
                src_s, off_s, elo_s, nlo_s, ehi_s, nhi_s, valid_s, nt_s,
                h_ref, gw_ref, xg_ref, gwg_ref, wg_stage, wu_stage, wd_stage, wg_buf, wu_buf, wd_buf,
                sems, in_sems, *, n_tokens):
    j = pl.program_id(0)

    stages = ((wg_hbm, wg_stage, wg_buf), (wu_hbm, wu_stage, wu_buf), (wd_hbm, wd_stage, wd_buf))

    def weight_copies(e, role):
        return [pltpu.make_async_copy(hbm.at[e], stage.at[role], sems.at[role, i])
                for i, (hbm, stage, _) in enumerate(stages)]

    def cast_weights(role):
        for _, stage, buf in stages:
            for r in range(0, stage.shape[1], 256):
                buf[role, r:r + 256, :] = stage[role, r:r + 256, :].astype(BF16)

    token_copies = [pltpu.make_async_copy(h_hbm, h_ref, in_sems.at[0]),
                    pltpu.make_async_copy(gw_hbm, gw_ref, in_sems.at[1])]
    roles = ((0, elo_s, nlo_s), (1, ehi_s, nhi_s))

    def gather_tile(tile, slot):
        base = tile * MOE_TILE
        for r in range(MOE_TILE):
            t = src_s[base + r]
            xg_ref[slot, r] = h_ref[t]
            gwg_ref[slot, r] = gw_ref[t]

    @pl.when(j == 0)
    def _():
        for cp in token_copies:
            cp.start()

        tile = 0
        for c in range(N_CLASSES):
            n_tok = cnt_s[c]
            n_cls_tiles = (n_tok + MOE_TILE - 1) // MOE_TILE
            off_s[c] = tile * MOE_TILE
            e_lo = (c // PAIRS_PER_GROUP) * EXPERTS_PER_GROUP + PAIR_LO[c % PAIRS_PER_GROUP]
            e_hi = (c // PAIRS_PER_GROUP) * EXPERTS_PER_GROUP + PAIR_HI[c % PAIRS_PER_GROUP]

            def fill(k, carry, tile=tile, n_tok=n_tok, e_lo=e_lo, e_hi=e_hi):
                elo_s[tile + k] = e_lo
                ehi_s[tile + k] = e_hi
                valid_s[tile + k] = jnp.minimum(n_tok - k * MOE_TILE, MOE_TILE)
                return carry

            lax.fori_loop(0, n_cls_tiles, fill, 0)
            tile = tile + n_cls_tiles
        off_s[N_CLASSES] = tile * MOE_TILE
        nt_s[0] = tile
        n_tiles = tile

        for role, e_s, nxt_s in roles:
            for cp in weight_copies(e_s[0], role):
                cp.start()

            def backward(i, nxt, e_s=e_s, nxt_s=nxt_s):
                t = n_tiles - 1 - i
                after = jnp.minimum(t + 1, n_tiles - 1)
                nxt = jnp.where((t + 1 < n_tiles) & (e_s[after] != e_s[t]), e_s[after], nxt)
                nxt_s[t] = nxt
                return nxt

            lax.fori_loop(0, n_tiles, backward, -1)

        def clear_tail(c, carry):
            start = jnp.maximum(off_s[c + 1] - MOE_TILE, 0)
            for i in range(MOE_TILE):
                src_s[start + i] = 0
            return carry

        lax.fori_loop(0, N_CLASSES, clear_tail, 0)

        def place(t, c):
            row = off_s[cls_s[t]] + rank_s[t]
            src_s[row] = t
            dest_s[t] = row
            return c

        lax.fori_loop(0, n_tokens, place, 0, unroll=16)
        for cp in token_copies:
            cp.wait()
        gather_tile(0, 0)

    n_tiles = nt_s[0]

    @pl.when(j < n_tiles)
    def _():
        for role, e_s, nxt_s in roles:
            run_starts = (j == 0) | (e_s[j] != e_s[jnp.maximum(j - 1, 0)])

            @pl.when(run_starts)
            def _():
                for cp in weight_copies(e_s[j], role):
                    cp.wait()
                cast_weights(role)

                @pl.when(nxt_s[j] >= 0)
                def _():
                    for cp in weight_copies(nxt_s[j], role):
                        cp.start()

        slot = j % 2
        gather_tile(jnp.minimum(j + 1, n_tiles - 1), 1 - slot)

        def experts(rows):
            x = xg_ref[slot, :rows].reshape(rows, D_MODEL)
            gw = gwg_ref[slot, :rows].reshape(rows, LANES)
            a1, a2 = _dot(x, wg_buf[0]), _dot(x, wg_buf[1])
            u1, u2 = _dot(x, wu_buf[0]), _dot(x, wu_buf[1])
            hid1 = (a1 * jax.nn.sigmoid(a1) * u1 * gw[:, 0:1]).astype(BF16)
            hid2 = (a2 * jax.nn.sigmoid(a2) * u2 * gw[:, 1:2]).astype(BF16)
            o = _dot(hid1, wd_buf[0]) + _dot(hid2, wd_buf[1])
            o_ref[:rows] = o.astype(BF16).reshape(rows, ROW_SUB, LANES)
            if rows < MOE_TILE:
                o_ref[rows:] = jnp.zeros((MOE_TILE - rows, ROW_SUB, LANES), BF16)

        pieces = (valid_s[j] + MOE_ROWS - 1) // MOE_ROWS
        for n in range(1, MOE_TILE // MOE_ROWS + 1):
            @pl.when(pieces == n)
            def _():
                experts(n * MOE_ROWS)

    @pl.when(j >= n_tiles)
    def _():
        o_ref[...] = jnp.zeros_like(o_ref)


def _moe(h2p, gw, cls, rank, cnt, lp):
    t, sub, _ = h2p.shape
    max_tiles = t // MOE_TILE + N_CLASSES
    in_hbm = pl.BlockSpec(memory_space=pl.ANY)
    tile_table = pltpu.SMEM((max_tiles,), jnp.int32)
    grid_spec = pltpu.PrefetchScalarGridSpec(
        num_scalar_prefetch=3,
        grid=(max_tiles,),
        in_specs=[in_hbm] * 5,
        out_specs=[pl.BlockSpec((MOE_TILE, sub, LANES), lambda j, *_: (j, 0, 0)),
                   pl.BlockSpec(memory_space=pltpu.SMEM)],
        scratch_shapes=[pltpu.SMEM((max_tiles * MOE_TILE,), jnp.int32),
                        pltpu.SMEM((N_CLASSES + 1,), jnp.int32)]
        + [tile_table] * 5
        + [pltpu.SMEM((1,), jnp.int32),
           pltpu.VMEM((t, sub, LANES), BF16),
           pltpu.VMEM((t, 1, LANES), F32),
           pltpu.VMEM((2, MOE_TILE, sub, LANES), BF16),
           pltpu.VMEM((2, MOE_TILE, 1, LANES), F32),
           pltpu.VMEM((2, D_MODEL, D_EXPERT), F32),
           pltpu.VMEM((2, D_MODEL, D_EXPERT), F32),
           pltpu.VMEM((2, D_EXPERT, D_MODEL), F32),
           pltpu.VMEM((2, D_MODEL, D_EXPERT), BF16),
           pltpu.VMEM((2, D_MODEL, D_EXPERT), BF16),
           pltpu.VMEM((2, D_EXPERT, D_MODEL), BF16),
           pltpu.SemaphoreType.DMA((2, 3)),
           pltpu.SemaphoreType.DMA((2,))])
    return pl.pallas_call(
        functools.partial(_moe_kernel, n_tokens=t),
        grid_spec=grid_spec,
        out_shape=[jax.ShapeDtypeStruct((max_tiles * MOE_TILE, sub, LANES), BF16),
                   jax.ShapeDtypeStruct((t,), jnp.int32)],
        compiler_params=_cparams("arbitrary"),
        name="moe",
    )(cls, rank, cnt, h2p, gw, lp["expert_w_gate"], lp["expert_w_up"], lp["expert_w_down"])


def _combine_kernel(dest_s, o_ref, x1_ref, mod_ref, y_ref, og_ref, *, tm):
    base = pl.program_id(0) * tm

    def gather(r, c):
        og_ref[r] = o_ref[dest_s[base + r]]
        return c

    lax.fori_loop(0, tm, gather, 0, unroll=8)
    moe = og_ref[...].reshape(tm, D_MODEL).astype(F32)
    y_ref[...] = x1_ref[...] + mod_ref[5:6, :] * moe


def _combine(o_sorted, x1, mod, dest, *, seq, tm):
    t = x1.shape[0]
    rows, sub, _ = o_sorted.shape
    tiles_per_seq = seq // tm
    single_mod = mod.shape[0] == 1
    mod_idx = (lambda i, *_: (0, 0, 0)) if single_mod else (lambda i, *_: (i // tiles_per_seq, 0, 0))
    row = lambda i, *_: (i, 0)
    grid_spec = pltpu.PrefetchScalarGridSpec(
        num_scalar_prefetch=1,
        grid=(t // tm,),
        in_specs=[pl.BlockSpec((rows, sub, LANES), lambda i, *_: (0, 0, 0), pipeline_mode=pl.Buffered(1)),
                  pl.BlockSpec((tm, D_MODEL), row),
                  pl.BlockSpec((None, ADA_CHUNKS, D_MODEL), mod_idx)],
        out_specs=pl.BlockSpec((tm, D_MODEL), row),
        scratch_shapes=[pltpu.VMEM((tm, sub, LANES), BF16)])
    return pl.pallas_call(
        functools.partial(_combine_kernel, tm=tm),
        grid_spec=grid_spec,
        out_shape=jax.ShapeDtypeStruct((t, D_MODEL), F32),
        compiler_params=_cparams("arbitrary"),
        name="moe_combine",
    )(dest, o_sorted, x1, mod)


def _rope_tables(n_tokens):
    rows = n_tokens // GRID_W
    row_ids = jnp.repeat(jnp.arange(rows, dtype=F32), GRID_W)
    col_ids = jnp.tile(jnp.arange(GRID_W, dtype=F32), rows)
    inv_freq = jnp.power(ROPE_THETA, -jnp.arange(ROPE_PAIRS, dtype=F32) / ROPE_PAIRS)
    ang_r = row_ids[:, None] * inv_freq[None, :]
    ang_c = col_ids[:, None] * inv_freq[None, :]
    ang = jnp.concatenate([ang_r, ang_r, ang_c, ang_c] * 2, axis=-1)
    return jnp.cos(ang), jnp.sin(ang)


def _layer(x, mod, lp, rope_tabs, cache, lam_rows, *, kv_dtype):
    b, n, _ = x.shape
    t = b * n
    q, k, v, p, g = _pre_mixer(x.reshape(t, D_MODEL), mod, lp, rope_tabs, seq=n, tm=256, kv_dtype=kv_dtype)
    o = _attention(q.reshape(b, n, ATTN_W), k.reshape(b, n, ATTN_W), v.reshape(b, n, ATTN_W),
                   cache, lam_rows, lp["subln_g"], tq=256, heads_per_step=N_HEADS if cache is None else 2)
    x1, h2p, gw, route, counts = _post_mixer(o.reshape(t, ATTN_W), p, g, x.reshape(t, D_MODEL), mod, lp,
                                             seq=n, n_sub=2)
    cls, rank = route[:, 0, :].reshape(t), route[:, 1, :].reshape(t)
    o_sorted, dest = _moe(h2p, gw, cls, rank, counts[0, :N_CLASSES].astype(jnp.int32), lp)
    y = _combine(o_sorted, x1, mod, dest, seq=n, tm=256)
    return y.reshape(b, n, D_MODEL), k, v


def kernel(x_prompt, x_sample, c, cache_k, cache_v, c_ctx, w_ada, b_ada, norm1_g, w_in, b_gate, q_norm_g, k_norm_g, lambda_q1, lambda_k1, lambda_q2, lambda_k2, subln_g, pool_w, pool_scale, w_br_attn, w_br_pool, w_out, norm2_g, router_group_w, router_group_b, router_expert_w, router_expert_b, expert_w_gate, expert_w_up, expert_w_down):
    b_ctx, n_ctx, _ = x_prompt.shape
    b_lat, n_lat, _ = x_sample.shape

    def router_layout(we, wg):
        pad = jnp.zeros(we.shape[:-1] + (LANES - N_EXPERTS,), F32)
        return jnp.concatenate([we, pad, jnp.repeat(wg, EXPERTS_PER_GROUP, axis=-1), pad], axis=-1)

    w_router = router_layout(router_expert_w[0], router_group_w[0])
    w_router_hi, w_router_lo = _split_bf16(w_router)
    seg = (jnp.arange(256)[:, None] // HEAD_DIM == jnp.arange(256)[None, :] // HEAD_DIM)
    lp = dict(
        norm1_g=norm1_g[0].reshape(1, D_MODEL),
        w_in=w_in[0],
        b_gate=b_gate[0].reshape(1, 2 * D_MODEL),
        q_gain=jnp.tile(q_norm_g[0], ATTN_W // HEAD_DIM).reshape(1, ATTN_W),
        k_gain=jnp.tile(k_norm_g[0], ATTN_W // HEAD_DIM).reshape(1, ATTN_W),
        seg=(seg.astype(F32) / HEAD_DIM).astype(BF16),
        subln_g=subln_g[0].reshape(1, HEAD_W),
        w_br_attn=w_br_attn[0],
        pool_w=pool_w[0].astype(BF16),
        pool_scale=pool_scale[0].reshape(1, POOL_W),
        w_br_pool=w_br_pool[0],
        w_out=w_out[0],
        norm2_g=norm2_g[0].reshape(1, D_MODEL),
        w_router_hi=w_router_hi, w_router_lo=w_router_lo,
        b_router=router_layout(router_expert_b[0], router_group_b[0]).reshape(1, ROUTER_W),
        expert_w_gate=expert_w_gate[0],
        expert_w_up=expert_w_up[0],
        expert_w_down=expert_w_down[0],
    )
    lam_rows = jnp.stack([lambda_q1[0], lambda_k1[0], lambda_q2[0], lambda_k2[0]], axis=0)

    n_cond = 1 + b_lat
    cond = jnp.concatenate([c_ctx[None, :], c, jnp.zeros((16 - n_cond, D_MODEL), F32)], axis=0)
    mod = _adaln(cond, w_ada[0], b_ada[0])[:n_cond].reshape(n_cond, ADA_CHUNKS, D_MODEL)

    y_prompt, k_ctx, v_ctx = _layer(x_prompt, mod[:1], lp, None, None, lam_rows, kv_dtype=F32)
    cache = (cache_k, cache_v)
    y_sample, _, _ = _layer(x_sample, mod[1:], lp, _rope_tables(n_lat), cache, lam_rows, kv_dtype=BF16)

    new_cache_k = k_ctx.reshape(b_ctx, 1, n_ctx, N_HEADS, HEAD_W)
    new_cache_v = v_ctx.reshape(b_ctx, 1, n_ctx, N_HEADS, HEAD_W)
    return (y_prompt, y_sample, new_cache_k, new_cache_v)
```

```python
import functools
import math

import jax
import jax.numpy as jnp
import numpy as np
from jax import lax
from jax.experimental import pallas as pl
from jax.experimental.pallas import tpu as pltpu

D_MODEL = 1024
GRID_W = 64
N_HEADS = 8
HEAD_DIM = 64
HEAD_W = 2 * HEAD_DIM
ATTN_W = N_HEADS * HEAD_W
POOL_GROUPS = 4
POOL_WINDOWS = (2, 4, 8, 16)
POOL_W = 512
POOL_GROUP_W = 128
IN_W = 3 * ATTN_W + POOL_W + 2 * D_MODEL
ROPE_THETA = 10000.0
ROPE_PAIRS = 16
N_GROUPS = 4
EXPERTS_PER_GROUP = 4
N_EXPERTS = 16
PAIRS_PER_GROUP = 6
N_CLASSES = N_GROUPS * PAIRS_PER_GROUP
PAIR_LO = (0, 0, 0, 1, 1, 2)
PAIR_HI = (1, 2, 3, 2, 3, 3)
D_EXPERT = 512
MOE_TILE = 256
MOE_ROWS = 128
POST_ROWS = 256
POOL_HALO = 16
ADA_CHUNKS = 6
EPS = 1e-6
LAMBDA_INIT = 0.8 - 0.6 * math.exp(-0.0)

LANES = 128
MXU_W = 256
ROW_TILE = 256
CAST_PIECE = 512
ADA_TILE = 1536
ROW_SUB = D_MODEL // LANES
ROUTER_W = 2 * LANES
NEG_BIG = -1e30
Q_SCALE = math.log2(math.e) * HEAD_DIM ** -0.5

F32 = jnp.float32
BF16 = jnp.bfloat16

VMEM_LIMIT = 56 * 1024 * 1024


def _cparams(*sem):
    return pltpu.CompilerParams(dimension_semantics=sem, vmem_limit_bytes=VMEM_LIMIT)


def _split_bf16(x):
    hi = x.astype(BF16)
    lo = (x - hi.astype(F32)).astype(BF16)
    return hi, lo


def _dot(a, b):
    return jnp.dot(a, b, preferred_element_type=F32)


def _cast_weight(w_ref, wb_ref):
    for c in range(0, w_ref.shape[1], CAST_PIECE):
        wb_ref[:, c:c + CAST_PIECE] = w_ref[:, c:c + CAST_PIECE].astype(BF16)


def _adaln_kernel(ctx_ref, lat_ref, w_ref, b_ref, o_ref):
    lat = lat_ref[...]
    c = jnp.concatenate([jnp.broadcast_to(ctx_ref[...], lat.shape), lat], axis=0)
    s = c * jax.nn.sigmoid(c)
    s_hi, s_lo = _split_bf16(s)
    w_hi, w_lo = _split_bf16(w_ref[...])
    rows = s.shape[0]
    both = _dot(jnp.concatenate([s_hi, s_lo], axis=0), w_hi)
    o_ref[...] = both[:rows] + both[rows:] + _dot(s_hi, w_lo) + b_ref[...]


def _adaln(c_ctx, c_lat, w_ada, b_ada):
    assert c_lat.shape[0] % 8 == 0, "latent conditions fill whole sublane groups"
    rows = 2 * c_lat.shape[0]
    n = w_ada.shape[1]
    tn = ADA_TILE
    return pl.pallas_call(
        _adaln_kernel,
        grid=(n // tn,),
        in_specs=[pl.BlockSpec((1, D_MODEL), lambda j: (0, 0)),
                  pl.BlockSpec((c_lat.shape[0], D_MODEL), lambda j: (0, 0)),
                  pl.BlockSpec((D_MODEL, tn), lambda j: (0, j)),
                  pl.BlockSpec((1, tn), lambda j: (0, j))],
        out_specs=pl.BlockSpec((rows, tn), lambda j: (0, j)),
        out_shape=jax.ShapeDtypeStruct((rows, n), F32),
        compiler_params=_cparams("arbitrary"),
        name="adaln",
    )(c_ctx.reshape(1, D_MODEL), c_lat, w_ada, b_ada.reshape(1, n))


def _chunk_rms(z, seg):
    zz = (z * z).astype(BF16)
    parts = [_dot(zz[:, c * MXU_W:(c + 1) * MXU_W], seg) for c in range(z.shape[1] // MXU_W)]
    return jnp.concatenate(parts, axis=1)


def _rope(x, cos, sin):
    lane = lax.broadcasted_iota(jnp.int32, (x.shape[0], LANES), 1)
    first = (lane % (2 * ROPE_PAIRS)) < ROPE_PAIRS
    parts = []
    for c in range(x.shape[1] // LANES):
        xc = x[:, c * LANES:(c + 1) * LANES]
        up = pltpu.roll(xc, LANES - ROPE_PAIRS, 1)
        dn = pltpu.roll(xc, ROPE_PAIRS, 1)
        parts.append(xc * cos + jnp.where(first, -up, dn) * sin)
    return jnp.concatenate(parts, axis=1)


def _pre_kernel(*refs, rope, kv_dtype):
    if rope:
        (x_ref, mod_ref, g1_ref, wf_ref, bg_ref, qg_ref, kg_ref, seg_ref, cos_ref, sin_ref,
         q_out, k_out, v_out, p_out, g_out, w_ref) = refs
    else:
        (x_ref, mod_ref, g1_ref, wf_ref, bg_ref, qg_ref, kg_ref, seg_ref,
         q_out, k_out, v_out, p_out, g_out, w_ref) = refs

    @pl.when(pl.program_id(0) == 0)
    def _():
        _cast_weight(wf_ref, w_ref)

    x = x_ref[...]
    shift = mod_ref[0:1, :]
    scale = mod_ref[1:2, :]
    xn = x * lax.rsqrt(jnp.mean(x * x, axis=-1, keepdims=True) + EPS) * g1_ref[...]
    h = (xn * (1.0 + scale) + shift).astype(BF16)
    seg = seg_ref[...]

    zq = _dot(h, w_ref[:, 0:ATTN_W])
    qn = zq * lax.rsqrt(_chunk_rms(zq, seg) + EPS) * qg_ref[...]
    if rope:
        qn = _rope(qn, cos_ref[...], sin_ref[...])
    q_out[...] = (qn * Q_SCALE).astype(BF16)

    zk = _dot(h, w_ref[:, ATTN_W:2 * ATTN_W])
    kn = zk * lax.rsqrt(_chunk_rms(zk, seg) + EPS) * kg_ref[...]
    if rope:
        kn = _rope(kn, cos_ref[...], sin_ref[...])
    k_out[...] = kn.astype(kv_dtype)

    v_out[...] = _dot(h, w_ref[:, 2 * ATTN_W:3 * ATTN_W]).astype(kv_dtype)
    p_out[...] = _dot(h, w_ref[:, 3 * ATTN_W:3 * ATTN_W + POOL_W]).astype(BF16)
    gl = _dot(h, w_ref[:, 3 * ATTN_W + POOL_W:IN_W]) + bg_ref[...]
    g_out[...] = jax.nn.sigmoid(gl).astype(BF16)


def _pre_mixer(x, mod, lp, rope_tabs, *, seq, tm, kv_dtype):
    t = x.shape[0]
    tiles_per_seq = seq // tm
    single_mod = mod.shape[0] == 1
    mod_idx = (lambda i: (0, 0, 0)) if single_mod else (lambda i: (i // tiles_per_seq, 0, 0))
    const = lambda i: (0, 0)
    row = lambda i: (i, 0)
    in_specs = [pl.BlockSpec((tm, D_MODEL), row),
                pl.BlockSpec((None, ADA_CHUNKS, D_MODEL), mod_idx),
                pl.BlockSpec((1, D_MODEL), const),
                pl.BlockSpec((D_MODEL, IN_W), const, pipeline_mode=pl.Buffered(1)),
                pl.BlockSpec((1, 2 * D_MODEL), const),
                pl.BlockSpec((1, ATTN_W), const),
                pl.BlockSpec((1, ATTN_W), const),
                pl.BlockSpec((MXU_W, MXU_W), const)]
    args = [x, mod, lp["norm1_g"], lp["w_in"], lp["b_gate"], lp["q_gain"], lp["k_gain"], lp["seg"]]
    rope = rope_tabs is not None
    if rope:
        in_specs += [pl.BlockSpec((tm, LANES), lambda i: (i % tiles_per_seq, 0))] * 2
        args += list(rope_tabs)
    out_shape = [jax.ShapeDtypeStruct((t, ATTN_W), BF16),
                 jax.ShapeDtypeStruct((t, ATTN_W), kv_dtype),
                 jax.ShapeDtypeStruct((t, ATTN_W), kv_dtype),
                 jax.ShapeDtypeStruct((t, POOL_W), BF16),
                 jax.ShapeDtypeStruct((t, 2 * D_MODEL), BF16)]
    out_specs = [pl.BlockSpec((tm, ATTN_W), row), pl.BlockSpec((tm, ATTN_W), row),
                 pl.BlockSpec((tm, ATTN_W), row), pl.BlockSpec((tm, POOL_W), row),
                 pl.BlockSpec((tm, 2 * D_MODEL), row)]
    return pl.pallas_call(
        functools.partial(_pre_kernel, rope=rope, kv_dtype=kv_dtype),
        grid=(t // tm,),
        in_specs=in_specs, out_specs=out_specs, out_shape=out_shape,
        scratch_shapes=[pltpu.VMEM((D_MODEL, IN_W), BF16)],
        compiler_params=_cparams("arbitrary"),
        name="pre_mixer_rope" if rope else "pre_mixer",
    )(*args)


def _attn_kernel(*refs, n_q_blocks, tq, has_cache, heads_per_step):
    if has_cache:
        q_ref, k_ref, v_ref, ck_ref, cv_ref, lam_ref, sg_ref, o_ref, k1_s, k2_s, v_s = refs
    else:
        q_ref, k_ref, v_ref, lam_ref, sg_ref, o_ref, k1_s, k2_s, v_s = refs
    lv = lam_ref[...]
    lam = (jnp.exp(jnp.sum(lv[0:1] * lv[1:2], axis=-1, keepdims=True))
           - jnp.exp(jnp.sum(lv[2:3] * lv[3:4], axis=-1, keepdims=True)) + LAMBDA_INIT)
    nt = (((1,), (1,)), ((), ()))
    sg = sg_ref[...] * (1.0 - LAMBDA_INIT)
    n = k_ref.shape[0]

    def stage(hh, rows, kf, vf):
        kf = kf.astype(F32)
        lane = lax.broadcasted_iota(jnp.int32, kf.shape, 1)
        k1_s[hh, rows, :] = jnp.where(lane < HEAD_DIM, kf, 0.0).astype(BF16)
        k2_s[hh, rows, :] = jnp.where(lane >= HEAD_DIM, kf, 0.0).astype(BF16)
        v_s[hh, rows, :HEAD_W] = vf.astype(BF16)

    v_s[:, :, HEAD_W:] = jnp.ones(v_s.shape[:2] + (HEAD_W,), BF16)

    for hh in range(heads_per_step):
        cols = slice(hh * HEAD_W, (hh + 1) * HEAD_W)
        stage(hh, slice(0, n), k_ref[:, cols], v_ref[:, cols])
        if has_cache:
            head = pl.program_id(1) * heads_per_step + hh
            stage(hh, slice(n, k1_s.shape[1]), ck_ref[:, head, :], cv_ref[:, head, :])

        def softmax_av(q, k_s):
            s = lax.dot_general(q, k_s[hh], nt, preferred_element_type=F32)
            e = jnp.exp2(s - jnp.max(s, axis=-1, keepdims=True)).astype(BF16)
            ov = _dot(e, v_s[hh])
            return ov[:, :HEAD_W] / ov[:, HEAD_W:]

        def block(i, carry):
            qs = pl.multiple_of(i * tq, tq)
            q = q_ref[pl.ds(qs, tq), cols]
            o = softmax_av(q, k1_s) - lam * softmax_av(q, k2_s)
            on = o * lax.rsqrt(jnp.mean(o * o, axis=-1, keepdims=True) + EPS) * sg
            o_ref[pl.ds(qs, tq), cols] = on.astype(o_ref.dtype)
            return carry

        if n_q_blocks == 1:
            block(0, 0)
        else:
            lax.fori_loop(0, n_q_blocks, block, 0, unroll=True)


def _attention(q, k, v, cache, lam_rows, subln_g, *, tq, heads_per_step):
    b, n, _ = q.shape
    has_cache = cache is not None
    heads = lambda bi, hi: (bi, 0, hi)
    const = lambda bi, hi: (0, 0)
    in_specs = [pl.BlockSpec((None, n, heads_per_step * HEAD_W), heads)] * 3
    args = [q, k, v]
    if has_cache:
        p_len = cache[0].shape[2]
        in_specs += [pl.BlockSpec((None, None, p_len, N_HEADS, HEAD_W), lambda bi, hi: (bi, 0, 0, 0, 0))] * 2
        args += list(cache)
    in_specs += [pl.BlockSpec((4, HEAD_DIM), const), pl.BlockSpec((1, HEAD_W), const)]
    args += [lam_rows, subln_g]
    n_keys = n + (cache[0].shape[2] if has_cache else 0)
    return pl.pallas_call(
        functools.partial(_attn_kernel, n_q_blocks=n // tq, tq=tq, has_cache=has_cache,
                          heads_per_step=heads_per_step),
        grid=(b, N_HEADS // heads_per_step),
        in_specs=in_specs,
        out_specs=pl.BlockSpec((None, n, heads_per_step * HEAD_W), heads),
        out_shape=jax.ShapeDtypeStruct((b, n, ATTN_W), BF16),
        scratch_shapes=[pltpu.VMEM((heads_per_step, n_keys, HEAD_W), BF16)] * 2
        + [pltpu.VMEM((heads_per_step, n_keys, 2 * HEAD_W), BF16)],
        compiler_params=_cparams("arbitrary", "arbitrary"),
        name="diff_attn_cache" if has_cache else "diff_attn",
    )(*args)


def _route(logits):
    rows = logits.shape[0]
    lane_i = lax.broadcasted_iota(jnp.int32, (rows, LANES), 1)
    valid = lane_i < N_EXPERTS
    lane = lane_i.astype(F32)
    grp = (lane_i // EXPERTS_PER_GROUP).astype(F32)
    e_log = logits[:, :LANES]
    g_log = jnp.where(valid, logits[:, LANES:], NEG_BIG)
    g_max = jnp.max(g_log, axis=-1, keepdims=True)
    g_den = jnp.sum(jnp.exp(g_log - g_max), axis=-1, keepdims=True) * (1.0 / EXPERTS_PER_GROUP)
    g_w = 1.0 / g_den
    g_idx = jnp.min(jnp.where(g_log == g_max, grp, float(N_GROUPS)), axis=-1, keepdims=True)
    e_sel = jnp.where(grp == g_idx, jnp.where(valid, e_log, NEG_BIG), NEG_BIG)
    v1 = jnp.max(e_sel, axis=-1, keepdims=True)
    i1 = jnp.min(jnp.where(e_sel == v1, lane, float(LANES)), axis=-1, keepdims=True)
    e_rest = jnp.where(lane == i1, NEG_BIG, e_sel)
    v2 = jnp.max(e_rest, axis=-1, keepdims=True)
    i2 = jnp.min(jnp.where(e_rest == v2, lane, float(LANES)), axis=-1, keepdims=True)
    t = jnp.exp(v2 - v1)
    w1 = g_w / (1.0 + t)
    w2 = w1 * t
    first_low = i1 < i2
    a = jnp.minimum(i1, i2) - EXPERTS_PER_GROUP * g_idx
    b = jnp.maximum(i1, i2) - EXPERTS_PER_GROUP * g_idx
    pair = a * (7.0 - a) * 0.5 + (b - a - 1.0)
    cls = g_idx * PAIRS_PER_GROUP + pair
    return cls, jnp.where(first_low, w1, w2), jnp.where(first_low, w2, w1)


def _post_kernel(o_ref, p_ref, g_ref, x_ref, mod_ref, waf_ref, pw_ref, ps_ref, wpf_ref, wof_ref,
                 g2_ref, wrh_ref, wrl_ref, br_ref, x1_out, h2p_out, gw_out, route_out, counts_out,
                 carry_ref, wa_ref, wp_ref, wo_ref, *, seq, n_sub):
    step = pl.program_id(0)

    @pl.when(step == 0)
    def _():
        carry_ref[...] = jnp.zeros_like(carry_ref)
        _cast_weight(waf_ref, wa_ref)
        _cast_weight(wpf_ref, wp_ref)
        _cast_weight(wof_ref, wo_ref)

    tm = POST_ROWS
    tiles_per_seq = max(seq // (n_sub * tm), 1)
    ext = tm + 2 * POOL_HALO
    chains = range(n_sub)
    rows = [slice(s * tm, (s + 1) * tm) for s in chains]
    if seq == tm:
        blk0, t0 = [s * tm for s in chains], [0] * n_sub
    else:
        blk0 = t0 = [pl.multiple_of(((step % tiles_per_seq) * n_sub + s) * tm, tm) for s in chains]

    attn_out = [_dot(o_ref[rows[s], :], wa_ref[...]) for s in chains]

    def pooled(s):
        p_mid = p_ref[pl.ds(blk0[s], tm), :].astype(F32)
        if seq == tm:
            halo_top = halo_bot = jnp.zeros((POOL_HALO, POOL_W), F32)
        else:
            top0 = pl.multiple_of(jnp.maximum(blk0[s] - POOL_HALO, 0), POOL_HALO)
            bot0 = pl.multiple_of(jnp.minimum(blk0[s] + tm, seq - POOL_HALO), POOL_HALO)
            halo_top = p_ref[pl.ds(top0, POOL_HALO), :].astype(F32) * jnp.where(t0[s] > 0, 1.0, 0.0)
            halo_bot = p_ref[pl.ds(bot0, POOL_HALO), :].astype(F32) * jnp.where(t0[s] + tm < seq, 1.0, 0.0)
        p_ext = jnp.concatenate([halo_top, p_mid, halo_bot], axis=0)
        tok1 = t0[s] + lax.broadcasted_iota(jnp.int32, (tm, 1), 0)
        out = []
        for gi, w in enumerate(POOL_WINDOWS):
            half = w // 2
            sl = slice(gi * POOL_GROUP_W, (gi + 1) * POOL_GROUP_W)
            run = p_ext[:, sl]
            k = 1
            while k < w:
                run = run + pltpu.roll(run, ext - k, 0)
                k *= 2
            win = pltpu.roll(run, ext - (POOL_HALO - half), 0)[:tm]
            cnt = (jnp.minimum(tok1 + half, seq) - jnp.maximum(tok1 - half, 0)).astype(F32)
            out.append((win / cnt - p_mid[:, sl]).astype(BF16))
        return out

    pool_in = [pooled(s) for s in chains]
    mixed = [jnp.concatenate([_dot(pool_in[s][gi], pw_ref[gi]) for gi in range(POOL_GROUPS)], axis=1)
             * ps_ref[...] for s in chains]
    pool_out = [_dot(mixed[s].astype(BF16), wp_ref[...]) for s in chains]

    def merge(s):
        g = g_ref[rows[s], :]
        return (g[:, :D_MODEL].astype(F32) * attn_out[s] + g[:, D_MODEL:].astype(F32) * pool_out[s]).astype(BF16)

    merged = [merge(s) for s in chains]
    gate1 = mod_ref[2:3, :]
    x1 = [x_ref[rows[s], :] + gate1 * _dot(merged[s], wo_ref[...]) for s in chains]
    for s in chains:
        x1_out[rows[s], :] = x1[s]

    shift2 = mod_ref[3:4, :]
    scale2 = mod_ref[4:5, :]
    h2 = [x1[s] * lax.rsqrt(jnp.mean(x1[s] * x1[s], axis=-1, keepdims=True) + EPS) * g2_ref[...]
          * (1.0 + scale2) + shift2 for s in chains]
    h2_parts = [_split_bf16(h2[s]) for s in chains]
    for s in chains:
        h2p_out[rows[s]] = h2_parts[s][0].reshape(tm, ROW_SUB, LANES)

    logits = [_dot(h2_parts[s][0], wrh_ref[...]) + _dot(h2_parts[s][1], wrh_ref[...])
              + _dot(h2_parts[s][0], wrl_ref[...]) + br_ref[...] for s in chains]
    routes = [_route(logits[s]) for s in chains]
    lane = lax.broadcasted_iota(jnp.int32, (tm, LANES), 1)
    for s in chains:
        _, w_lo, w_hi = routes[s]
        gw_out[rows[s]] = jnp.where(lane == 0, w_lo, jnp.where(lane == 1, w_hi, 0.0)).reshape(tm, 1, LANES)

    row = lax.broadcasted_iota(jnp.int32, (tm, tm), 0)
    col = lax.broadcasted_iota(jnp.int32, (tm, tm), 1)
    before = jnp.where(col < row, 1.0, 0.0).astype(BF16)
    onehot = [jnp.where(lane.astype(F32) == routes[s][0], 1.0, 0.0) for s in chains]
    within = [_dot(before, onehot[s].astype(BF16)) for s in chains]
    for s in chains:
        rank = jnp.sum(onehot[s] * (within[s] + carry_ref[...]), axis=-1, keepdims=True)
        info = jnp.where(lane == 0, routes[s][0], jnp.where(lane == 1, rank, 0.0))
        route_out[s] = jnp.transpose(info)[:8, :].astype(jnp.int32)
        carry_ref[...] += jnp.sum(onehot[s], axis=0, keepdims=True)
    counts_out[...] = carry_ref[...]


def _post_mixer(o, p, g, x, mod, lp, *, seq, n_sub):
    t = x.shape[0]
    tm = n_sub * POST_ROWS
    tiles_per_seq = max(seq // tm, 1)
    single_mod = mod.shape[0] == 1
    mod_idx = (lambda i: (0, 0, 0)) if single_mod else (lambda i: (i // tiles_per_seq, 0, 0))
    tile = lambda i: (i, 0)
    const2 = lambda i: (0, 0)
    const3 = lambda i: (0, 0, 0)
    p_spec = (pl.BlockSpec((tm, POOL_W), tile) if seq == POST_ROWS
              else pl.BlockSpec((seq, POOL_W), lambda i: (i // tiles_per_seq, 0)))
    in_specs = [pl.BlockSpec((tm, ATTN_W), tile),
                p_spec,
                pl.BlockSpec((tm, 2 * D_MODEL), tile),
                pl.BlockSpec((tm, D_MODEL), tile),
                pl.BlockSpec((None, ADA_CHUNKS, D_MODEL), mod_idx),
                pl.BlockSpec((ATTN_W, D_MODEL), const2, pipeline_mode=pl.Buffered(1)),
                pl.BlockSpec((POOL_GROUPS, POOL_GROUP_W, POOL_GROUP_W), const3),
                pl.BlockSpec((1, POOL_W), const2),
                pl.BlockSpec((POOL_W, D_MODEL), const2, pipeline_mode=pl.Buffered(1)),
                pl.BlockSpec((D_MODEL, D_MODEL), const2, pipeline_mode=pl.Buffered(1)),
                pl.BlockSpec((1, D_MODEL), const2),
                pl.BlockSpec((D_MODEL, ROUTER_W), const2),
                pl.BlockSpec((D_MODEL, ROUTER_W), const2),
                pl.BlockSpec((1, ROUTER_W), const2)]
    tile3 = lambda i: (i, 0, 0)
    out_shape = [jax.ShapeDtypeStruct((t, D_MODEL), F32),
                 jax.ShapeDtypeStruct((t, ROW_SUB, LANES), BF16),
                 jax.ShapeDtypeStruct((t, 1, LANES), F32),
                 jax.ShapeDtypeStruct((t // POST_ROWS, 8, POST_ROWS), jnp.int32),
                 jax.ShapeDtypeStruct((1, LANES), F32)]
    out_specs = [pl.BlockSpec((tm, D_MODEL), tile),
                 pl.BlockSpec((tm, ROW_SUB, LANES), tile3),
                 pl.BlockSpec((tm, 1, LANES), tile3),
                 pl.BlockSpec((n_sub, 8, POST_ROWS), tile3),
                 pl.BlockSpec((1, LANES), const2)]
    return pl.pallas_call(
        functools.partial(_post_kernel, seq=seq, n_sub=n_sub),
        grid=(t // tm,),
        in_specs=in_specs, out_specs=out_specs, out_shape=out_shape,
        scratch_shapes=[pltpu.VMEM((1, LANES), F32),
                        pltpu.VMEM((ATTN_W, D_MODEL), BF16),
                        pltpu.VMEM((POOL_W, D_MODEL), BF16),
                        pltpu.VMEM((D_MODEL, D_MODEL), BF16)],
        compiler_params=_cparams("arbitrary"),
        name="post_mixer",
    )(o, p, g, x, mod, lp["w_br_attn"], lp["pool_w"], lp["pool_scale"], lp["w_br_pool"], lp["w_out"],
      lp["norm2_g"], lp["w_router_hi"], lp["w_router_lo"], lp["b_router"])


def _moe_kernel(cls_s, rank_s, cnt_s, h_hbm, gw_hbm, wg_hbm, wu_hbm, wd_hbm, o_ref, dest_s,
                src_s, off_s, elo_s, nlo_s, ehi_s, nhi_s, valid_s, nt_s,
                h_ref, gw_ref, xg_ref, gwg_ref, wg_stage, wu_stage, wd_stage, wg_buf, wu_buf, wd_buf,
                sems, in_sems, *, n_tokens):
    j = pl.program_id(0)

    stages = ((wg_hbm, wg_stage, wg_buf), (wu_hbm, wu_stage, wu_buf), (wd_hbm, wd_stage, wd_buf))

    def weight_copies(e, role):
        return [pltpu.make_async_copy(hbm.at[e], stage.at[role], sems.at[role, i])
                for i, (hbm, stage, _) in enumerate(stages)]

    def cast_weights(role):
        for _, stage, buf in stages:
            for r in range(0, stage.shape[1], CAST_PIECE // 2):
                buf[role, r:r + CAST_PIECE // 2, :] = stage[role, r:r + CAST_PIECE // 2, :].astype(BF16)

    token_copies = [pltpu.make_async_copy(h_hbm, h_ref, in_sems.at[0]),
                    pltpu.make_async_copy(gw_hbm, gw_ref, in_sems.at[1])]
    roles = ((0, elo_s, nlo_s), (1, ehi_s, nhi_s))

    def gather_tile(tile, slot):
        base = tile * MOE_TILE
        for r in range(MOE_TILE):
            t = src_s[base + r]
            xg_ref[slot, r] = h_ref[t]
            gwg_ref[slot, r] = gw_ref[t]

    @pl.when(j == 0)
    def _():
        for cp in token_copies:
            cp.start()

        tile = 0
        for c in range(N_CLASSES):
            n_tok = cnt_s[c]
            n_cls_tiles = (n_tok + MOE_TILE - 1) // MOE_TILE
            off_s[c] = tile * MOE_TILE
            e_lo = (c // PAIRS_PER_GROUP) * EXPERTS_PER_GROUP + PAIR_LO[c % PAIRS_PER_GROUP]
            e_hi = (c // PAIRS_PER_GROUP) * EXPERTS_PER_GROUP + PAIR_HI[c % PAIRS_PER_GROUP]

            def fill(k, carry, tile=tile, n_tok=n_tok, e_lo=e_lo, e_hi=e_hi):
                elo_s[tile + k] = e_lo
                ehi_s[tile + k] = e_hi
                valid_s[tile + k] = jnp.minimum(n_tok - k * MOE_TILE, MOE_TILE)
                return carry

            lax.fori_loop(0, n_cls_tiles, fill, 0)
            tile = tile + n_cls_tiles
        off_s[N_CLASSES] = tile * MOE_TILE
        nt_s[0] = tile
        n_tiles = tile

        for role, e_s, nxt_s in roles:
            for cp in weight_copies(e_s[0], role):
                cp.start()

            def backward(i, nxt, e_s=e_s, nxt_s=nxt_s):
                t = n_tiles - 1 - i
                after = jnp.minimum(t + 1, n_tiles - 1)
                nxt = jnp.where((t + 1 < n_tiles) & (e_s[after] != e_s[t]), e_s[after], nxt)
                nxt_s[t] = nxt
                return nxt

            lax.fori_loop(0, n_tiles, backward, -1)

        def clear_tail(c, carry):
            start = jnp.maximum(off_s[c + 1] - MOE_TILE, 0)
            for i in range(MOE_TILE):
                src_s[start + i] = 0
            return carry

        lax.fori_loop(0, N_CLASSES, clear_tail, 0)

        def place(t, c):
            row = off_s[cls_s[t]] + rank_s[t]
            src_s[row] = t
            dest_s[t] = row
            return c

        lax.fori_loop(0, n_tokens, place, 0, unroll=16)
        for cp in token_copies:
            cp.wait()
        gather_tile(0, 0)

    n_tiles = nt_s[0]

    @pl.when(j < n_tiles)
    def _():
        for role, e_s, nxt_s in roles:
            run_starts = (j == 0) | (e_s[j] != e_s[jnp.maximum(j - 1, 0)])

            @pl.when(run_starts)
            def _():
                for cp in weight_copies(e_s[j], role):
                    cp.wait()
                cast_weights(role)

                @pl.when(nxt_s[j] >= 0)
                def _():
                    for cp in weight_copies(nxt_s[j], role):
                        cp.start()

        slot = j % 2
        gather_tile(jnp.minimum(j + 1, n_tiles - 1), 1 - slot)

        def experts(rows):
            x = xg_ref[slot, :rows].reshape(rows, D_MODEL)
            gw = gwg_ref[slot, :rows].reshape(rows, LANES)
            a1, a2 = _dot(x, wg_buf[0]), _dot(x, wg_buf[1])
            u1, u2 = _dot(x, wu_buf[0]), _dot(x, wu_buf[1])
            hid1 = (a1 * jax.nn.sigmoid(a1) * u1 * gw[:, 0:1]).astype(BF16)
            hid2 = (a2 * jax.nn.sigmoid(a2) * u2 * gw[:, 1:2]).astype(BF16)
            o = _dot(hid1, wd_buf[0]) + _dot(hid2, wd_buf[1])
            o_ref[:rows] = o.astype(BF16).reshape(rows, ROW_SUB, LANES)
            if rows < MOE_TILE:
                o_ref[rows:] = jnp.zeros((MOE_TILE - rows, ROW_SUB, LANES), BF16)

        pieces = (valid_s[j] + MOE_ROWS - 1) // MOE_ROWS
        for n in range(1, MOE_TILE // MOE_ROWS + 1):
            @pl.when(pieces == n)
            def _():
                experts(n * MOE_ROWS)

    @pl.when(j >= n_tiles)
    def _():
        o_ref[...] = jnp.zeros_like(o_ref)


def _moe(h2p, gw, cls, rank, cnt, lp):
    t, sub, _ = h2p.shape
    max_tiles = t // MOE_TILE + N_CLASSES
    in_hbm = pl.BlockSpec(memory_space=pl.ANY)
    tile_table = pltpu.SMEM((max_tiles,), jnp.int32)
    grid_spec = pltpu.PrefetchScalarGridSpec(
        num_scalar_prefetch=3,
        grid=(max_tiles,),
        in_specs=[in_hbm] * 5,
        out_specs=[pl.BlockSpec((MOE_TILE, sub, LANES), lambda j, *_: (j, 0, 0)),
                   pl.BlockSpec(memory_space=pltpu.SMEM)],
        scratch_shapes=[pltpu.SMEM((max_tiles * MOE_TILE,), jnp.int32),
                        pltpu.SMEM((N_CLASSES + 1,), jnp.int32)]
        + [tile_table] * 5
        + [pltpu.SMEM((1,), jnp.int32),
           pltpu.VMEM((t, sub, LANES), BF16),
           pltpu.VMEM((t, 1, LANES), F32),
           pltpu.VMEM((2, MOE_TILE, sub, LANES), BF16),
           pltpu.VMEM((2, MOE_TILE, 1, LANES), F32),
           pltpu.VMEM((2, D_MODEL, D_EXPERT), F32),
           pltpu.VMEM((2, D_MODEL, D_EXPERT), F32),
           pltpu.VMEM((2, D_EXPERT, D_MODEL), F32),
           pltpu.VMEM((2, D_MODEL, D_EXPERT), BF16),
           pltpu.VMEM((2, D_MODEL, D_EXPERT), BF16),
           pltpu.VMEM((2, D_EXPERT, D_MODEL), BF16),
           pltpu.SemaphoreType.DMA((2, 3)),
           pltpu.SemaphoreType.DMA((2,))])
    return pl.pallas_call(
        functools.partial(_moe_kernel, n_tokens=t),
        grid_spec=grid_spec,
        out_shape=[jax.ShapeDtypeStruct((max_tiles * MOE_TILE, sub, LANES), BF16),
                   jax.ShapeDtypeStruct((t,), jnp.int32)],
        compiler_params=_cparams("arbitrary"),
        name="moe",
    )(cls, rank, cnt, h2p, gw, lp["expert_w_gate"], lp["expert_w_up"], lp["expert_w_down"])


def _combine_kernel(dest_s, o_ref, x1_ref, mod_ref, y_ref, og_ref, *, tm):
    base = pl.program_id(0) * tm

    def gather(r, c):
        og_ref[r] = o_ref[dest_s[base + r]]
        return c

    lax.fori_loop(0, tm, gather, 0, unroll=32)
    moe = og_ref[...].reshape(tm, D_MODEL).astype(F32)
    y_ref[...] = x1_ref[...] + mod_ref[5:6, :] * moe


def _combine(o_sorted, x1, mod, dest, *, seq, tm):
    t = x1.shape[0]
    rows, sub, _ = o_sorted.shape
    tiles_per_seq = seq // tm
    single_mod = mod.shape[0] == 1
    mod_idx = (lambda i, *_: (0, 0, 0)) if single_mod else (lambda i, *_: (i // tiles_per_seq, 0, 0))
    row = lambda i, *_: (i, 0)
    grid_spec = pltpu.PrefetchScalarGridSpec(
        num_scalar_prefetch=1,
        grid=(t // tm,),
        in_specs=[pl.BlockSpec((rows, sub, LANES), lambda i, *_: (0, 0, 0), pipeline_mode=pl.Buffered(1)),
                  pl.BlockSpec((tm, D_MODEL), row),
                  pl.BlockSpec((None, ADA_CHUNKS, D_MODEL), mod_idx)],
        out_specs=pl.BlockSpec((tm, D_MODEL), row),
        scratch_shapes=[pltpu.VMEM((tm, sub, LANES), BF16)])
    return pl.pallas_call(
        functools.partial(_combine_kernel, tm=tm),
        grid_spec=grid_spec,
        out_shape=jax.ShapeDtypeStruct((t, D_MODEL), F32),
        compiler_params=_cparams("arbitrary"),
        name="moe_combine",
    )(dest, o_sorted, x1, mod)


def _rope_tables(n_tokens):
    rows = n_tokens // GRID_W
    row_ids = np.repeat(np.arange(rows, dtype=np.float32), GRID_W)
    col_ids = np.tile(np.arange(GRID_W, dtype=np.float32), rows)
    inv_freq = np.power(np.float32(ROPE_THETA), -np.arange(ROPE_PAIRS, dtype=np.float32) / np.float32(ROPE_PAIRS))
    ang_r = row_ids[:, None] * inv_freq[None, :]
    ang_c = col_ids[:, None] * inv_freq[None, :]
    ang = np.concatenate([ang_r, ang_r, ang_c, ang_c] * 2, axis=-1)
    return jnp.asarray(np.cos(ang), F32), jnp.asarray(np.sin(ang), F32)


def _layer(x, mod, lp, rope_tabs, cache, lam_rows, *, kv_dtype):
    b, n, _ = x.shape
    t = b * n
    q, k, v, p, g = _pre_mixer(x.reshape(t, D_MODEL), mod, lp, rope_tabs, seq=n, tm=ROW_TILE, kv_dtype=kv_dtype)
    o = _attention(q.reshape(b, n, ATTN_W), k.reshape(b, n, ATTN_W), v.reshape(b, n, ATTN_W),
                   cache, lam_rows, lp["subln_g"], tq=ROW_TILE, heads_per_step=N_HEADS if cache is None else 2)
    x1, h2p, gw, route, counts = _post_mixer(o.reshape(t, ATTN_W), p, g, x.reshape(t, D_MODEL), mod, lp,
                                             seq=n, n_sub=2)
    cls, rank = route[:, 0, :].reshape(t), route[:, 1, :].reshape(t)
    o_sorted, dest = _moe(h2p, gw, cls, rank, counts[0, :N_CLASSES].astype(jnp.int32), lp)
    y = _combine(o_sorted, x1, mod, dest, seq=n, tm=ROW_TILE)
    return y.reshape(b, n, D_MODEL), k, v


def kernel(x_prompt, x_sample, c, cache_k, cache_v, c_ctx, w_ada, b_ada, norm1_g, w_in, b_gate, q_norm_g, k_norm_g, lambda_q1, lambda_k1, lambda_q2, lambda_k2, subln_g, pool_w, pool_scale, w_br_attn, w_br_pool, w_out, norm2_g, router_group_w, router_group_b, router_expert_w, router_expert_b, expert_w_gate, expert_w_up, expert_w_down):
    b_ctx, n_ctx, _ = x_prompt.shape
    b_lat, n_lat, _ = x_sample.shape
    assert w_ada.shape[0] == 1 and cache_k.shape[1] == 1, "one layer"
    assert n_ctx == POST_ROWS and n_lat % (2 * POST_ROWS) == 0 and n_lat % GRID_W == 0

    def router_layout(we, wg):
        pad = jnp.zeros(we.shape[:-1] + (LANES - N_EXPERTS,), F32)
        return jnp.concatenate([we, pad, jnp.repeat(wg, EXPERTS_PER_GROUP, axis=-1), pad], axis=-1)

    w_router = router_layout(router_expert_w[0], router_group_w[0])
    w_router_hi, w_router_lo = _split_bf16(w_router)
    seg = jnp.asarray(np.arange(MXU_W)[:, None] // HEAD_DIM == np.arange(MXU_W)[None, :] // HEAD_DIM)
    lp = dict(
        norm1_g=norm1_g[0].reshape(1, D_MODEL),
        w_in=w_in[0],
        b_gate=b_gate[0].reshape(1, 2 * D_MODEL),
        q_gain=jnp.tile(q_norm_g[0], ATTN_W // HEAD_DIM).reshape(1, ATTN_W),
        k_gain=jnp.tile(k_norm_g[0], ATTN_W // HEAD_DIM).reshape(1, ATTN_W),
        seg=(seg.astype(F32) / HEAD_DIM).astype(BF16),
        subln_g=subln_g[0].reshape(1, HEAD_W),
        w_br_attn=w_br_attn[0],
        pool_w=pool_w[0].astype(BF16),
        pool_scale=pool_scale[0].reshape(1, POOL_W),
        w_br_pool=w_br_pool[0],
        w_out=w_out[0],
        norm2_g=norm2_g[0].reshape(1, D_MODEL),
        w_router_hi=w_router_hi, w_router_lo=w_router_lo,
        b_router=router_layout(router_expert_b[0], router_group_b[0]).reshape(1, ROUTER_W),
        expert_w_gate=expert_w_gate[0],
        expert_w_up=expert_w_up[0],
        expert_w_down=expert_w_down[0],
    )
    lam_rows = jnp.stack([lambda_q1[0], lambda_k1[0], lambda_q2[0], lambda_k2[0]], axis=0)

    mod = _adaln(c_ctx, c, w_ada[0], b_ada[0]).reshape(2 * b_lat, ADA_CHUNKS, D_MODEL)

    y_prompt, k_ctx, v_ctx = _layer(x_prompt, mod[:1], lp, None, None, lam_rows, kv_dtype=F32)
    cache = (cache_k, cache_v)
    y_sample, _, _ = _layer(x_sample, mod[b_lat:], lp, _rope_tables(n_lat), cache, lam_rows, kv_dtype=BF16)

    new_cache_k = k_ctx.reshape(b_ctx, 1, n_ctx, N_HEADS, HEAD_W)
    new_cache_v = v_ctx.reshape(b_ctx, 1, n_ctx, N_HEADS, HEAD_W)
    return (y_prompt, y_sample, new_cache_k, new_cache_v)
```

```python
import functools
import math

import jax
import jax.numpy as jnp
import numpy as np
from jax import lax
from jax.experimental import pallas as pl
from jax.experimental.pallas import tpu as pltpu

D_MODEL = 1024
GRID_W = 64
N_HEADS = 8
HEAD_DIM = 64
HEAD_W = 2 * HEAD_DIM
ATTN_W = N_HEADS * HEAD_W
POOL_GROUPS = 4
POOL_WINDOWS = (2, 4, 8, 16)
POOL_W = 512
POOL_GROUP_W = 128
IN_W = 3 * ATTN_W + POOL_W + 2 * D_MODEL
ROPE_THETA = 10000.0
ROPE_PAIRS = 16
N_GROUPS = 4
EXPERTS_PER_GROUP = 4
N_EXPERTS = 16
PAIRS_PER_GROUP = 6
N_CLASSES = N_GROUPS * PAIRS_PER_GROUP
PAIR_LO = (0, 0, 0, 1, 1, 2)
PAIR_HI = (1, 2, 3, 2, 3, 3)
D_EXPERT = 512
MOE_TILE = 256
MOE_ROWS = 128
POST_ROWS = 256
POOL_HALO = 16
ADA_CHUNKS = 6
EPS = 1e-6
LAMBDA_INIT = 0.8 - 0.6 * math.exp(-0.0)

LANES = 128
MXU_W = 256
ROW_TILE = 256
CAST_PIECE = 512
ADA_TILE = 1536
ROW_SUB = D_MODEL // LANES
ROUTER_W = 2 * LANES
NEG_BIG = -1e30
Q_SCALE = math.log2(math.e) * HEAD_DIM ** -0.5

F32 = jnp.float32
BF16 = jnp.bfloat16

VMEM_LIMIT = 56 * 1024 * 1024


def _cparams(*sem):
    return pltpu.CompilerParams(dimension_semantics=sem, vmem_limit_bytes=VMEM_LIMIT)


def _split_bf16(x):
    hi = x.astype(BF16)
    lo = (x - hi.astype(F32)).astype(BF16)
    return hi, lo


def _dot(a, b):
    return jnp.dot(a, b, preferred_element_type=F32)


def _cast_weight(w_ref, wb_ref):
    for c in range(0, w_ref.shape[1], CAST_PIECE):
        wb_ref[:, c:c + CAST_PIECE] = w_ref[:, c:c + CAST_PIECE].astype(BF16)


def _adaln_kernel(ctx_ref, lat_ref, w_ref, b_ref, o_ref):
    lat = lat_ref[...]
    c = jnp.concatenate([jnp.broadcast_to(ctx_ref[...], lat.shape), lat], axis=0)
    s = c * jax.nn.sigmoid(c)
    s_hi, s_lo = _split_bf16(s)
    w_hi, w_lo = _split_bf16(w_ref[...])
    rows = s.shape[0]
    both = _dot(jnp.concatenate([s_hi, s_lo], axis=0), w_hi)
    o_ref[...] = both[:rows] + both[rows:] + _dot(s_hi, w_lo) + b_ref[...]


def _adaln(c_ctx, c_lat, w_ada, b_ada):
    assert c_lat.shape[0] % 8 == 0, "latent conditions fill whole sublane groups"
    rows = 2 * c_lat.shape[0]
    n = w_ada.shape[1]
    tn = ADA_TILE
    return pl.pallas_call(
        _adaln_kernel,
        grid=(n // tn,),
        in_specs=[pl.BlockSpec((1, D_MODEL), lambda j: (0, 0)),
                  pl.BlockSpec((c_lat.shape[0], D_MODEL), lambda j: (0, 0)),
                  pl.BlockSpec((D_MODEL, tn), lambda j: (0, j)),
                  pl.BlockSpec((1, tn), lambda j: (0, j))],
        out_specs=pl.BlockSpec((rows, tn), lambda j: (0, j)),
        out_shape=jax.ShapeDtypeStruct((rows, n), F32),
        compiler_params=_cparams("arbitrary"),
        name="adaln",
    )(c_ctx.reshape(1, D_MODEL), c_lat, w_ada, b_ada.reshape(1, n))


def _chunk_rms(z, seg):
    zz = (z * z).astype(BF16)
    parts = [_dot(zz[:, c * MXU_W:(c + 1) * MXU_W], seg) for c in range(z.shape[1] // MXU_W)]
    return jnp.concatenate(parts, axis=1)


def _rope(x, cos, sin):
    lane = lax.broadcasted_iota(jnp.int32, (x.shape[0], LANES), 1)
    first = (lane % (2 * ROPE_PAIRS)) < ROPE_PAIRS
    parts = []
    for c in range(x.shape[1] // LANES):
        xc = x[:, c * LANES:(c + 1) * LANES]
        up = pltpu.roll(xc, LANES - ROPE_PAIRS, 1)
        dn = pltpu.roll(xc, ROPE_PAIRS, 1)
        parts.append(xc * cos + jnp.where(first, -up, dn) * sin)
    return jnp.concatenate(parts, axis=1)


def _pre_kernel(*refs, rope, kv_dtype):
    if rope:
        (x_ref, mod_ref, g1_ref, wf_ref, bg_ref, qg_ref, kg_ref, seg_ref, cos_ref, sin_ref,
         q_out, k_out, v_out, p_out, g_out, w_ref) = refs
    else:
        (x_ref, mod_ref, g1_ref, wf_ref, bg_ref, qg_ref, kg_ref, seg_ref,
         q_out, k_out, v_out, p_out, g_out, w_ref) = refs

    @pl.when(pl.program_id(0) == 0)
    def _():
        _cast_weight(wf_ref, w_ref)

    x = x_ref[...]
    shift = mod_ref[0:1, :]
    scale = mod_ref[1:2, :]
    xn = x * lax.rsqrt(jnp.mean(x * x, axis=-1, keepdims=True) + EPS) * g1_ref[...]
    h = (xn * (1.0 + scale) + shift).astype(BF16)
    seg = seg_ref[...]

    zq = _dot(h, w_ref[:, 0:ATTN_W])
    qn = zq * lax.rsqrt(_chunk_rms(zq, seg) + EPS) * qg_ref[...]
    if rope:
        qn = _rope(qn, cos_ref[...], sin_ref[...])
    q_out[...] = (qn * Q_SCALE).astype(BF16)

    zk = _dot(h, w_ref[:, ATTN_W:2 * ATTN_W])
    kn = zk * lax.rsqrt(_chunk_rms(zk, seg) + EPS) * kg_ref[...]
    if rope:
        kn = _rope(kn, cos_ref[...], sin_ref[...])
    k_out[...] = kn.astype(kv_dtype)

    v_out[...] = _dot(h, w_ref[:, 2 * ATTN_W:3 * ATTN_W]).astype(kv_dtype)
    p_out[...] = _dot(h, w_ref[:, 3 * ATTN_W:3 * ATTN_W + POOL_W]).astype(BF16)
    gl = _dot(h, w_ref[:, 3 * ATTN_W + POOL_W:IN_W]) + bg_ref[...]
    g_out[...] = jax.nn.sigmoid(gl).astype(BF16)


def _pre_mixer(x, mod, lp, rope_tabs, *, seq, tm, kv_dtype):
    t = x.shape[0]
    tiles_per_seq = seq // tm
    single_mod = mod.shape[0] == 1
    mod_idx = (lambda i: (0, 0, 0)) if single_mod else (lambda i: (i // tiles_per_seq, 0, 0))
    const = lambda i: (0, 0)
    row = lambda i: (i, 0)
    in_specs = [pl.BlockSpec((tm, D_MODEL), row),
                pl.BlockSpec((None, ADA_CHUNKS, D_MODEL), mod_idx),
                pl.BlockSpec((1, D_MODEL), const),
                pl.BlockSpec((D_MODEL, IN_W), const, pipeline_mode=pl.Buffered(1)),
                pl.BlockSpec((1, 2 * D_MODEL), const),
                pl.BlockSpec((1, ATTN_W), const),
                pl.BlockSpec((1, ATTN_W), const),
                pl.BlockSpec((MXU_W, MXU_W), const)]
    args = [x, mod, lp["norm1_g"], lp["w_in"], lp["b_gate"], lp["q_gain"], lp["k_gain"], lp["seg"]]
    rope = rope_tabs is not None
    if rope:
        in_specs += [pl.BlockSpec((tm, LANES), lambda i: (i % tiles_per_seq, 0))] * 2
        args += list(rope_tabs)
    out_shape = [jax.ShapeDtypeStruct((t, ATTN_W), BF16),
                 jax.ShapeDtypeStruct((t, ATTN_W), kv_dtype),
                 jax.ShapeDtypeStruct((t, ATTN_W), kv_dtype),
                 jax.ShapeDtypeStruct((t, POOL_W), BF16),
                 jax.ShapeDtypeStruct((t, 2 * D_MODEL), BF16)]
    out_specs = [pl.BlockSpec((tm, ATTN_W), row), pl.BlockSpec((tm, ATTN_W), row),
                 pl.BlockSpec((tm, ATTN_W), row), pl.BlockSpec((tm, POOL_W), row),
                 pl.BlockSpec((tm, 2 * D_MODEL), row)]
    return pl.pallas_call(
        functools.partial(_pre_kernel, rope=rope, kv_dtype=kv_dtype),
        grid=(t // tm,),
        in_specs=in_specs, out_specs=out_specs, out_shape=out_shape,
        scratch_shapes=[pltpu.VMEM((D_MODEL, IN_W), BF16)],
        compiler_params=_cparams("arbitrary"),
        name="pre_mixer_rope" if rope else "pre_mixer",
    )(*args)


def _attn_kernel(*refs, n_q_blocks, tq, has_cache, heads_per_step):
    if has_cache:
        q_ref, k_ref, v_ref, ck_ref, cv_ref, lam_ref, sg_ref, o_ref, k1_s, k2_s, v_s = refs
    else:
        q_ref, k_ref, v_ref, lam_ref, sg_ref, o_ref, k1_s, k2_s, v_s = refs
    lv = lam_ref[...]
    lam = (jnp.exp(jnp.sum(lv[0:1] * lv[1:2], axis=-1, keepdims=True))
           - jnp.exp(jnp.sum(lv[2:3] * lv[3:4], axis=-1, keepdims=True)) + LAMBDA_INIT)
    nt = (((1,), (1,)), ((), ()))
    sg = sg_ref[...] * (1.0 - LAMBDA_INIT)
    n = k_ref.shape[0]

    def stage(hh, rows, kf, vf):
        kf = kf.astype(F32)
        lane = lax.broadcasted_iota(jnp.int32, kf.shape, 1)
        k1_s[hh, rows, :] = jnp.where(lane < HEAD_DIM, kf, 0.0).astype(BF16)
        k2_s[hh, rows, :] = jnp.where(lane >= HEAD_DIM, kf, 0.0).astype(BF16)
        v_s[hh, rows, :HEAD_W] = vf.astype(BF16)

    v_s[:, :, HEAD_W:] = jnp.ones(v_s.shape[:2] + (HEAD_W,), BF16)

    for hh in range(heads_per_step):
        cols = slice(hh * HEAD_W, (hh + 1) * HEAD_W)
        stage(hh, slice(0, n), k_ref[:, cols], v_ref[:, cols])
        if has_cache:
            head = pl.program_id(1) * heads_per_step + hh
            stage(hh, slice(n, k1_s.shape[1]), ck_ref[:, head, :], cv_ref[:, head, :])

        def softmax_av(q, k_s):
            s = lax.dot_general(q, k_s[hh], nt, preferred_element_type=F32)
            e = jnp.exp2(s - jnp.max(s, axis=-1, keepdims=True)).astype(BF16)
            ov = _dot(e, v_s[hh])
            return ov[:, :HEAD_W] / ov[:, HEAD_W:]

        def block(i, carry):
            qs = pl.multiple_of(i * tq, tq)
            q = q_ref[pl.ds(qs, tq), cols]
            o = softmax_av(q, k1_s) - lam * softmax_av(q, k2_s)
            on = o * lax.rsqrt(jnp.mean(o * o, axis=-1, keepdims=True) + EPS) * sg
            o_ref[pl.ds(qs, tq), cols] = on.astype(o_ref.dtype)
            return carry

        if n_q_blocks == 1:
            block(0, 0)
        else:
            lax.fori_loop(0, n_q_blocks, block, 0, unroll=True)


def _attention(q, k, v, cache, lam_rows, subln_g, *, tq, heads_per_step):
    b, n, _ = q.shape
    has_cache = cache is not None
    heads = lambda bi, hi: (bi, 0, hi)
    const = lambda bi, hi: (0, 0)
    in_specs = [pl.BlockSpec((None, n, heads_per_step * HEAD_W), heads)] * 3
    args = [q, k, v]
    if has_cache:
        p_len = cache[0].shape[2]
        in_specs += [pl.BlockSpec((None, None, p_len, N_HEADS, HEAD_W), lambda bi, hi: (bi, 0, 0, 0, 0))] * 2
        args += list(cache)
    in_specs += [pl.BlockSpec((4, HEAD_DIM), const), pl.BlockSpec((1, HEAD_W), const)]
    args += [lam_rows, subln_g]
    n_keys = n + (cache[0].shape[2] if has_cache else 0)
    return pl.pallas_call(
        functools.partial(_attn_kernel, n_q_blocks=n // tq, tq=tq, has_cache=has_cache,
                          heads_per_step=heads_per_step),
        grid=(b, N_HEADS // heads_per_step),
        in_specs=in_specs,
        out_specs=pl.BlockSpec((None, n, heads_per_step * HEAD_W), heads),
        out_shape=jax.ShapeDtypeStruct((b, n, ATTN_W), BF16),
        scratch_shapes=[pltpu.VMEM((heads_per_step, n_keys, HEAD_W), BF16)] * 2
        + [pltpu.VMEM((heads_per_step, n_keys, 2 * HEAD_W), BF16)],
        compiler_params=_cparams("arbitrary", "arbitrary"),
        name="diff_attn_cache" if has_cache else "diff_attn",
    )(*args)


def _route(logits):
    rows = logits.shape[0]
    lane_i = lax.broadcasted_iota(jnp.int32, (rows, LANES), 1)
    valid = lane_i < N_EXPERTS
    lane = lane_i.astype(F32)
    grp = (lane_i // EXPERTS_PER_GROUP).astype(F32)
    e_log = logits[:, :LANES]
    g_log = jnp.where(valid, logits[:, LANES:], NEG_BIG)
    g_max = jnp.max(g_log, axis=-1, keepdims=True)
    g_den = jnp.sum(jnp.exp(g_log - g_max), axis=-1, keepdims=True) * (1.0 / EXPERTS_PER_GROUP)
    g_w = 1.0 / g_den
    g_idx = jnp.min(jnp.where(g_log == g_max, grp, float(N_GROUPS)), axis=-1, keepdims=True)
    e_sel = jnp.where(grp == g_idx, jnp.where(valid, e_log, NEG_BIG), NEG_BIG)
    v1 = jnp.max(e_sel, axis=-1, keepdims=True)
    i1 = jnp.min(jnp.where(e_sel == v1, lane, float(LANES)), axis=-1, keepdims=True)
    e_rest = jnp.where(lane == i1, NEG_BIG, e_sel)
    v2 = jnp.max(e_rest, axis=-1, keepdims=True)
    i2 = jnp.min(jnp.where(e_rest == v2, lane, float(LANES)), axis=-1, keepdims=True)
    t = jnp.exp(v2 - v1)
    w1 = g_w / (1.0 + t)
    w2 = w1 * t
    first_low = i1 < i2
    a = jnp.minimum(i1, i2) - EXPERTS_PER_GROUP * g_idx
    b = jnp.maximum(i1, i2) - EXPERTS_PER_GROUP * g_idx
    pair = a * (7.0 - a) * 0.5 + (b - a - 1.0)
    cls = g_idx * PAIRS_PER_GROUP + pair
    return cls, jnp.where(first_low, w1, w2), jnp.where(first_low, w2, w1)


def _post_kernel(o_ref, p_ref, g_ref, x_ref, mod_ref, waf_ref, pw_ref, ps_ref, wpf_ref, wof_ref,
                 g2_ref, wrh_ref, wrl_ref, br_ref, x1_out, h2p_out, gw_out, route_out, counts_out,
                 carry_ref, wa_ref, wp_ref, wo_ref, *, seq, n_sub):
    step = pl.program_id(0)

    @pl.when(step == 0)
    def _():
        carry_ref[...] = jnp.zeros_like(carry_ref)
        _cast_weight(waf_ref, wa_ref)
        _cast_weight(wpf_ref, wp_ref)
        _cast_weight(wof_ref, wo_ref)

    tm = POST_ROWS
    tiles_per_seq = max(seq // (n_sub * tm), 1)
    ext = tm + 2 * POOL_HALO
    chains = range(n_sub)
    rows = [slice(s * tm, (s + 1) * tm) for s in chains]
    if seq == tm:
        blk0, t0 = [s * tm for s in chains], [0] * n_sub
    else:
        blk0 = t0 = [pl.multiple_of(((step % tiles_per_seq) * n_sub + s) * tm, tm) for s in chains]

    attn_out = [_dot(o_ref[rows[s], :], wa_ref[...]) for s in chains]

    def pooled(s):
        p_mid = p_ref[pl.ds(blk0[s], tm), :].astype(F32)
        if seq == tm:
            halo_top = halo_bot = jnp.zeros((POOL_HALO, POOL_W), F32)
        else:
            top0 = pl.multiple_of(jnp.maximum(blk0[s] - POOL_HALO, 0), POOL_HALO)
            bot0 = pl.multiple_of(jnp.minimum(blk0[s] + tm, seq - POOL_HALO), POOL_HALO)
            halo_top = p_ref[pl.ds(top0, POOL_HALO), :].astype(F32) * jnp.where(t0[s] > 0, 1.0, 0.0)
            halo_bot = p_ref[pl.ds(bot0, POOL_HALO), :].astype(F32) * jnp.where(t0[s] + tm < seq, 1.0, 0.0)
        p_ext = jnp.concatenate([halo_top, p_mid, halo_bot], axis=0)
        tok1 = t0[s] + lax.broadcasted_iota(jnp.int32, (tm, 1), 0)
        out = []
        for gi, w in enumerate(POOL_WINDOWS):
            half = w // 2
            sl = slice(gi * POOL_GROUP_W, (gi + 1) * POOL_GROUP_W)
            run = p_ext[:, sl]
            k = 1
            while k < w:
                run = run + pltpu.roll(run, ext - k, 0)
                k *= 2
            win = pltpu.roll(run, ext - (POOL_HALO - half), 0)[:tm]
            cnt = (jnp.minimum(tok1 + half, seq) - jnp.maximum(tok1 - half, 0)).astype(F32)
            out.append((win / cnt - p_mid[:, sl]).astype(BF16))
        return out

    pool_in = [pooled(s) for s in chains]
    mixed = [jnp.concatenate([_dot(pool_in[s][gi], pw_ref[gi]) for gi in range(POOL_GROUPS)], axis=1)
             * ps_ref[...] for s in chains]
    pool_out = [_dot(mixed[s].astype(BF16), wp_ref[...]) for s in chains]

    def merge(s):
        g = g_ref[rows[s], :]
        return (g[:, :D_MODEL].astype(F32) * attn_out[s] + g[:, D_MODEL:].astype(F32) * pool_out[s]).astype(BF16)

    merged = [merge(s) for s in chains]
    gate1 = mod_ref[2:3, :]
    x1 = [x_ref[rows[s], :] + gate1 * _dot(merged[s], wo_ref[...]) for s in chains]
    for s in chains:
        x1_out[rows[s], :] = x1[s]

    shift2 = mod_ref[3:4, :]
    scale2 = mod_ref[4:5, :]
    h2 = [x1[s] * lax.rsqrt(jnp.mean(x1[s] * x1[s], axis=-1, keepdims=True) + EPS) * g2_ref[...]
          * (1.0 + scale2) + shift2 for s in chains]
    h2_parts = [_split_bf16(h2[s]) for s in chains]
    for s in chains:
        h2p_out[rows[s]] = h2_parts[s][0].reshape(tm, ROW_SUB, LANES)

    logits = [_dot(h2_parts[s][0], wrh_ref[...]) + _dot(h2_parts[s][1], wrh_ref[...])
              + _dot(h2_parts[s][0], wrl_ref[...]) + br_ref[...] for s in chains]
    routes = [_route(logits[s]) for s in chains]
    lane = lax.broadcasted_iota(jnp.int32, (tm, LANES), 1)
    for s in chains:
        _, w_lo, w_hi = routes[s]
        gw_out[rows[s]] = jnp.where(lane == 0, w_lo, jnp.where(lane == 1, w_hi, 0.0)).reshape(tm, 1, LANES)

    row = lax.broadcasted_iota(jnp.int32, (tm, tm), 0)
    col = lax.broadcasted_iota(jnp.int32, (tm, tm), 1)
    before = jnp.where(col < row, 1.0, 0.0).astype(BF16)
    onehot = [jnp.where(lane.astype(F32) == routes[s][0], 1.0, 0.0) for s in chains]
    within = [_dot(before, onehot[s].astype(BF16)) for s in chains]
    for s in chains:
        rank = jnp.sum(onehot[s] * (within[s] + carry_ref[...]), axis=-1, keepdims=True)
        info = jnp.where(lane == 0, routes[s][0], jnp.where(lane == 1, rank, 0.0))
        route_out[s] = jnp.transpose(info)[:8, :].astype(jnp.int32)
        carry_ref[...] += jnp.sum(onehot[s], axis=0, keepdims=True)
    counts_out[...] = carry_ref[...]


def _post_mixer(o, p, g, x, mod, lp, *, seq, n_sub):
    t = x.shape[0]
    tm = n_sub * POST_ROWS
    tiles_per_seq = max(seq // tm, 1)
    single_mod = mod.shape[0] == 1
    mod_idx = (lambda i: (0, 0, 0)) if single_mod else (lambda i: (i // tiles_per_seq, 0, 0))
    tile = lambda i: (i, 0)
    const2 = lambda i: (0, 0)
    const3 = lambda i: (0, 0, 0)
    p_spec = (pl.BlockSpec((tm, POOL_W), tile) if seq == POST_ROWS
              else pl.BlockSpec((seq, POOL_W), lambda i: (i // tiles_per_seq, 0)))
    in_specs = [pl.BlockSpec((tm, ATTN_W), tile),
                p_spec,
                pl.BlockSpec((tm, 2 * D_MODEL), tile),
                pl.BlockSpec((tm, D_MODEL), tile),
                pl.BlockSpec((None, ADA_CHUNKS, D_MODEL), mod_idx),
                pl.BlockSpec((ATTN_W, D_MODEL), const2, pipeline_mode=pl.Buffered(1)),
                pl.BlockSpec((POOL_GROUPS, POOL_GROUP_W, POOL_GROUP_W), const3),
                pl.BlockSpec((1, POOL_W), const2),
                pl.BlockSpec((POOL_W, D_MODEL), const2, pipeline_mode=pl.Buffered(1)),
                pl.BlockSpec((D_MODEL, D_MODEL), const2, pipeline_mode=pl.Buffered(1)),
                pl.BlockSpec((1, D_MODEL), const2),
                pl.BlockSpec((D_MODEL, ROUTER_W), const2),
                pl.BlockSpec((D_MODEL, ROUTER_W), const2),
                pl.BlockSpec((1, ROUTER_W), const2)]
    tile3 = lambda i: (i, 0, 0)
    out_shape = [jax.ShapeDtypeStruct((t, D_MODEL), F32),
                 jax.ShapeDtypeStruct((t, ROW_SUB, LANES), BF16),
                 jax.ShapeDtypeStruct((t, 1, LANES), F32),
                 jax.ShapeDtypeStruct((t // POST_ROWS, 8, POST_ROWS), jnp.int32),
                 jax.ShapeDtypeStruct((1, LANES), F32)]
    out_specs = [pl.BlockSpec((tm, D_MODEL), tile),
                 pl.BlockSpec((tm, ROW_SUB, LANES), tile3),
                 pl.BlockSpec((tm, 1, LANES), tile3),
                 pl.BlockSpec((n_sub, 8, POST_ROWS), tile3),
                 pl.BlockSpec((1, LANES), const2)]
    return pl.pallas_call(
        functools.partial(_post_kernel, seq=seq, n_sub=n_sub),
        grid=(t // tm,),
        in_specs=in_specs, out_specs=out_specs, out_shape=out_shape,
        scratch_shapes=[pltpu.VMEM((1, LANES), F32),
                        pltpu.VMEM((ATTN_W, D_MODEL), BF16),
                        pltpu.VMEM((POOL_W, D_MODEL), BF16),
                        pltpu.VMEM((D_MODEL, D_MODEL), BF16)],
        compiler_params=_cparams("arbitrary"),
        name="post_mixer",
    )(o, p, g, x, mod, lp["w_br_attn"], lp["pool_w"], lp["pool_scale"], lp["w_br_pool"], lp["w_out"],
      lp["norm2_g"], lp["w_router_hi"], lp["w_router_lo"], lp["b_router"])


def _moe_kernel(cls_s, rank_s, cnt_s, h_hbm, gw_hbm, wg_hbm, wu_hbm, wd_hbm, o_ref, dest_s,
                src_s, off_s, elo_s, nlo_s, ehi_s, nhi_s, valid_s, nt_s,
                h_ref, gw_ref, xg_ref, gwg_ref, wg_stage, wu_stage, wd_stage, wg_buf, wu_buf, wd_buf,
                sems, in_sems, *, n_tokens):
    j = pl.program_id(0)

    stages = ((wg_hbm, wg_stage, wg_buf), (wu_hbm, wu_stage, wu_buf), (wd_hbm, wd_stage, wd_buf))

    def weight_copies(e, role):
        return [pltpu.make_async_copy(hbm.at[e], stage.at[role], sems.at[role, i])
                for i, (hbm, stage, _) in enumerate(stages)]

    def cast_weights(role):
        for _, stage, buf in stages:
            for r in range(0, stage.shape[1], CAST_PIECE // 2):
                buf[role, r:r + CAST_PIECE // 2, :] = stage[role, r:r + CAST_PIECE // 2, :].astype(BF16)

    token_copies = [pltpu.make_async_copy(h_hbm, h_ref, in_sems.at[0]),
                    pltpu.make_async_copy(gw_hbm, gw_ref, in_sems.at[1])]
    roles = ((0, elo_s, nlo_s), (1, ehi_s, nhi_s))

    def gather_tile(tile, slot):
        base = tile * MOE_TILE
        for r in range(MOE_TILE):
            t = src_s[base + r]
            xg_ref[slot, r] = h_ref[t]
            gwg_ref[slot, r] = gw_ref[t]

    @pl.when(j == 0)
    def _():
        for cp in token_copies:
            cp.start()

        tile = 0
        for c in range(N_CLASSES):
            n_tok = cnt_s[c]
            n_cls_tiles = (n_tok + MOE_TILE - 1) // MOE_TILE
            off_s[c] = tile * MOE_TILE
            e_lo = (c // PAIRS_PER_GROUP) * EXPERTS_PER_GROUP + PAIR_LO[c % PAIRS_PER_GROUP]
            e_hi = (c // PAIRS_PER_GROUP) * EXPERTS_PER_GROUP + PAIR_HI[c % PAIRS_PER_GROUP]

            def fill(k, carry, tile=tile, n_tok=n_tok, e_lo=e_lo, e_hi=e_hi):
                elo_s[tile + k] = e_lo
                ehi_s[tile + k] = e_hi
                valid_s[tile + k] = jnp.minimum(n_tok - k * MOE_TILE, MOE_TILE)
                return carry

            lax.fori_loop(0, n_cls_tiles, fill, 0)
            tile = tile + n_cls_tiles
        off_s[N_CLASSES] = tile * MOE_TILE
        nt_s[0] = tile
        n_tiles = tile

        for role, e_s, nxt_s in roles:
            for cp in weight_copies(e_s[0], role):
                cp.start()

            def backward(i, nxt, e_s=e_s, nxt_s=nxt_s):
                t = n_tiles - 1 - i
                after = jnp.minimum(t + 1, n_tiles - 1)
                nxt = jnp.where((t + 1 < n_tiles) & (e_s[after] != e_s[t]), e_s[after], nxt)
                nxt_s[t] = nxt
                return nxt

            lax.fori_loop(0, n_tiles, backward, -1)

        def clear_tail(c, carry):
            start = jnp.maximum(off_s[c + 1] - MOE_TILE, 0)
            for i in range(MOE_TILE):
                src_s[start + i] = 0
            return carry

        lax.fori_loop(0, N_CLASSES, clear_tail, 0)

        def place(t, c):
            row = off_s[cls_s[t]] + rank_s[t]
            src_s[row] = t
            dest_s[t] = row
            return c

        lax.fori_loop(0, n_tokens, place, 0, unroll=16)
        for cp in token_copies:
            cp.wait()
        gather_tile(0, 0)

    n_tiles = nt_s[0]

    @pl.when(j < n_tiles)
    def _():
        for role, e_s, nxt_s in roles:
            run_starts = (j == 0) | (e_s[j] != e_s[jnp.maximum(j - 1, 0)])

            @pl.when(run_starts)
            def _():
                for cp in weight_copies(e_s[j], role):
                    cp.wait()
                cast_weights(role)

                @pl.when(nxt_s[j] >= 0)
                def _():
                    for cp in weight_copies(nxt_s[j], role):
                        cp.start()

        slot = j % 2
        gather_tile(jnp.minimum(j + 1, n_tiles - 1), 1 - slot)

        def experts(rows):
            x = xg_ref[slot, :rows].reshape(rows, D_MODEL)
            gw = gwg_ref[slot, :rows].reshape(rows, LANES)
            a1, a2 = _dot(x, wg_buf[0]), _dot(x, wg_buf[1])
            u1, u2 = _dot(x, wu_buf[0]), _dot(x, wu_buf[1])
            hid1 = (a1 * jax.nn.sigmoid(a1) * u1 * gw[:, 0:1]).astype(BF16)
            hid2 = (a2 * jax.nn.sigmoid(a2) * u2 * gw[:, 1:2]).astype(BF16)
            o = _dot(jnp.concatenate([hid1, hid2], axis=1), wd_buf[...].reshape(2 * D_EXPERT, D_MODEL))
            o_ref[:rows] = o.astype(BF16).reshape(rows, ROW_SUB, LANES)
            if rows < MOE_TILE:
                o_ref[rows:] = jnp.zeros((MOE_TILE - rows, ROW_SUB, LANES), BF16)

        pieces = (valid_s[j] + MOE_ROWS - 1) // MOE_ROWS
        for n in range(1, MOE_TILE // MOE_ROWS + 1):
            @pl.when(pieces == n)
            def _():
                experts(n * MOE_ROWS)

    @pl.when(j >= n_tiles)
    def _():
        o_ref[...] = jnp.zeros_like(o_ref)


def _moe(h2p, gw, cls, rank, cnt, lp):
    t, sub, _ = h2p.shape
    max_tiles = t // MOE_TILE + N_CLASSES
    in_hbm = pl.BlockSpec(memory_space=pl.ANY)
    tile_table = pltpu.SMEM((max_tiles,), jnp.int32)
    grid_spec = pltpu.PrefetchScalarGridSpec(
        num_scalar_prefetch=3,
        grid=(max_tiles,),
        in_specs=[in_hbm] * 5,
        out_specs=[pl.BlockSpec((MOE_TILE, sub, LANES), lambda j, *_: (j, 0, 0)),
                   pl.BlockSpec(memory_space=pltpu.SMEM)],
        scratch_shapes=[pltpu.SMEM((max_tiles * MOE_TILE,), jnp.int32),
                        pltpu.SMEM((N_CLASSES + 1,), jnp.int32)]
        + [tile_table] * 5
        + [pltpu.SMEM((1,), jnp.int32),
           pltpu.VMEM((t, sub, LANES), BF16),
           pltpu.VMEM((t, 1, LANES), F32),
           pltpu.VMEM((2, MOE_TILE, sub, LANES), BF16),
           pltpu.VMEM((2, MOE_TILE, 1, LANES), F32),
           pltpu.VMEM((2, D_MODEL, D_EXPERT), F32),
           pltpu.VMEM((2, D_MODEL, D_EXPERT), F32),
           pltpu.VMEM((2, D_EXPERT, D_MODEL), F32),
           pltpu.VMEM((2, D_MODEL, D_EXPERT), BF16),
           pltpu.VMEM((2, D_MODEL, D_EXPERT), BF16),
           pltpu.VMEM((2, D_EXPERT, D_MODEL), BF16),
           pltpu.SemaphoreType.DMA((2, 3)),
           pltpu.SemaphoreType.DMA((2,))])
    return pl.pallas_call(
        functools.partial(_moe_kernel, n_tokens=t),
        grid_spec=grid_spec,
        out_shape=[jax.ShapeDtypeStruct((max_tiles * MOE_TILE, sub, LANES), BF16),
                   jax.ShapeDtypeStruct((t,), jnp.int32)],
        compiler_params=_cparams("arbitrary"),
        name="moe",
    )(cls, rank, cnt, h2p, gw, lp["expert_w_gate"], lp["expert_w_up"], lp["expert_w_down"])


def _combine_kernel(dest_s, o_ref, x1_ref, mod_ref, y_ref, og_ref, *, tm):
    base = pl.program_id(0) * tm

    def gather(r, c):
        og_ref[r] = o_ref[dest_s[base + r]]
        return c

    lax.fori_loop(0, tm, gather, 0, unroll=32)
    moe = og_ref[...].reshape(tm, D_MODEL).astype(F32)
    y_ref[...] = x1_ref[...] + mod_ref[5:6, :] * moe


def _combine(o_sorted, x1, mod, dest, *, seq, tm):
    t = x1.shape[0]
    rows, sub, _ = o_sorted.shape
    tiles_per_seq = seq // tm
    single_mod = mod.shape[0] == 1
    mod_idx = (lambda i, *_: (0, 0, 0)) if single_mod else (lambda i, *_: (i // tiles_per_seq, 0, 0))
    row = lambda i, *_: (i, 0)
    grid_spec = pltpu.PrefetchScalarGridSpec(
        num_scalar_prefetch=1,
        grid=(t // tm,),
        in_specs=[pl.BlockSpec((rows, sub, LANES), lambda i, *_: (0, 0, 0), pipeline_mode=pl.Buffered(1)),
                  pl.BlockSpec((tm, D_MODEL), row),
                  pl.BlockSpec((None, ADA_CHUNKS, D_MODEL), mod_idx)],
        out_specs=pl.BlockSpec((tm, D_MODEL), row),
        scratch_shapes=[pltpu.VMEM((tm, sub, LANES), BF16)])
    return pl.pallas_call(
        functools.partial(_combine_kernel, tm=tm),
        grid_spec=grid_spec,
        out_shape=jax.ShapeDtypeStruct((t, D_MODEL), F32),
        compiler_params=_cparams("arbitrary"),
        name="moe_combine",
    )(dest, o_sorted, x1, mod)


def _rope_tables(n_tokens):
    rows = n_tokens // GRID_W
    row_ids = np.repeat(np.arange(rows, dtype=np.float32), GRID_W)
    col_ids = np.tile(np.arange(GRID_W, dtype=np.float32), rows)
    inv_freq = np.power(np.float32(ROPE_THETA), -np.arange(ROPE_PAIRS, dtype=np.float32) / np.float32(ROPE_PAIRS))
    ang_r = row_ids[:, None] * inv_freq[None, :]
    ang_c = col_ids[:, None] * inv_freq[None, :]
    ang = np.concatenate([ang_r, ang_r, ang_c, ang_c] * 2, axis=-1)
    return jnp.asarray(np.cos(ang), F32), jnp.asarray(np.sin(ang), F32)


def _layer(x, mod, lp, rope_tabs, cache, lam_rows, *, kv_dtype):
    b, n, _ = x.shape
    t = b * n
    q, k, v, p, g = _pre_mixer(x.reshape(t, D_MODEL), mod, lp, rope_tabs, seq=n, tm=ROW_TILE, kv_dtype=kv_dtype)
    o = _attention(q.reshape(b, n, ATTN_W), k.reshape(b, n, ATTN_W), v.reshape(b, n, ATTN_W),
                   cache, lam_rows, lp["subln_g"], tq=ROW_TILE, heads_per_step=N_HEADS if cache is None else 2)
    x1, h2p, gw, route, counts = _post_mixer(o.reshape(t, ATTN_W), p, g, x.reshape(t, D_MODEL), mod, lp,
                                             seq=n, n_sub=2)
    cls, rank = route[:, 0, :].reshape(t), route[:, 1, :].reshape(t)
    o_sorted, dest = _moe(h2p, gw, cls, rank, counts[0, :N_CLASSES].astype(jnp.int32), lp)
    y = _combine(o_sorted, x1, mod, dest, seq=n, tm=ROW_TILE)
    return y.reshape(b, n, D_MODEL), k, v


def kernel(x_prompt, x_sample, c, cache_k, cache_v, c_ctx, w_ada, b_ada, norm1_g, w_in, b_gate, q_norm_g, k_norm_g, lambda_q1, lambda_k1, lambda_q2, lambda_k2, subln_g, pool_w, pool_scale, w_br_attn, w_br_pool, w_out, norm2_g, router_group_w, router_group_b, router_expert_w, router_expert_b, expert_w_gate, expert_w_up, expert_w_down):
    b_ctx, n_ctx, _ = x_prompt.shape
    b_lat, n_lat, _ = x_sample.shape
    assert w_ada.shape[0] == 1 and cache_k.shape[1] == 1, "one layer"
    assert n_ctx == POST_ROWS and n_lat % (2 * POST_ROWS) == 0 and n_lat % GRID_W == 0

    def router_layout(we, wg):
        pad = jnp.zeros(we.shape[:-1] + (LANES - N_EXPERTS,), F32)
        return jnp.concatenate([we, pad, jnp.repeat(wg, EXPERTS_PER_GROUP, axis=-1), pad], axis=-1)

    w_router = router_layout(router_expert_w[0], router_group_w[0])
    w_router_hi, w_router_lo = _split_bf16(w_router)
    seg = jnp.asarray(np.arange(MXU_W)[:, None] // HEAD_DIM == np.arange(MXU_W)[None, :] // HEAD_DIM)
    lp = dict(
        norm1_g=norm1_g[0].reshape(1, D_MODEL),
        w_in=w_in[0],
        b_gate=b_gate[0].reshape(1, 2 * D_MODEL),
        q_gain=jnp.tile(q_norm_g[0], ATTN_W // HEAD_DIM).reshape(1, ATTN_W),
        k_gain=jnp.tile(k_norm_g[0], ATTN_W // HEAD_DIM).reshape(1, ATTN_W),
        seg=(seg.astype(F32) / HEAD_DIM).astype(BF16),
        subln_g=subln_g[0].reshape(1, HEAD_W),
        w_br_attn=w_br_attn[0],
        pool_w=pool_w[0].astype(BF16),
        pool_scale=pool_scale[0].reshape(1, POOL_W),
        w_br_pool=w_br_pool[0],
        w_out=w_out[0],
        norm2_g=norm2_g[0].reshape(1, D_MODEL),
        w_router_hi=w_router_hi, w_router_lo=w_router_lo,
        b_router=router_layout(router_expert_b[0], router_group_b[0]).reshape(1, ROUTER_W),
        expert_w_gate=expert_w_gate[0],
        expert_w_up=expert_w_up[0],
        expert_w_down=expert_w_down[0],
    )
    lam_rows = jnp.stack([lambda_q1[0], lambda_k1[0], lambda_q2[0], lambda_k2[0]], axis=0)

    mod = _adaln(c_ctx, c, w_ada[0], b_ada[0]).reshape(2 * b_lat, ADA_CHUNKS, D_MODEL)

    y_prompt, k_ctx, v_ctx = _layer(x_prompt, mod[:1], lp, None, None, lam_rows, kv_dtype=F32)
    cache = (cache_k, cache_v)
    y_sample, _, _ = _layer(x_sample, mod[b_lat:], lp, _rope_tables(n_lat), cache, lam_rows, kv_dtype=BF16)

    new_cache_k = k_ctx.reshape(b_ctx, 1, n_ctx, N_HEADS, HEAD_W)
    new_cache_v = v_ctx.reshape(b_ctx, 1, n_ctx, N_HEADS, HEAD_W)
    return (y_prompt, y_sample, new_cache_k, new_cache_v)
```
